```python
import math
import jax, jax.numpy as jnp
from jax import lax
import numpy as np

D_MODEL = 2048
BATCH = 2
SEQ = 4096
DEPTH = 1

N_HEADS = 16
HEAD_DIM = D_MODEL // N_HEADS
N_KV_GROUPS = 4
GROUP_SIZE = N_HEADS // N_KV_GROUPS
Q_DIM = N_HEADS * HEAD_DIM
KV_DIM = N_KV_GROUPS * HEAD_DIM
CMP_LEN = 32
CMP_STRIDE = 16
SEL_BLOCK = 64
SEL_TOPN = 16
WINDOW = 512
Q_BLOCK = 64
SSM_CH = D_MODEL
SSM_GROUP = 16
SSM_NGROUPS = SSM_CH // SSM_GROUP
SSM_STATE = 64
DT_MIN = 1e-3
DT_MAX = 1e-1
D_FF = 4 * D_MODEL
EPS = 1e-6
NEG = -1e30
BIG = 1e6

PROJ_WIDTHS = [Q_DIM, KV_DIM, KV_DIM, KV_DIM, KV_DIM, KV_DIM, KV_DIM,
               3 * N_HEADS, SSM_CH, 2 * D_MODEL]
PROJ_COLS = sum(PROJ_WIDTHS)
PROJ_SPLITS = [int(v) for v in np.cumsum(PROJ_WIDTHS)[:-1]]

kernel_name = "hybrid_nsa_s5_gated_block"


def rmsnorm(x, g):
    x32 = x.astype(jnp.float32)
    y = x32 * lax.rsqrt(jnp.mean(x32 * x32, axis=-1, keepdims=True) + EPS)
    return (y * g.astype(jnp.float32)).astype(x.dtype)


def masked_softmax(s, mask):
    s = jnp.where(mask, s.astype(jnp.float32), NEG)
    m = jnp.max(s, axis=-1, keepdims=True)
    p = jnp.where(mask, jnp.exp(s - m), 0.0)
    return p / jnp.maximum(jnp.sum(p, axis=-1, keepdims=True), 1e-30)


def nsa_attention(q, k_cmp, v_cmp, k_sel, v_sel, k_win, v_win, gates,
                  w_ck1, w_ck2, pe_ck, w_cv1, w_cv2, pe_cv):
    bsz, seq = q.shape[0], q.shape[1]
    G, R, hd = N_KV_GROUPS, GROUP_SIZE, HEAD_DIM
    n_cmp = (seq - CMP_LEN) // CMP_STRIDE + 1
    n_sel = seq // SEL_BLOCK
    topn = min(SEL_TOPN, n_sel)
    scale = hd ** -0.5
    qg = q.reshape(bsz, seq, G, R, hd)
    gg = gates.reshape(bsz, seq, G, R, 3)
    kvs = [a.reshape(bsz, seq, G, hd) for a in (k_cmp, v_cmp, k_sel, v_sel, k_win, v_win)]
    k_cmp, v_cmp, k_sel, v_sel, k_win, v_win = kvs

    win_idx = jnp.arange(n_cmp)[:, None] * CMP_STRIDE + jnp.arange(CMP_LEN)[None, :]

    def compress(t, w1, w2, pe):
        blocks = t[:, win_idx] + pe[None, None, :, None, :]
        hid = jax.nn.gelu(jnp.einsum('bnlgd,lde->bnge', blocks, w1))
        return jnp.einsum('bnge,ef->bngf', hid, w2)

    kc = compress(k_cmp, w_ck1, w_ck2, pe_ck)
    vc = compress(v_cmp, w_cv1, w_cv2, pe_cv)
    cmp_start = jnp.arange(n_cmp) * CMP_STRIDE
    cmp_end = cmp_start + CMP_LEN - 1
    sel_start = jnp.arange(n_sel) * SEL_BLOCK
    overlap = ((cmp_start[:, None] < sel_start[None, :] + SEL_BLOCK)
               & (cmp_start[:, None] + CMP_LEN > sel_start[None, :])).astype(jnp.float32)

    ks_blocks = k_sel.reshape(bsz, n_sel, SEL_BLOCK, G, hd).transpose(0, 3, 1, 2, 4)
    vs_blocks = v_sel.reshape(bsz, n_sel, SEL_BLOCK, G, hd).transpose(0, 3, 1, 2, 4)
    pad = ((0, 0), (WINDOW, 0), (0, 0), (0, 0))
    kw_pad = jnp.pad(k_win, pad)
    vw_pad = jnp.pad(v_win, pad)
    gather = jax.vmap(jax.vmap(lambda blocks, ix: blocks[ix]))
    jj = jnp.arange(n_sel)

    def one_block(s0):
        t = s0 + jnp.arange(Q_BLOCK)
        qb = lax.dynamic_slice_in_dim(qg, s0, Q_BLOCK, axis=1) * scale
        gb = jax.nn.sigmoid(lax.dynamic_slice_in_dim(gg, s0, Q_BLOCK, axis=1))
        s_c = jnp.einsum('bqgrd,bngd->bgrqn', qb, kc)
        p_c = masked_softmax(s_c, cmp_end[None, :] <= t[:, None])
        o_c = jnp.einsum('bgrqn,bngd->bqgrd', p_c.astype(vc.dtype), vc)
        imp = jnp.einsum('bgrqn,nj->bgqj', p_c, overlap)
        cur = t // SEL_BLOCK
        valid = jj[None, :] <= cur[:, None]
        forced = (jj[None, :] == 0) | (jj[None, :] == cur[:, None]) | (jj[None, :] == cur[:, None] - 1)
        score = jnp.where(forced, BIG, jnp.where(valid, imp, -BIG))
        _, idx = lax.top_k(score, topn)
        sel_valid = jnp.take_along_axis(jnp.broadcast_to(valid, score.shape), idx, axis=-1)
        ksel = gather(ks_blocks, idx)
        vsel = gather(vs_blocks, idx)
        s_s = jnp.einsum('bqgrd,bgqnkd->bgrqnk', qb, ksel)
        kpos = idx[..., None] * SEL_BLOCK + jnp.arange(SEL_BLOCK)
        m_s = sel_valid[..., None] & (kpos <= t[None, None, :, None, None])
        p_s = masked_softmax(s_s.reshape(bsz, G, R, Q_BLOCK, topn * SEL_BLOCK),
                             m_s.reshape(bsz, G, 1, Q_BLOCK, topn * SEL_BLOCK))
        o_s = jnp.einsum('bgrqnk,bgqnkd->bqgrd',
                         p_s.reshape(bsz, G, R, Q_BLOCK, topn, SEL_BLOCK).astype(vsel.dtype), vsel)
        kw = lax.dynamic_slice_in_dim(kw_pad, s0, WINDOW + Q_BLOCK, axis=1)
        vw = lax.dynamic_slice_in_dim(vw_pad, s0, WINDOW + Q_BLOCK, axis=1)
        kp = s0 - WINDOW + jnp.arange(WINDOW + Q_BLOCK)
        diff = t[:, None] - kp[None, :]
        m_w = (diff >= 0) & (diff < WINDOW) & (kp[None, :] >= 0)
        s_w = jnp.einsum('bqgrd,bkgd->bgrqk', qb, kw)
        p_w = masked_softmax(s_w, m_w)
        o_w = jnp.einsum('bgrqk,bkgd->bqgrd', p_w.astype(vw.dtype), vw)
        return gb[..., 0:1] * o_c + gb[..., 1:2] * o_s + gb[..., 2:3] * o_w

    outs = lax.map(one_block, jnp.arange(seq // Q_BLOCK) * Q_BLOCK)
    return outs.transpose(1, 0, 2, 3, 4, 5).reshape(bsz, seq, Q_DIM)


def _complex_scan_op(left, right):
    a1r, a1i, b1r, b1i = left
    a2r, a2i, b2r, b2i = right
    return (a2r * a1r - a2i * a1i,
            a2r * a1i + a2i * a1r,
            a2r * b1r - a2i * b1i + b2r,
            a2r * b1i + a2i * b1r + b2i)


def s5_ssm(u, a_re, a_im, log_dt, b_re, b_im, c_re, c_im, d_skip):
    bsz, seq = u.shape[0], u.shape[1]
    ug = u.reshape(bsz, seq, SSM_NGROUPS, SSM_GROUP).astype(jnp.float32)
    are = a_re.astype(jnp.float32)
    aim = a_im.astype(jnp.float32)
    dt = jnp.exp(log_dt.astype(jnp.float32))[:, None]
    mag = jnp.exp(are * dt)
    lam_re = mag * jnp.cos(aim * dt)
    lam_im = mag * jnp.sin(aim * dt)
    den = are * are + aim * aim
    nr = lam_re - 1.0
    coef_re = (nr * are + lam_im * aim) / den
    coef_im = (lam_im * are - nr * aim) / den
    br = b_re.astype(jnp.float32)
    bi = b_im.astype(jnp.float32)
    bb_re = coef_re[..., None] * br - coef_im[..., None] * bi
    bb_im = coef_re[..., None] * bi + coef_im[..., None] * br
    bu_re = jnp.einsum('gpc,bsgc->bsgp', bb_re, ug)
    bu_im = jnp.einsum('gpc,bsgc->bsgp', bb_im, ug)
    a_r = jnp.broadcast_to(lam_re, bu_re.shape)
    a_i = jnp.broadcast_to(lam_im, bu_re.shape)
    _, _, x_re, x_im = lax.associative_scan(_complex_scan_op, (a_r, a_i, bu_re, bu_im), axis=1)
    y = (jnp.einsum('gcp,bsgp->bsgc', c_re.astype(jnp.float32), x_re)
         - jnp.einsum('gcp,bsgp->bsgc', c_im.astype(jnp.float32), x_im))
    y = y + d_skip.astype(jnp.float32).reshape(SSM_NGROUPS, SSM_GROUP) * ug
    return y.reshape(bsz, seq, SSM_CH).astype(u.dtype)


def setup_inputs(seed: int = 0) -> dict:
    key = jax.random.key(seed)
    ks = jax.random.split(key, 28)
    L = DEPTH
    nrm = lambda k, shape, s: jax.random.normal(k, shape, jnp.float32) * s
    x = nrm(ks[0], (BATCH, SEQ, D_MODEL), 1.0)
    c = nrm(ks[1], (BATCH, D_MODEL), 1.0)
    w_ada = nrm(ks[2], (L, D_MODEL, 6 * D_MODEL), 0.02)
    b_ada = nrm(ks[3], (L, 6 * D_MODEL), 0.02)
    g_mix = 1.0 + nrm(ks[4], (L, D_MODEL), 0.02)
    w_in = nrm(ks[5], (L, D_MODEL, PROJ_COLS), D_MODEL ** -0.5)
    w_ck1 = nrm(ks[6], (L, CMP_LEN, HEAD_DIM, HEAD_DIM), (CMP_LEN * HEAD_DIM) ** -0.5)
    w_ck2 = nrm(ks[7], (L, HEAD_DIM, HEAD_DIM), HEAD_DIM ** -0.5)
    pe_ck = nrm(ks[8], (L, CMP_LEN, HEAD_DIM), 0.1)
    w_cv1 = nrm(ks[9], (L, CMP_LEN, HEAD_DIM, HEAD_DIM), (CMP_LEN * HEAD_DIM) ** -0.5)
    w_cv2 = nrm(ks[10], (L, HEAD_DIM, HEAD_DIM), HEAD_DIM ** -0.5)
    pe_cv = nrm(ks[11], (L, CMP_LEN, HEAD_DIM), 0.1)
    a_re = -0.5 + nrm(ks[12], (L, SSM_NGROUPS, SSM_STATE), 0.01)
    a_im = jnp.broadcast_to(math.pi * jnp.arange(SSM_STATE, dtype=jnp.float32),
                            (L, SSM_NGROUPS, SSM_STATE)) + 0.0
    log_dt = jax.random.uniform(ks[13], (L, SSM_NGROUPS), jnp.float32,
                                math.log(DT_MIN), math.log(DT_MAX))
    b_re = nrm(ks[14], (L, SSM_NGROUPS, SSM_STATE, SSM_GROUP), (2 * SSM_GROUP) ** -0.5)
    b_im = nrm(ks[15], (L, SSM_NGROUPS, SSM_STATE, SSM_GROUP), (2 * SSM_GROUP) ** -0.5)
    c_re = nrm(ks[16], (L, SSM_NGROUPS, SSM_GROUP, SSM_STATE), (SSM_STATE) ** -0.5)
    c_im = nrm(ks[17], (L, SSM_NGROUPS, SSM_GROUP, SSM_STATE), (SSM_STATE) ** -0.5)
    d_skip = nrm(ks[18], (L, SSM_CH), 1.0)
    w_glu = nrm(ks[19], (L, SSM_CH, D_MODEL), SSM_CH ** -0.5)
    b_glu = nrm(ks[20], (L, D_MODEL), 0.02)
    w_out = nrm(ks[21], (L, D_MODEL, D_MODEL), D_MODEL ** -0.5)
    g_mlp = 1.0 + nrm(ks[22], (L, D_MODEL), 0.02)
    w_up = nrm(ks[23], (L, D_MODEL, D_FF), D_MODEL ** -0.5)
    w_down = nrm(ks[24], (L, D_FF, D_MODEL), D_FF ** -0.5)
    g_final = 1.0 + nrm(ks[25], (D_MODEL,), 0.02)
    return {"x": x, "c": c, "w_ada": w_ada, "b_ada": b_ada, "g_mix": g_mix, "w_in": w_in,
            "w_ck1": w_ck1, "w_ck2": w_ck2, "pe_ck": pe_ck,
            "w_cv1": w_cv1, "w_cv2": w_cv2, "pe_cv": pe_cv,
            "a_re": a_re, "a_im": a_im, "log_dt": log_dt, "b_re": b_re, "b_im": b_im,
            "c_re": c_re, "c_im": c_im, "d_skip": d_skip, "w_glu": w_glu, "b_glu": b_glu,
            "w_out": w_out, "g_mlp": g_mlp, "w_up": w_up, "w_down": w_down, "g_final": g_final}


def reference(x, c, w_ada, b_ada, g_mix, w_in, w_ck1, w_ck2, pe_ck, w_cv1, w_cv2, pe_cv,
              a_re, a_im, log_dt, b_re, b_im, c_re, c_im, d_skip, w_glu, b_glu,
              w_out, g_mlp, w_up, w_down, g_final):
    h = x
    cond = jax.nn.silu(c)
    for l in range(DEPTH):
        ada = (cond @ w_ada[l] + b_ada[l])[:, None, :]
        sh1, sc1, gt1, sh2, sc2, gt2 = jnp.split(ada, 6, axis=-1)
        u = rmsnorm(h, g_mix[l]) * (1.0 + sc1) + sh1
        proj = u @ w_in[l]
        q, kc, vc, ks_, vs_, kw, vw, g_nsa, u_ssm, g_merge = jnp.split(proj, PROJ_SPLITS, axis=-1)
        y_a = nsa_attention(q, kc, vc, ks_, vs_, kw, vw, g_nsa,
                            w_ck1[l], w_ck2[l], pe_ck[l], w_cv1[l], w_cv2[l], pe_cv[l])
        y_s = jax.nn.gelu(s5_ssm(u_ssm, a_re[l], a_im[l], log_dt[l], b_re[l], b_im[l],
                                 c_re[l], c_im[l], d_skip[l]))
        y_b = y_s * jax.nn.sigmoid(y_s @ w_glu[l] + b_glu[l])
        g_a, g_b = jnp.split(jax.nn.sigmoid(g_merge), 2, axis=-1)
        h = h + gt1 * ((g_a * y_a + g_b * y_b) @ w_out[l])
        u2 = rmsnorm(h, g_mlp[l]) * (1.0 + sc2) + sh2
        h = h + gt2 * (jnp.square(jax.nn.relu(u2 @ w_up[l])) @ w_down[l])
    return rmsnorm(h, g_final)
```

```python
import functools

import jax
import jax.numpy as jnp
from jax import lax
from jax.experimental import pallas as pl
from jax.experimental.pallas import tpu as pltpu

F32 = jnp.float32
BF16 = jnp.bfloat16

D_MODEL = 2048
N_HEADS = 16
HEAD_DIM = 128
N_KV_GROUPS = 4
GROUP_SIZE = 4
KV_DIM = N_KV_GROUPS * HEAD_DIM
CMP_LEN = 32
CMP_STRIDE = 16
SEL_BLOCK = 64
SEL_TOPN = 16
WINDOW = 512
SSM_GROUP = 16
SSM_NGROUPS = D_MODEL // SSM_GROUP
SSM_STATE = 64
EPS = 1e-6
NEG = -1e30
BIG = 1e6

LANES = 128
SSM_CHUNK = 16
GROUPS_PER_BLOCK = LANES // SSM_GROUP
N_SSM_BLOCKS = SSM_NGROUPS // GROUPS_PER_BLOCK

OFF_Q = 0
OFF_KC = OFF_Q + D_MODEL
OFF_VC = OFF_KC + KV_DIM
OFF_KS = OFF_VC + KV_DIM
OFF_VS = OFF_KS + KV_DIM
OFF_KW = OFF_VS + KV_DIM
OFF_VW = OFF_KW + KV_DIM
OFF_GATE = OFF_VW + KV_DIM
OFF_SSM = OFF_GATE + N_KV_GROUPS * LANES
OFF_GA = OFF_SSM + D_MODEL
OFF_GB = OFF_GA + D_MODEL
PROJ_PAD = OFF_GB + D_MODEL
N_GATE_RAW = 3 * N_HEADS
OFF_GATE_RAW = OFF_VW + KV_DIM
OFF_SSM_RAW = OFF_GATE_RAW + N_GATE_RAW

VMEM_LIMIT = 56 * 1024 * 1024


def _cparams(sem):
    return pltpu.CompilerParams(dimension_semantics=sem, vmem_limit_bytes=VMEM_LIMIT)


def _nt_dot(a, b):
    return lax.dot_general(a, b, (((1,), (1,)), ((), ())), preferred_element_type=F32)


def _ada_kernel(c_ref, w_ref, b_ref, o_ref):
    c = c_ref[...]
    cond = c * jax.nn.sigmoid(c)
    o_ref[...] = jnp.dot(cond.astype(BF16), w_ref[...].astype(BF16),
                         preferred_element_type=F32) + b_ref[...]


def _ada(c_pad, w_ada, b_ada):
    n = w_ada.shape[1]
    tn = 1024
    return pl.pallas_call(
        _ada_kernel,
        grid=(n // tn,),
        in_specs=[pl.BlockSpec((8, D_MODEL), lambda j: (0, 0)),
                  pl.BlockSpec((D_MODEL, tn), lambda j: (0, j)),
                  pl.BlockSpec((1, tn), lambda j: (0, j))],
        out_specs=pl.BlockSpec((8, tn), lambda j: (0, j)),
        out_shape=jax.ShapeDtypeStruct((8, n), F32),
        compiler_params=_cparams(("parallel",)),
        name="ada",
    )(c_pad, w_ada, b_ada)


def _inproj_kernel(x_ref, g_ref, sc_ref, sh_ref, w_ref, o_ref, u_scr):
    @pl.when(pl.program_id(1) == 0)
    def _():
        x = x_ref[...]
        ms = jnp.mean(x * x, axis=-1, keepdims=True)
        y = x * lax.rsqrt(ms + EPS) * g_ref[...]
        u_scr[...] = (y * (1.0 + sc_ref[...]) + sh_ref[...]).astype(BF16)

    o_ref[...] = jnp.dot(u_scr[...], w_ref[...], preferred_element_type=F32).astype(o_ref.dtype)


def _inproj(x2, g_mix, ada3, w_p, seq):
    m = x2.shape[0]
    tm, tn = 1024, 512
    per_b = seq // tm
    return pl.pallas_call(
        _inproj_kernel,
        grid=(m // tm, PROJ_PAD // tn),
        in_specs=[pl.BlockSpec((tm, D_MODEL), lambda i, j: (i, 0)),
                  pl.BlockSpec((1, D_MODEL), lambda i, j: (0, 0)),
                  pl.BlockSpec((None, 1, D_MODEL), lambda i, j: ((i // per_b) * 6 + 1, 0, 0)),
                  pl.BlockSpec((None, 1, D_MODEL), lambda i, j: ((i // per_b) * 6 + 0, 0, 0)),
                  pl.BlockSpec((D_MODEL, tn), lambda i, j: (0, j))],
        out_specs=pl.BlockSpec((tm, tn), lambda i, j: (i, j)),
        out_shape=jax.ShapeDtypeStruct((m, PROJ_PAD), BF16),
        scratch_shapes=[pltpu.VMEM((tm, D_MODEL), BF16)],
        compiler_params=_cparams(("parallel", "arbitrary")),
        name="inproj",
    )(x2, g_mix, ada3, ada3, w_p)


def _compress_kernel(x_ref, w1_ref, pe_ref, w2_ref, o_ref):
    x = x_ref[...]
    nc = x.shape[0]
    half = CMP_STRIDE * HEAD_DIM
    w1 = w1_ref[...]
    ha = jnp.dot(x, w1[:half], preferred_element_type=F32)
    hb = jnp.dot(x, w1[half:], preferred_element_type=F32)
    hb = pltpu.roll(hb, nc - 1, axis=0)
    pe_t = jnp.dot(pe_ref[...], w1, preferred_element_type=F32)[0:1]
    hid = jax.nn.gelu(ha + hb + pe_t)
    out = jnp.dot(hid.astype(BF16), w2_ref[...], preferred_element_type=F32)
    row = lax.broadcasted_iota(jnp.int32, out.shape, 0)
    o_ref[...] = jnp.where(row < nc - 1, out, 0.0).astype(o_ref.dtype)


def _compress(xc, w1s, pes, w2s):
    _, b, g, nc, kk = xc.shape
    return pl.pallas_call(
        _compress_kernel,
        grid=(2, b, g),
        in_specs=[pl.BlockSpec((None, None, None, nc, kk), lambda a, i, j: (a, i, j, 0, 0)),
                  pl.BlockSpec((None, CMP_LEN * HEAD_DIM, HEAD_DIM), lambda a, i, j: (a, 0, 0)),
                  pl.BlockSpec((None, 8, CMP_LEN * HEAD_DIM), lambda a, i, j: (a, 0, 0)),
                  pl.BlockSpec((None, HEAD_DIM, HEAD_DIM), lambda a, i, j: (a, 0, 0))],
        out_specs=pl.BlockSpec((None, None, None, nc, HEAD_DIM), lambda a, i, j: (a, i, j, 0, 0)),
        out_shape=jax.ShapeDtypeStruct((2, b, g, nc, HEAD_DIM), BF16),
        compiler_params=_cparams(("parallel", "parallel", "parallel")),
        name="compress",
    )(xc, w1s, pes, w2s)


def _nsa_kernel(q_ref, gate_ref, kc_ref, vc_ref, ks_ref, vs_ref, kw_ref, vw_ref, o_ref,
                *, tq, tk, seq):
    s0 = pl.program_id(2) * tq
    rows = GROUP_SIZE * tq
    nsel = seq // SEL_BLOCK
    topn = min(SEL_TOPN, nsel)
    scale = HEAD_DIM ** -0.5

    q = q_ref[...]
    q4 = jnp.concatenate([q[:, r * HEAD_DIM:(r + 1) * HEAD_DIM] for r in range(GROUP_SIZE)], axis=0)
    q4 = (q4.astype(F32) * scale).astype(BF16)
    t_row = s0 + (lax.broadcasted_iota(jnp.int32, (rows, 1), 0) & (tq - 1))

    kc = kc_ref[...]
    nc = kc.shape[0]
    sc = _nt_dot(q4, kc)
    n_idx = lax.broadcasted_iota(jnp.int32, (1, nc), 1)
    valid_c = (n_idx * CMP_STRIDE + (CMP_LEN - 1)) <= t_row
    sc = jnp.where(valid_c, sc, NEG)
    mc = jnp.max(sc, axis=-1, keepdims=True)
    pc = jnp.where(valid_c, jnp.exp(sc - mc), 0.0)
    pc = pc / jnp.maximum(jnp.sum(pc, axis=-1, keepdims=True), 1e-30)
    o_c = jnp.dot(pc.astype(BF16), vc_ref[...], preferred_element_type=F32)
    psum = pc[0:tq] + pc[tq:2 * tq] + pc[2 * tq:3 * tq] + pc[3 * tq:4 * tq]

    jj = lax.broadcasted_iota(jnp.int32, (nsel, nc), 0)
    nn = lax.broadcasted_iota(jnp.int32, (nsel, nc), 1)
    ov = ((nn * CMP_STRIDE < jj * SEL_BLOCK + SEL_BLOCK)
          & (nn * CMP_STRIDE + CMP_LEN > jj * SEL_BLOCK)
          & (nn < nc - 1))
    ov = jnp.where(ov, 1.0, 0.0).astype(BF16)
    p_hi = psum.astype(BF16)
    p_lo = (psum - p_hi.astype(F32)).astype(BF16)
    imp_t = _nt_dot(ov, p_hi) + _nt_dot(ov, p_lo)

    jcol = lax.broadcasted_iota(jnp.int32, (nsel, tq), 0)
    cur = lax.shift_right_logical(s0 + lax.broadcasted_iota(jnp.int32, (nsel, tq), 1), 6)
    valid = jcol <= cur
    forced = (jcol == 0) | (jcol == cur) | (jcol == cur - 1)
    score = jnp.where(forced, BIG, jnp.where(valid, imp_t, -BIG))

    slabs = []
    for sb in range(nsel // 8):
        sj = score[sb * 8:(sb + 1) * 8]
        jr = jcol[sb * 8:(sb + 1) * 8]
        rank = jnp.zeros((8, tq), F32)
        for i in range(nsel):
            si = score[i:i + 1]
            if i < sb * 8:
                cond = si >= sj
            elif i >= sb * 8 + 8:
                cond = si > sj
            else:
                cond = (si > sj) | ((jr > i) & (si == sj))
            rank = rank + jnp.where(cond, 1.0, 0.0)
        slabs.append(rank)
    rank = jnp.concatenate(slabs, axis=0)
    sel_t = jnp.where((rank < topn) & valid, 1.0, 0.0)
    sel_t = jnp.concatenate([sel_t, jnp.zeros((LANES - nsel, tq), F32)], axis=0)
    sel_q = sel_t.T.astype(BF16)

    def flash(lo, hi, k_ref, v_ref, mask_fn):
        def body(kb, carry):
            m, l, acc = carry
            off = pl.multiple_of(kb * tk, tk)
            k = k_ref[pl.ds(off, tk), :]
            v = v_ref[pl.ds(off, tk), :]
            s = _nt_dot(q4, k)
            kpos = off + lax.broadcasted_iota(jnp.int32, (1, tk), 1)
            mask = mask_fn(kpos)
            s = jnp.where(mask, s, NEG)
            m_new = jnp.maximum(m, jnp.max(s, axis=-1, keepdims=True))
            alpha = jnp.exp(m - m_new)
            p = jnp.where(mask, jnp.exp(s - m_new), 0.0)
            l = alpha * l + jnp.sum(p, axis=-1, keepdims=True)
            acc = alpha * acc + jnp.dot(p.astype(BF16), v, preferred_element_type=F32)
            return m_new, l, acc

        init = (jnp.full((rows, 1), NEG, F32), jnp.zeros((rows, 1), F32),
                jnp.zeros((rows, HEAD_DIM), F32))
        _, l, acc = lax.fori_loop(lo, hi, body, init)
        return acc / jnp.maximum(l, 1e-30)

    jrow = lax.broadcasted_iota(jnp.int32, (LANES, 1), 0)

    def sel_mask(kpos):
        expand = jnp.where(lax.shift_right_logical(kpos, 6) == jrow, 1.0, 0.0).astype(BF16)
        hit = jnp.dot(sel_q, expand, preferred_element_type=F32)
        hit = jnp.concatenate([hit] * GROUP_SIZE, axis=0) > 0.5
        return hit & (kpos <= t_row)

    def win_mask(kpos):
        diff = t_row - kpos
        return (diff >= 0) & (diff < WINDOW)

    hi = (s0 + tq - 1) // tk + 1
    o_s = flash(0, hi, ks_ref, vs_ref, sel_mask)
    lo_w = jnp.maximum(s0 - (WINDOW - 1), 0) // tk
    o_w = flash(lo_w, hi, kw_ref, vw_ref, win_mask)

    sg = jax.nn.sigmoid(gate_ref[...].astype(F32))
    outs = []
    for r in range(GROUP_SIZE):
        sl = slice(r * tq, (r + 1) * tq)
        outs.append(sg[:, 3 * r:3 * r + 1] * o_c[sl]
                    + sg[:, 3 * r + 1:3 * r + 2] * o_s[sl]
                    + sg[:, 3 * r + 2:3 * r + 3] * o_w[sl])
    o_ref[...] = jnp.concatenate(outs, axis=1).astype(o_ref.dtype)


def _nsa(proj, kvc, bsz, seq):
    tq, tk = 128, 256
    nq = seq // tq
    gw = GROUP_SIZE * HEAD_DIM

    def kv_spec(off):
        cb = off // HEAD_DIM
        return pl.BlockSpec((seq, HEAD_DIM), lambda b, g, i: (b, cb + g))

    nc = kvc.shape[3]
    return pl.pallas_call(
        functools.partial(_nsa_kernel, tq=tq, tk=tk, seq=seq),
        grid=(bsz, N_KV_GROUPS, nq),
        in_specs=[pl.BlockSpec((tq, gw), lambda b, g, i: (b * nq + i, g)),
                  pl.BlockSpec((tq, LANES), lambda b, g, i: (b * nq + i, OFF_GATE // LANES + g)),
                  pl.BlockSpec((None, None, None, nc, HEAD_DIM), lambda b, g, i: (0, b, g, 0, 0)),
                  pl.BlockSpec((None, None, None, nc, HEAD_DIM), lambda b, g, i: (1, b, g, 0, 0)),
                  kv_spec(OFF_KS), kv_spec(OFF_VS), kv_spec(OFF_KW), kv_spec(OFF_VW)],
        out_specs=pl.BlockSpec((tq, gw), lambda b, g, i: (b * nq + i, g)),
        out_shape=jax.ShapeDtypeStruct((bsz * seq, D_MODEL), BF16),
        compiler_params=_cparams(("parallel", "parallel", "arbitrary")),
        name="nsa",
    )(proj, proj, kvc, kvc, proj, proj, proj, proj)


def _s5_kernel(u_ref, pg_ref, bt_ref, pl_ref, dsk_ref, o_ref, wt_scr, ws_scr, wo_scr, xp_scr,
               *, chunks_per_seq):
    L = SSM_CHUNK
    hp = lax.Precision.HIGHEST
    are_g, aim_g, ldt_g = pg_ref[0], pg_ref[1], pg_ref[2]
    bt = bt_ref[...]
    are_l, aim_l, ldt_l, cr_l, ci_l = pl_ref[0], pl_ref[1], pl_ref[2], pl_ref[3], pl_ref[4]

    lane = lax.broadcasted_iota(jnp.int32, (LANES, LANES), 1)
    rowi = lax.broadcasted_iota(jnp.int32, (LANES, LANES), 0)
    is_re = lane < SSM_STATE
    row_grp = rowi // SSM_GROUP
    lane_grp = lane // SSM_GROUP

    def rows16(z):
        return jnp.concatenate([jnp.broadcast_to(z[g:g + 1], (SSM_GROUP, LANES))
                                for g in range(GROUPS_PER_BLOCK)], axis=0)

    def cmul(x, zr, zi):
        return x * zr + pltpu.roll(x, SSM_STATE, axis=1) * jnp.where(is_re, -zi, zi)

    dt_g = jnp.exp(ldt_g)
    l1r = jnp.exp(are_g * dt_g) * jnp.cos(aim_g * dt_g)
    l1i = jnp.exp(are_g * dt_g) * jnp.sin(aim_g * dt_g)
    den = are_g * are_g + aim_g * aim_g
    nr = l1r - 1.0
    coef_r = (nr * are_g + l1i * aim_g) / den
    coef_i = (l1i * are_g - nr * aim_g) / den
    bbar = cmul(bt, rows16(coef_r), rows16(coef_i))

    pw_g = [(jnp.ones_like(l1r), jnp.zeros_like(l1r))]
    for _ in range(L):
        pr, pi = pw_g[-1]
        pw_g.append((pr * l1r - pi * l1i, pr * l1i + pi * l1r))

    dt_l = jnp.exp(ldt_l)
    m1r = jnp.exp(are_l * dt_l) * jnp.cos(aim_l * dt_l)
    m1i = jnp.exp(are_l * dt_l) * jnp.sin(aim_l * dt_l)
    gs = []
    pr, pi = jnp.ones_like(m1r), jnp.zeros_like(m1r)
    for _ in range(L + 1):
        gs.append(jnp.concatenate([cr_l * pr - ci_l * pi, -(cr_l * pi + ci_l * pr)], axis=0))
        pr, pi = pr * m1r - pi * m1i, pr * m1i + pi * m1r

    same_grp = row_grp == lane_grp
    zeros_tile = jnp.zeros((LANES, LANES), BF16)

    for t in range(L):
        d_t = jnp.dot(bbar, gs[t], precision=hp, preferred_element_type=F32)
        d_t = jnp.where(same_grp, d_t, 0.0).astype(BF16)
        for j in range(L - t):
            i = j + t
            wt_scr[j * LANES:(j + 1) * LANES, i * LANES:(i + 1) * LANES] = d_t
    for j in range(L):
        for i in range(j):
            wt_scr[j * LANES:(j + 1) * LANES, i * LANES:(i + 1) * LANES] = zeros_tile

    for j in range(L):
        zr, zi = pw_g[L - 1 - j]
        s_j = cmul(bbar, rows16(zr), rows16(zi))
        for g in range(GROUPS_PER_BLOCK):
            ws_scr[j * LANES:(j + 1) * LANES, g * LANES:(g + 1) * LANES] = (
                jnp.where(row_grp == g, s_j, 0.0).astype(BF16))

    for g in range(GROUPS_PER_BLOCK):
        for i in range(L):
            wo_scr[g * LANES:(g + 1) * LANES, i * LANES:(i + 1) * LANES] = (
                jnp.where(lane_grp == g, gs[i + 1], 0.0).astype(BF16))

    u = u_ref[...]
    nrow = u.shape[0]
    y = jnp.dot(u, wt_scr[...], preferred_element_type=F32)
    st = jnp.dot(u, ws_scr[...], preferred_element_type=F32)

    kk = lax.broadcasted_iota(jnp.int32, (nrow, 1), 0) & (chunks_per_seq - 1)
    lane1 = lax.broadcasted_iota(jnp.int32, (1, LANES), 1)
    for g in range(GROUPS_PER_BLOCK):
        xs = st[:, g * LANES:(g + 1) * LANES]
        zr = pw_g[L][0][g:g + 1]
        zi = pw_g[L][1][g:g + 1]
        d = 1
        while d < chunks_per_seq:
            sh = jnp.where(kk >= d, pltpu.roll(xs, d, axis=0), 0.0)
            zmix = jnp.where(lane1 < SSM_STATE, -zi, zi)
            xs = xs + sh * zr + pltpu.roll(sh, SSM_STATE, axis=1) * zmix
            zr, zi = zr * zr - zi * zi, 2.0 * zr * zi
            d *= 2
        prev = jnp.where(kk >= 1, pltpu.roll(xs, 1, axis=0), 0.0)
        xp_scr[:, g * LANES:(g + 1) * LANES] = prev.astype(BF16)

    y = y + jnp.dot(xp_scr[...], wo_scr[...], preferred_element_type=F32)
    y = y + dsk_ref[...] * u.astype(F32)
    o_ref[...] = jax.nn.gelu(y).astype(o_ref.dtype)


def _s5(ucat, p_grp, p_bt, p_lanes, dsk, chunks_per_seq):
    nb, nrow, kk = ucat.shape
    nst = GROUPS_PER_BLOCK * LANES
    return pl.pallas_call(
        functools.partial(_s5_kernel, chunks_per_seq=chunks_per_seq),
        grid=(nb,),
        in_specs=[pl.BlockSpec((None, nrow, kk), lambda i: (i, 0, 0)),
                  pl.BlockSpec((None, 3, GROUPS_PER_BLOCK, LANES), lambda i: (i, 0, 0, 0)),
                  pl.BlockSpec((None, LANES, LANES), lambda i: (i, 0, 0)),
                  pl.BlockSpec((None, 5, SSM_STATE, LANES), lambda i: (i, 0, 0, 0)),
                  pl.BlockSpec((None, 1, kk), lambda i: (i, 0, 0))],
        out_specs=pl.BlockSpec((None, nrow, kk), lambda i: (i, 0, 0)),
        out_shape=jax.ShapeDtypeStruct((nb, nrow, kk), BF16),
        scratch_shapes=[pltpu.VMEM((kk, kk), BF16), pltpu.VMEM((kk, nst), BF16),
                        pltpu.VMEM((nst, kk), BF16), pltpu.VMEM((nrow, nst), BF16)],
        compiler_params=_cparams(("parallel",)),
        name="s5",
    )(ucat, p_grp, p_bt, p_lanes, dsk)


def _glu_kernel(ys_ref, w_ref, b_ref, ya_ref, ga_ref, gb_ref, o_ref, *, tn):
    j = pl.program_id(1)
    z = jnp.dot(ys_ref[...], w_ref[...], preferred_element_type=F32) + b_ref[...]
    ys_n = ys_ref[:, pl.ds(pl.multiple_of(j * tn, tn), tn)].astype(F32)
    yb = ys_n * jax.nn.sigmoid(z)
    out = (jax.nn.sigmoid(ga_ref[...].astype(F32)) * ya_ref[...].astype(F32)
           + jax.nn.sigmoid(gb_ref[...].astype(F32)) * yb)
    o_ref[...] = out.astype(o_ref.dtype)


def _glu(ys, proj, w_glu, b_glu, ya):
    m = ys.shape[0]
    tm, tn = 512, 512
    return pl.pallas_call(
        functools.partial(_glu_kernel, tn=tn),
        grid=(m // tm, D_MODEL // tn),
        in_specs=[pl.BlockSpec((tm, D_MODEL), lambda i, j: (i, 0)),
                  pl.BlockSpec((D_MODEL, tn), lambda i, j: (0, j)),
                  pl.BlockSpec((1, tn), lambda i, j: (0, j)),
                  pl.BlockSpec((tm, tn), lambda i, j: (i, j)),
                  pl.BlockSpec((tm, tn), lambda i, j: (i, OFF_GA // tn + j)),
                  pl.BlockSpec((tm, tn), lambda i, j: (i, OFF_GB // tn + j))],
        out_specs=pl.BlockSpec((tm, tn), lambda i, j: (i, j)),
        out_shape=jax.ShapeDtypeStruct((m, D_MODEL), BF16),
        compiler_params=_cparams(("parallel", "parallel")),
        name="glu",
    )(ys, w_glu, b_glu, ya, proj, proj)


def _outproj_kernel(m_ref, w_ref, x_ref, gt_ref, o_ref):
    acc = jnp.dot(m_ref[...], w_ref[...], preferred_element_type=F32)
    o_ref[...] = x_ref[...] + gt_ref[...] * acc


def _outproj(mix, w_out, x2, ada3, seq):
    m = mix.shape[0]
    tm, tn = 512, 512
    per_b = seq // tm
    return pl.pallas_call(
        _outproj_kernel,
        grid=(m // tm, D_MODEL // tn),
        in_specs=[pl.BlockSpec((tm, D_MODEL), lambda i, j: (i, 0)),
                  pl.BlockSpec((D_MODEL, tn), lambda i, j: (0, j)),
                  pl.BlockSpec((tm, tn), lambda i, j: (i, j)),
                  pl.BlockSpec((None, 1, tn), lambda i, j: ((i // per_b) * 6 + 2, 0, j))],
        out_specs=pl.BlockSpec((tm, tn), lambda i, j: (i, j)),
        out_shape=jax.ShapeDtypeStruct((m, D_MODEL), F32),
        compiler_params=_cparams(("parallel", "parallel")),
        name="outproj",
    )(mix, w_out, x2, ada3)


def _mlp_kernel(h_ref, g_ref, sc_ref, sh_ref, gt_ref, wu_ref, wd_ref, gf_ref, o_ref, u_scr, acc_scr):
    f = pl.program_id(1)

    @pl.when(f == 0)
    def _():
        h = h_ref[...]
        ms = jnp.mean(h * h, axis=-1, keepdims=True)
        y = h * lax.rsqrt(ms + EPS) * g_ref[...]
        u_scr[...] = (y * (1.0 + sc_ref[...]) + sh_ref[...]).astype(BF16)
        acc_scr[...] = jnp.zeros_like(acc_scr)

    a = jnp.dot(u_scr[...], wu_ref[...], preferred_element_type=F32)
    a = jnp.square(jnp.maximum(a, 0.0))
    acc_scr[...] += jnp.dot(a.astype(BF16), wd_ref[...], preferred_element_type=F32)

    @pl.when(f == pl.num_programs(1) - 1)
    def _():
        h2 = h_ref[...] + gt_ref[...] * acc_scr[...]
        ms = jnp.mean(h2 * h2, axis=-1, keepdims=True)
        o_ref[...] = h2 * lax.rsqrt(ms + EPS) * gf_ref[...]


def _mlp(h1, g_mlp, ada3, w_up, w_down, g_final, seq):
    m = h1.shape[0]
    tm, tf = 512, 512
    per_b = seq // tm
    ff = w_up.shape[1]

    def ada_spec(k):
        return pl.BlockSpec((None, 1, D_MODEL), lambda i, f: ((i // per_b) * 6 + k, 0, 0))

    return pl.pallas_call(
        _mlp_kernel,
        grid=(m // tm, ff // tf),
        in_specs=[pl.BlockSpec((tm, D_MODEL), lambda i, f: (i, 0)),
                  pl.BlockSpec((1, D_MODEL), lambda i, f: (0, 0)),
                  ada_spec(4), ada_spec(3), ada_spec(5),
                  pl.BlockSpec((D_MODEL, tf), lambda i, f: (0, f)),
                  pl.BlockSpec((tf, D_MODEL), lambda i, f: (f, 0)),
                  pl.BlockSpec((1, D_MODEL), lambda i, f: (0, 0))],
        out_specs=pl.BlockSpec((tm, D_MODEL), lambda i, f: (i, 0)),
        out_shape=jax.ShapeDtypeStruct((m, D_MODEL), F32),
        scratch_shapes=[pltpu.VMEM((tm, D_MODEL), BF16), pltpu.VMEM((tm, D_MODEL), F32)],
        compiler_params=_cparams(("parallel", "arbitrary")),
        name="mlp",
    )(h1, g_mlp, ada3, ada3, ada3, w_up, w_down, g_final)


def _pad_w_in(w):
    wg = w[:, OFF_GATE_RAW:OFF_SSM_RAW].reshape(D_MODEL, N_KV_GROUPS, 3 * GROUP_SIZE)
    wg = jnp.pad(wg, ((0, 0), (0, 0), (0, LANES - 3 * GROUP_SIZE))).reshape(D_MODEL, N_KV_GROUPS * LANES)
    return jnp.concatenate([w[:, :OFF_GATE_RAW], wg, w[:, OFF_SSM_RAW:]], axis=1).astype(BF16)


def _s5_params(a_re, a_im, log_dt, b_re, b_im, c_re, c_im, d_skip):
    nb, gp, p, cg = N_SSM_BLOCKS, GROUPS_PER_BLOCK, SSM_STATE, SSM_GROUP
    ldt = jnp.broadcast_to(log_dt[:, None], (SSM_NGROUPS, p))

    def grp(a):
        return jnp.concatenate([a, a], axis=-1).reshape(nb, gp, LANES)

    def lanes(a):
        a = jnp.broadcast_to(a.reshape(nb, gp, p, 1).transpose(0, 2, 1, 3), (nb, p, gp, cg))
        return a.reshape(nb, p, LANES)

    def ct(cm):
        return cm.reshape(nb, gp, cg, p).transpose(0, 3, 1, 2).reshape(nb, p, LANES)

    p_grp = jnp.stack([grp(a_re), grp(a_im), grp(ldt)], axis=1)
    p_bt = jnp.stack([b_re, b_im], axis=1).transpose(0, 3, 1, 2).reshape(nb, LANES, LANES)
    p_lanes = jnp.stack([lanes(a_re), lanes(a_im), lanes(ldt), ct(c_re), ct(c_im)], axis=1)
    dsk = jnp.tile(d_skip.reshape(nb, 1, LANES), (1, 1, SSM_CHUNK))
    return p_grp, p_bt, p_lanes, dsk


def kernel(x, c, w_ada, b_ada, g_mix, w_in, w_ck1, w_ck2, pe_ck, w_cv1, w_cv2, pe_cv,
           a_re, a_im, log_dt, b_re, b_im, c_re, c_im, d_skip, w_glu, b_glu,
           w_out, g_mlp, w_up, w_down, g_final):
    bsz, seq, _ = x.shape
    assert w_ada.shape[0] == 1, "single-layer block"
    assert seq % 1024 == 0 and (seq & (seq - 1)) == 0
    m = bsz * seq
    x2 = x.reshape(m, D_MODEL)

    c_pad = jnp.pad(c, ((0, 8 - bsz), (0, 0)))
    ada = _ada(c_pad, w_ada[0], b_ada)
    ada3 = ada[:bsz].reshape(bsz * 6, 1, D_MODEL)

    proj = _inproj(x2, g_mix, ada3, _pad_w_in(w_in[0]), seq)

    nc = seq // CMP_STRIDE
    xc = proj[:, OFF_KC:OFF_KS].reshape(bsz, nc, CMP_STRIDE, 2, N_KV_GROUPS, HEAD_DIM)
    xc = xc.transpose(3, 0, 4, 1, 2, 5).reshape(2, bsz, N_KV_GROUPS, nc, CMP_STRIDE * HEAD_DIM)
    w1s = jnp.stack([w_ck1[0], w_cv1[0]]).reshape(2, CMP_LEN * HEAD_DIM, HEAD_DIM).astype(BF16)
    pes = jnp.stack([pe_ck[0], pe_cv[0]]).reshape(2, 1, CMP_LEN * HEAD_DIM)
    pes = jnp.pad(pes, ((0, 0), (0, 7), (0, 0))).astype(BF16)
    w2s = jnp.stack([w_ck2[0], w_cv2[0]]).astype(BF16)
    kvc = _compress(xc, w1s, pes, w2s)

    y_a = _nsa(proj, kvc, bsz, seq)

    n_chunks = m // SSM_CHUNK
    ucat = proj[:, OFF_SSM:OFF_GA].reshape(n_chunks, SSM_CHUNK, N_SSM_BLOCKS, LANES)
    ucat = ucat.transpose(2, 0, 1, 3).reshape(N_SSM_BLOCKS, n_chunks, SSM_CHUNK * LANES)
    p_grp, p_bt, p_lanes, dsk = _s5_params(a_re[0], a_im[0], log_dt[0], b_re[0], b_im[0],
                                           c_re[0], c_im[0], d_skip[0])
    ycat = _s5(ucat, p_grp, p_bt, p_lanes, dsk, seq // SSM_CHUNK)
    ys = ycat.reshape(N_SSM_BLOCKS, n_chunks, SSM_CHUNK, LANES).transpose(1, 2, 0, 3).reshape(m, D_MODEL)

    mix = _glu(ys, proj, w_glu[0].astype(BF16), b_glu, y_a)
    h1 = _outproj(mix, w_out[0].astype(BF16), x2, ada3, seq)
    out = _mlp(h1, g_mlp, ada3, w_up[0].astype(BF16), w_down[0].astype(BF16),
               g_final.reshape(1, D_MODEL), seq)
    return out.reshape(bsz, seq, D_MODEL)
```

```python
import functools

import jax
import jax.numpy as jnp
from jax import lax
from jax.experimental import pallas as pl
from jax.experimental.pallas import tpu as pltpu

F32 = jnp.float32
BF16 = jnp.bfloat16

D_MODEL = 2048
N_HEADS = 16
HEAD_DIM = 128
N_KV_GROUPS = 4
GROUP_SIZE = 4
KV_DIM = N_KV_GROUPS * HEAD_DIM
CMP_LEN = 32
CMP_STRIDE = 16
SEL_BLOCK = 64
SEL_TOPN = 16
WINDOW = 512
SSM_GROUP = 16
SSM_NGROUPS = D_MODEL // SSM_GROUP
SSM_STATE = 64
EPS = 1e-6
NEG = -1e30
BIG = 1e6
LOG2E = 1.4426950408889634

LANES = 128
SSM_CHUNK = 16
GROUPS_PER_BLOCK = LANES // SSM_GROUP
N_SSM_BLOCKS = SSM_NGROUPS // GROUPS_PER_BLOCK

OFF_Q = 0
OFF_KC = OFF_Q + D_MODEL
OFF_VC = OFF_KC + KV_DIM
OFF_KS = OFF_VC + KV_DIM
OFF_VS = OFF_KS + KV_DIM
OFF_KW = OFF_VS + KV_DIM
OFF_VW = OFF_KW + KV_DIM
OFF_GATE = OFF_VW + KV_DIM
OFF_SSM = OFF_GATE + N_KV_GROUPS * LANES
OFF_GA = OFF_SSM + D_MODEL
OFF_GB = OFF_GA + D_MODEL
PROJ_PAD = OFF_GB + D_MODEL
N_GATE_RAW = 3 * N_HEADS
OFF_GATE_RAW = OFF_VW + KV_DIM
OFF_SSM_RAW = OFF_GATE_RAW + N_GATE_RAW

VMEM_LIMIT = 56 * 1024 * 1024


def _cparams(sem):
    return pltpu.CompilerParams(dimension_semantics=sem, vmem_limit_bytes=VMEM_LIMIT)


def _t(x):
    r, c = x.shape
    rows = []
    for j in range(c // LANES):
        rows.append(jnp.concatenate([x[i * LANES:(i + 1) * LANES, j * LANES:(j + 1) * LANES].T
                                     for i in range(r // LANES)], axis=1))
    return jnp.concatenate(rows, axis=0)


def _nt_dot(a, b):
    return lax.dot_general(a, b, (((1,), (1,)), ((), ())), preferred_element_type=F32)


def _ada_kernel(c_ref, w_ref, b_ref, o_ref):
    c = c_ref[...]
    cond = c * jax.nn.sigmoid(c)
    o_ref[...] = jnp.dot(cond.astype(BF16), w_ref[...].astype(BF16),
                         preferred_element_type=F32) + b_ref[...]


def _ada(c_pad, w_ada, b_ada):
    n = w_ada.shape[1]
    tn = 1024
    return pl.pallas_call(
        _ada_kernel,
        grid=(n // tn,),
        in_specs=[pl.BlockSpec((8, D_MODEL), lambda j: (0, 0)),
                  pl.BlockSpec((D_MODEL, tn), lambda j: (0, j)),
                  pl.BlockSpec((1, tn), lambda j: (0, j))],
        out_specs=pl.BlockSpec((8, tn), lambda j: (0, j)),
        out_shape=jax.ShapeDtypeStruct((8, n), F32),
        compiler_params=_cparams(("parallel",)),
        name="ada",
    )(c_pad, w_ada, b_ada)


def _inproj_kernel(x_ref, g_ref, sc_ref, sh_ref, w_ref, o_ref, u_scr):
    @pl.when(pl.program_id(1) == 0)
    def _():
        x = x_ref[...]
        ms = jnp.mean(x * x, axis=-1, keepdims=True)
        y = x * lax.rsqrt(ms + EPS) * g_ref[...]
        u_scr[...] = (y * (1.0 + sc_ref[...]) + sh_ref[...]).astype(BF16)

    o_ref[...] = jnp.dot(u_scr[...], w_ref[...], preferred_element_type=F32).astype(o_ref.dtype)


def _inproj(x2, g_mix, ada3, w_p, seq):
    m = x2.shape[0]
    tm, tn = 1024, 512
    per_b = seq // tm
    return pl.pallas_call(
        _inproj_kernel,
        grid=(m // tm, PROJ_PAD // tn),
        in_specs=[pl.BlockSpec((tm, D_MODEL), lambda i, j: (i, 0)),
                  pl.BlockSpec((1, D_MODEL), lambda i, j: (0, 0)),
                  pl.BlockSpec((None, 1, D_MODEL), lambda i, j: ((i // per_b) * 6 + 1, 0, 0)),
                  pl.BlockSpec((None, 1, D_MODEL), lambda i, j: ((i // per_b) * 6 + 0, 0, 0)),
                  pl.BlockSpec((D_MODEL, tn), lambda i, j: (0, j))],
        out_specs=pl.BlockSpec((tm, tn), lambda i, j: (i, j)),
        out_shape=jax.ShapeDtypeStruct((m, PROJ_PAD), BF16),
        scratch_shapes=[pltpu.VMEM((tm, D_MODEL), BF16)],
        compiler_params=_cparams(("parallel", "arbitrary")),
        name="inproj",
    )(x2, g_mix, ada3, ada3, w_p)


def _compress_kernel(x_ref, w1_ref, pe_ref, w2_ref, o_ref):
    x = x_ref[...]
    nc = x.shape[0]
    half = CMP_STRIDE * HEAD_DIM
    w1 = w1_ref[...]
    ha = jnp.dot(x, w1[:half], preferred_element_type=F32)
    hb = jnp.dot(x, w1[half:], preferred_element_type=F32)
    hb = pltpu.roll(hb, nc - 1, axis=0)
    pe_t = jnp.dot(pe_ref[...], w1, preferred_element_type=F32)[0:1]
    hid = jax.nn.gelu(ha + hb + pe_t)
    out = jnp.dot(hid.astype(BF16), w2_ref[...], preferred_element_type=F32)
    row = lax.broadcasted_iota(jnp.int32, out.shape, 0)
    o_ref[...] = jnp.where(row < nc - 1, out, 0.0).astype(o_ref.dtype)


def _compress(xc, w1s, pes, w2s):
    _, b, g, nc, kk = xc.shape
    return pl.pallas_call(
        _compress_kernel,
        grid=(2, b, g),
        in_specs=[pl.BlockSpec((None, None, None, nc, kk), lambda a, i, j: (a, i, j, 0, 0)),
                  pl.BlockSpec((None, CMP_LEN * HEAD_DIM, HEAD_DIM), lambda a, i, j: (a, 0, 0)),
                  pl.BlockSpec((None, 8, CMP_LEN * HEAD_DIM), lambda a, i, j: (a, 0, 0)),
                  pl.BlockSpec((None, HEAD_DIM, HEAD_DIM), lambda a, i, j: (a, 0, 0))],
        out_specs=pl.BlockSpec((None, None, None, nc, HEAD_DIM), lambda a, i, j: (a, i, j, 0, 0)),
        out_shape=jax.ShapeDtypeStruct((2, b, g, nc, HEAD_DIM), BF16),
        compiler_params=_cparams(("parallel", "parallel", "parallel")),
        name="compress",
    )(xc, w1s, pes, w2s)


def _nsa_kernel(q_ref, gate_ref, kc_ref, vct_ref, ks_ref, e_ref, vst_ref, kw_ref, vwt_ref, o_ref,
                *, tq, tk, seq):
    s0 = pl.program_id(2) * tq
    rows = GROUP_SIZE * tq
    nsel = seq // SEL_BLOCK
    topn = min(SEL_TOPN, nsel)
    qscale = HEAD_DIM ** -0.5 * LOG2E

    q = q_ref[...]
    q4 = jnp.concatenate([q[:, r * HEAD_DIM:(r + 1) * HEAD_DIM] for r in range(GROUP_SIZE)], axis=0)
    q4 = (q4.astype(F32) * qscale).astype(BF16)
    t_lane = s0 + (lax.broadcasted_iota(jnp.int32, (1, rows), 1) & (tq - 1))

    kc = kc_ref[...]
    nc = kc.shape[0]
    sc = _nt_dot(kc, q4)
    n_col = lax.broadcasted_iota(jnp.int32, (nc, 1), 0)
    valid_c = (n_col * CMP_STRIDE + (CMP_LEN - 1)) <= t_lane
    sc = jnp.where(valid_c, sc, NEG)
    mc = jnp.max(sc, axis=0, keepdims=True)
    pc = jnp.where(valid_c, jnp.exp2(sc - mc), 0.0)
    pc = pc * (1.0 / jnp.maximum(jnp.sum(pc, axis=0, keepdims=True), 1e-30))
    o_c = jnp.dot(vct_ref[...], pc.astype(BF16), preferred_element_type=F32)
    psum = pc[:, 0:tq] + pc[:, tq:2 * tq] + pc[:, 2 * tq:3 * tq] + pc[:, 3 * tq:4 * tq]

    jj = lax.broadcasted_iota(jnp.int32, (nsel, nc), 0)
    nn = lax.broadcasted_iota(jnp.int32, (nsel, nc), 1)
    ov = ((nn * CMP_STRIDE < jj * SEL_BLOCK + SEL_BLOCK)
          & (nn * CMP_STRIDE + CMP_LEN > jj * SEL_BLOCK)
          & (nn < nc - 1))
    ov = jnp.where(ov, 1.0, 0.0).astype(BF16)
    p_hi = psum.astype(BF16)
    p_lo = (psum - p_hi.astype(F32)).astype(BF16)
    imp_t = (jnp.dot(ov, p_hi, preferred_element_type=F32)
             + jnp.dot(ov, p_lo, preferred_element_type=F32))

    jcol = lax.broadcasted_iota(jnp.int32, (nsel, tq), 0)
    cur = lax.shift_right_logical(s0 + lax.broadcasted_iota(jnp.int32, (nsel, tq), 1), 6)
    valid = jcol <= cur
    forced = (jcol == 0) | (jcol == cur) | (jcol == cur - 1)
    score = jnp.where(forced, BIG, jnp.where(valid, imp_t, -BIG))

    def rank_of(sc_):
        slabs = []
        for sb in range(nsel // 8):
            sj = sc_[sb * 8:(sb + 1) * 8]
            jr = jcol[sb * 8:(sb + 1) * 8, :LANES]
            rank = jnp.zeros((8, LANES), F32)
            for i in range(nsel):
                si = sc_[i:i + 1]
                if i < sb * 8:
                    cond = si >= sj
                elif i >= sb * 8 + 8:
                    cond = si > sj
                else:
                    cond = (si > sj) | ((jr > i) & (si == sj))
                rank = rank + jnp.where(cond, 1.0, 0.0)
            slabs.append(rank)
        return jnp.concatenate(slabs, axis=0)

    rank = jnp.concatenate([rank_of(score[:, c * LANES:(c + 1) * LANES]) for c in range(tq // LANES)], axis=1)
    bias_t = jnp.where((rank < topn) & valid, 0.0, NEG)
    bias_t = jnp.concatenate([bias_t, jnp.zeros((LANES - nsel, tq), F32)], axis=0)
    bias_q = _t(bias_t).astype(BF16)
    q_aug = jnp.concatenate([q4, jnp.concatenate([bias_q] * GROUP_SIZE, axis=0)], axis=1)

    def flash(lo, hi, key_fn, vt_ref, rhs, mask_fn, carry=None):
        def body(kb, carry):
            m, l, acc = carry
            off = pl.multiple_of(kb * tk, tk)
            s = _nt_dot(key_fn(off), rhs)
            if mask_fn is not None:
                kpos = off + lax.broadcasted_iota(jnp.int32, (tk, 1), 0)
                s = jnp.where(mask_fn(kpos), s, NEG)
            m_new = jnp.maximum(m, jnp.max(s, axis=0, keepdims=True))
            alpha = jnp.exp2(m - m_new)
            p = jnp.exp2(s - m_new)
            l = alpha * l + jnp.sum(p, axis=0, keepdims=True)
            acc = alpha * acc + jnp.dot(vt_ref[:, pl.ds(off, tk)], p.astype(BF16),
                                        preferred_element_type=F32)
            return m_new, l, acc

        if carry is None:
            carry = (jnp.full((1, rows), NEG, F32), jnp.zeros((1, rows), F32),
                     jnp.zeros((HEAD_DIM, rows), F32))
        return lax.fori_loop(lo, hi, body, carry)

    def finish(carry):
        _, l, acc = carry
        return acc * (1.0 / jnp.maximum(l, 1e-30))

    def sel_keys(off):
        return jnp.concatenate([ks_ref[pl.ds(off, tk), :], e_ref[pl.ds(off, tk), :]], axis=1)

    hi = (s0 + tq - 1) // tk + 1
    carry = flash(0, hi - 1, sel_keys, vst_ref, q_aug, None)
    o_s = finish(flash(hi - 1, hi, sel_keys, vst_ref, q_aug, lambda kpos: kpos <= t_lane, carry))

    def win_mask(kpos):
        diff = t_lane - kpos
        return (diff >= 0) & (diff < WINDOW)

    lo_w = jnp.maximum(s0 - (WINDOW - 1), 0) // tk
    o_w = finish(flash(lo_w, hi, lambda off: kw_ref[pl.ds(off, tk), :], vwt_ref, q4, win_mask))

    sg = _t(jax.nn.sigmoid(gate_ref[...].astype(F32)))
    outs = []
    for r in range(GROUP_SIZE):
        sl = slice(r * tq, (r + 1) * tq)
        o_r = (sg[3 * r:3 * r + 1] * o_c[:, sl] + sg[3 * r + 1:3 * r + 2] * o_s[:, sl]
               + sg[3 * r + 2:3 * r + 3] * o_w[:, sl])
        outs.append(_t(o_r))
    o_ref[...] = jnp.concatenate(outs, axis=1).astype(o_ref.dtype)


def _nsa(proj, kvc, vct, vst, vwt, e_tab, bsz, seq):
    tq, tk = 256, 512
    nq = seq // tq
    gw = GROUP_SIZE * HEAD_DIM
    nc = kvc.shape[3]

    def k_spec(off):
        cb = off // HEAD_DIM
        return pl.BlockSpec((seq, HEAD_DIM), lambda b, g, i: (b, cb + g))

    def vt_spec(n):
        return pl.BlockSpec((None, None, HEAD_DIM, n), lambda b, g, i: (b, g, 0, 0))

    return pl.pallas_call(
        functools.partial(_nsa_kernel, tq=tq, tk=tk, seq=seq),
        grid=(bsz, N_KV_GROUPS, nq),
        in_specs=[pl.BlockSpec((tq, gw), lambda b, g, i: (b * nq + i, g)),
                  pl.BlockSpec((tq, LANES), lambda b, g, i: (b * nq + i, OFF_GATE // LANES + g)),
                  pl.BlockSpec((None, None, None, nc, HEAD_DIM), lambda b, g, i: (0, b, g, 0, 0)),
                  vt_spec(nc),
                  k_spec(OFF_KS),
                  pl.BlockSpec((seq, LANES), lambda b, g, i: (0, 0)),
                  vt_spec(seq),
                  k_spec(OFF_KW),
                  vt_spec(seq)],
        out_specs=pl.BlockSpec((tq, gw), lambda b, g, i: (b * nq + i, g)),
        out_shape=jax.ShapeDtypeStruct((bsz * seq, D_MODEL), BF16),
        compiler_params=_cparams(("parallel", "parallel", "arbitrary")),
        name="nsa",
    )(proj, proj, kvc, vct, proj, e_tab, vst, proj, vwt)


def _s5_kernel(u_ref, pg_ref, bt_ref, pl_ref, dsk_ref, o_ref, wt_scr, ws_scr, wo_scr, xp_scr,
               *, chunks_per_seq):
    L = SSM_CHUNK
    hp = lax.Precision.HIGHEST
    are_g, aim_g, ldt_g = pg_ref[0], pg_ref[1], pg_ref[2]
    bt = bt_ref[...]
    are_l, aim_l, ldt_l, cr_l, ci_l = pl_ref[0], pl_ref[1], pl_ref[2], pl_ref[3], pl_ref[4]

    lane = lax.broadcasted_iota(jnp.int32, (LANES, LANES), 1)
    rowi = lax.broadcasted_iota(jnp.int32, (LANES, LANES), 0)
    is_re = lane < SSM_STATE
    row_grp = rowi // SSM_GROUP
    lane_grp = lane // SSM_GROUP

    def rows16(z):
        return jnp.concatenate([jnp.broadcast_to(z[g:g + 1], (SSM_GROUP, LANES))
                                for g in range(GROUPS_PER_BLOCK)], axis=0)

    def cmul(x, zr, zi):
        return x * zr + pltpu.roll(x, SSM_STATE, axis=1) * jnp.where(is_re, -zi, zi)

    dt_g = jnp.exp(ldt_g)
    l1r = jnp.exp(are_g * dt_g) * jnp.cos(aim_g * dt_g)
    l1i = jnp.exp(are_g * dt_g) * jnp.sin(aim_g * dt_g)
    den = are_g * are_g + aim_g * aim_g
    nr = l1r - 1.0
    coef_r = (nr * are_g + l1i * aim_g) / den
    coef_i = (l1i * are_g - nr * aim_g) / den
    bbar = cmul(bt, rows16(coef_r), rows16(coef_i))

    pw_g = [(jnp.ones_like(l1r), jnp.zeros_like(l1r))]
    for _ in range(L):
        pr, pi = pw_g[-1]
        pw_g.append((pr * l1r - pi * l1i, pr * l1i + pi * l1r))

    dt_l = jnp.exp(ldt_l)
    m1r = jnp.exp(are_l * dt_l) * jnp.cos(aim_l * dt_l)
    m1i = jnp.exp(are_l * dt_l) * jnp.sin(aim_l * dt_l)
    gs = []
    pr, pi = jnp.ones_like(m1r), jnp.zeros_like(m1r)
    for _ in range(L + 1):
        gs.append(jnp.concatenate([cr_l * pr - ci_l * pi, -(cr_l * pi + ci_l * pr)], axis=0))
        pr, pi = pr * m1r - pi * m1i, pr * m1i + pi * m1r

    same_grp = row_grp == lane_grp
    zeros_tile = jnp.zeros((LANES, LANES), BF16)

    for t in range(L):
        d_t = jnp.dot(bbar, gs[t], precision=hp, preferred_element_type=F32)
        d_t = jnp.where(same_grp, d_t, 0.0).astype(BF16)
        for j in range(L - t):
            i = j + t
            wt_scr[j * LANES:(j + 1) * LANES, i * LANES:(i + 1) * LANES] = d_t
    for j in range(L):
        for i in range(j):
            wt_scr[j * LANES:(j + 1) * LANES, i * LANES:(i + 1) * LANES] = zeros_tile

    for j in range(L):
        zr, zi = pw_g[L - 1 - j]
        s_j = cmul(bbar, rows16(zr), rows16(zi))
        for g in range(GROUPS_PER_BLOCK):
            ws_scr[j * LANES:(j + 1) * LANES, g * LANES:(g + 1) * LANES] = (
                jnp.where(row_grp == g, s_j, 0.0).astype(BF16))

    for g in range(GROUPS_PER_BLOCK):
        for i in range(L):
            wo_scr[g * LANES:(g + 1) * LANES, i * LANES:(i + 1) * LANES] = (
                jnp.where(lane_grp == g, gs[i + 1], 0.0).astype(BF16))

    u = u_ref[...]
    nrow = u.shape[0]
    y = jnp.dot(u, wt_scr[...], preferred_element_type=F32)
    st = jnp.dot(u, ws_scr[...], preferred_element_type=F32)

    kk = lax.broadcasted_iota(jnp.int32, (nrow, 1), 0) & (chunks_per_seq - 1)
    lane1 = lax.broadcasted_iota(jnp.int32, (1, LANES), 1)
    for g in range(GROUPS_PER_BLOCK):
        xs = st[:, g * LANES:(g + 1) * LANES]
        zr = pw_g[L][0][g:g + 1]
        zi = pw_g[L][1][g:g + 1]
        d = 1
        while d < chunks_per_seq:
            sh = jnp.where(kk >= d, pltpu.roll(xs, d, axis=0), 0.0)
            zmix = jnp.where(lane1 < SSM_STATE, -zi, zi)
            xs = xs + sh * zr + pltpu.roll(sh, SSM_STATE, axis=1) * zmix
            zr, zi = zr * zr - zi * zi, 2.0 * zr * zi
            d *= 2
        prev = jnp.where(kk >= 1, pltpu.roll(xs, 1, axis=0), 0.0)
        xp_scr[:, g * LANES:(g + 1) * LANES] = prev.astype(BF16)

    y = y + jnp.dot(xp_scr[...], wo_scr[...], preferred_element_type=F32)
    y = y + dsk_ref[...] * u.astype(F32)
    o_ref[...] = jax.nn.gelu(y).astype(o_ref.dtype)


def _s5(ucat, p_grp, p_bt, p_lanes, dsk, chunks_per_seq):
    nb, nrow, kk = ucat.shape
    nst = GROUPS_PER_BLOCK * LANES
    return pl.pallas_call(
        functools.partial(_s5_kernel, chunks_per_seq=chunks_per_seq),
        grid=(nb,),
        in_specs=[pl.BlockSpec((None, nrow, kk), lambda i: (i, 0, 0)),
                  pl.BlockSpec((None, 3, GROUPS_PER_BLOCK, LANES), lambda i: (i, 0, 0, 0)),
                  pl.BlockSpec((None, LANES, LANES), lambda i: (i, 0, 0)),
                  pl.BlockSpec((None, 5, SSM_STATE, LANES), lambda i: (i, 0, 0, 0)),
                  pl.BlockSpec((None, 1, kk), lambda i: (i, 0, 0))],
        out_specs=pl.BlockSpec((None, nrow, kk), lambda i: (i, 0, 0)),
        out_shape=jax.ShapeDtypeStruct((nb, nrow, kk), BF16),
        scratch_shapes=[pltpu.VMEM((kk, kk), BF16), pltpu.VMEM((kk, nst), BF16),
                        pltpu.VMEM((nst, kk), BF16), pltpu.VMEM((nrow, nst), BF16)],
        compiler_params=_cparams(("parallel",)),
        name="s5",
    )(ucat, p_grp, p_bt, p_lanes, dsk)


def _glu_kernel(ys_ref, w_ref, b_ref, ya_ref, ga_ref, gb_ref, o_ref, *, tn):
    j = pl.program_id(1)
    z = jnp.dot(ys_ref[...], w_ref[...], preferred_element_type=F32) + b_ref[...]
    ys_n = ys_ref[:, pl.ds(pl.multiple_of(j * tn, tn), tn)].astype(F32)
    yb = ys_n * jax.nn.sigmoid(z)
    out = (jax.nn.sigmoid(ga_ref[...].astype(F32)) * ya_ref[...].astype(F32)
           + jax.nn.sigmoid(gb_ref[...].astype(F32)) * yb)
    o_ref[...] = out.astype(o_ref.dtype)


def _glu(ys, proj, w_glu, b_glu, ya):
    m = ys.shape[0]
    tm, tn = 512, 512
    return pl.pallas_call(
        functools.partial(_glu_kernel, tn=tn),
        grid=(m // tm, D_MODEL // tn),
        in_specs=[pl.BlockSpec((tm, D_MODEL), lambda i, j: (i, 0)),
                  pl.BlockSpec((D_MODEL, tn), lambda i, j: (0, j)),
                  pl.BlockSpec((1, tn), lambda i, j: (0, j)),
                  pl.BlockSpec((tm, tn), lambda i, j: (i, j)),
                  pl.BlockSpec((tm, tn), lambda i, j: (i, OFF_GA // tn + j)),
                  pl.BlockSpec((tm, tn), lambda i, j: (i, OFF_GB // tn + j))],
        out_specs=pl.BlockSpec((tm, tn), lambda i, j: (i, j)),
        out_shape=jax.ShapeDtypeStruct((m, D_MODEL), BF16),
        compiler_params=_cparams(("parallel", "parallel")),
        name="glu",
    )(ys, w_glu, b_glu, ya, proj, proj)


def _outproj_kernel(m_ref, w_ref, x_ref, gt_ref, o_ref):
    acc = jnp.dot(m_ref[...], w_ref[...], preferred_element_type=F32)
    o_ref[...] = x_ref[...] + gt_ref[...] * acc


def _outproj(mix, w_out, x2, ada3, seq):
    m = mix.shape[0]
    tm, tn = 512, 512
    per_b = seq // tm
    return pl.pallas_call(
        _outproj_kernel,
        grid=(m // tm, D_MODEL // tn),
        in_specs=[pl.BlockSpec((tm, D_MODEL), lambda i, j: (i, 0)),
                  pl.BlockSpec((D_MODEL, tn), lambda i, j: (0, j)),
                  pl.BlockSpec((tm, tn), lambda i, j: (i, j)),
                  pl.BlockSpec((None, 1, tn), lambda i, j: ((i // per_b) * 6 + 2, 0, j))],
        out_specs=pl.BlockSpec((tm, tn), lambda i, j: (i, j)),
        out_shape=jax.ShapeDtypeStruct((m, D_MODEL), F32),
        compiler_params=_cparams(("parallel", "parallel")),
        name="outproj",
    )(mix, w_out, x2, ada3)


def _mlp_kernel(h_ref, g_ref, sc_ref, sh_ref, gt_ref, wu_ref, wd_ref, gf_ref, o_ref, u_scr, acc_scr):
    f = pl.program_id(1)

    @pl.when(f == 0)
    def _():
        h = h_ref[...]
        ms = jnp.mean(h * h, axis=-1, keepdims=True)
        y = h * lax.rsqrt(ms + EPS) * g_ref[...]
        u_scr[...] = (y * (1.0 + sc_ref[...]) + sh_ref[...]).astype(BF16)
        acc_scr[...] = jnp.zeros_like(acc_scr)

    a = jnp.dot(u_scr[...], wu_ref[...], preferred_element_type=F32)
    a = jnp.square(jnp.maximum(a, 0.0))
    acc_scr[...] += jnp.dot(a.astype(BF16), wd_ref[...], preferred_element_type=F32)

    @pl.when(f == pl.num_programs(1) - 1)
    def _():
        h2 = h_ref[...] + gt_ref[...] * acc_scr[...]
        ms = jnp.mean(h2 * h2, axis=-1, keepdims=True)
        o_ref[...] = h2 * lax.rsqrt(ms + EPS) * gf_ref[...]


def _mlp(h1, g_mlp, ada3, w_up, w_down, g_final, seq):
    m = h1.shape[0]
    tm, tf = 512, 512
    per_b = seq // tm
    ff = w_up.shape[1]

    def ada_spec(k):
        return pl.BlockSpec((None, 1, D_MODEL), lambda i, f: ((i // per_b) * 6 + k, 0, 0))

    return pl.pallas_call(
        _mlp_kernel,
        grid=(m // tm, ff // tf),
        in_specs=[pl.BlockSpec((tm, D_MODEL), lambda i, f: (i, 0)),
                  pl.BlockSpec((1, D_MODEL), lambda i, f: (0, 0)),
                  ada_spec(4), ada_spec(3), ada_spec(5),
                  pl.BlockSpec((D_MODEL, tf), lambda i, f: (0, f)),
                  pl.BlockSpec((tf, D_MODEL), lambda i, f: (f, 0)),
                  pl.BlockSpec((1, D_MODEL), lambda i, f: (0, 0))],
        out_specs=pl.BlockSpec((tm, D_MODEL), lambda i, f: (i, 0)),
        out_shape=jax.ShapeDtypeStruct((m, D_MODEL), F32),
        scratch_shapes=[pltpu.VMEM((tm, D_MODEL), BF16), pltpu.VMEM((tm, D_MODEL), F32)],
        compiler_params=_cparams(("parallel", "arbitrary")),
        name="mlp",
    )(h1, g_mlp, ada3, ada3, ada3, w_up, w_down, g_final)


def _pad_w_in(w):
    wg = w[:, OFF_GATE_RAW:OFF_SSM_RAW].reshape(D_MODEL, N_KV_GROUPS, 3 * GROUP_SIZE)
    wg = jnp.pad(wg, ((0, 0), (0, 0), (0, LANES - 3 * GROUP_SIZE))).reshape(D_MODEL, N_KV_GROUPS * LANES)
    return jnp.concatenate([w[:, :OFF_GATE_RAW], wg, w[:, OFF_SSM_RAW:]], axis=1).astype(BF16)


def _s5_params(a_re, a_im, log_dt, b_re, b_im, c_re, c_im, d_skip):
    nb, gp, p, cg = N_SSM_BLOCKS, GROUPS_PER_BLOCK, SSM_STATE, SSM_GROUP
    ldt = jnp.broadcast_to(log_dt[:, None], (SSM_NGROUPS, p))

    def grp(a):
        return jnp.concatenate([a, a], axis=-1).reshape(nb, gp, LANES)

    def lanes(a):
        a = jnp.broadcast_to(a.reshape(nb, gp, p, 1).transpose(0, 2, 1, 3), (nb, p, gp, cg))
        return a.reshape(nb, p, LANES)

    def ct(cm):
        return cm.reshape(nb, gp, cg, p).transpose(0, 3, 1, 2).reshape(nb, p, LANES)

    p_grp = jnp.stack([grp(a_re), grp(a_im), grp(ldt)], axis=1)
    p_bt = jnp.stack([b_re, b_im], axis=1).transpose(0, 3, 1, 2).reshape(nb, LANES, LANES)
    p_lanes = jnp.stack([lanes(a_re), lanes(a_im), lanes(ldt), ct(c_re), ct(c_im)], axis=1)
    dsk = jnp.tile(d_skip.reshape(nb, 1, LANES), (1, 1, SSM_CHUNK))
    return p_grp, p_bt, p_lanes, dsk


def kernel(x, c, w_ada, b_ada, g_mix, w_in, w_ck1, w_ck2, pe_ck, w_cv1, w_cv2, pe_cv,
           a_re, a_im, log_dt, b_re, b_im, c_re, c_im, d_skip, w_glu, b_glu,
           w_out, g_mlp, w_up, w_down, g_final):
    bsz, seq, _ = x.shape
    assert w_ada.shape[0] == 1, "single-layer block"
    assert seq % 1024 == 0 and (seq & (seq - 1)) == 0
    m = bsz * seq
    x2 = x.reshape(m, D_MODEL)

    c_pad = jnp.pad(c, ((0, 8 - bsz), (0, 0)))
    ada = _ada(c_pad, w_ada[0], b_ada)
    ada3 = ada[:bsz].reshape(bsz * 6, 1, D_MODEL)

    proj = _inproj(x2, g_mix, ada3, _pad_w_in(w_in[0]), seq)

    nc = seq // CMP_STRIDE
    xc = proj[:, OFF_KC:OFF_KS].reshape(bsz, nc, CMP_STRIDE, 2, N_KV_GROUPS, HEAD_DIM)
    xc = xc.transpose(3, 0, 4, 1, 2, 5).reshape(2, bsz, N_KV_GROUPS, nc, CMP_STRIDE * HEAD_DIM)
    w1s = jnp.stack([w_ck1[0], w_cv1[0]]).reshape(2, CMP_LEN * HEAD_DIM, HEAD_DIM).astype(BF16)
    pes = jnp.stack([pe_ck[0], pe_cv[0]]).reshape(2, 1, CMP_LEN * HEAD_DIM)
    pes = jnp.pad(pes, ((0, 0), (0, 7), (0, 0))).astype(BF16)
    w2s = jnp.stack([w_ck2[0], w_cv2[0]]).astype(BF16)
    kvc = _compress(xc, w1s, pes, w2s)

    def heads_t(off):
        return proj[:, off:off + KV_DIM].reshape(bsz, seq, N_KV_GROUPS, HEAD_DIM).transpose(0, 2, 3, 1)

    vct = kvc[1].transpose(0, 1, 3, 2)
    e_tab = (jnp.arange(seq)[:, None] // SEL_BLOCK == jnp.arange(LANES)[None, :]).astype(BF16)
    y_a = _nsa(proj, kvc, vct, heads_t(OFF_VS), heads_t(OFF_VW), e_tab, bsz, seq)

    n_chunks = m // SSM_CHUNK
    ucat = proj[:, OFF_SSM:OFF_GA].reshape(n_chunks, SSM_CHUNK, N_SSM_BLOCKS, LANES)
    ucat = ucat.transpose(2, 0, 1, 3).reshape(N_SSM_BLOCKS, n_chunks, SSM_CHUNK * LANES)
    p_grp, p_bt, p_lanes, dsk = _s5_params(a_re[0], a_im[0], log_dt[0], b_re[0], b_im[0],
                                           c_re[0], c_im[0], d_skip[0])
    ycat = _s5(ucat, p_grp, p_bt, p_lanes, dsk, seq // SSM_CHUNK)
    ys = ycat.reshape(N_SSM_BLOCKS, n_chunks, SSM_CHUNK, LANES).transpose(1, 2, 0, 3).reshape(m, D_MODEL)

    mix = _glu(ys, proj, w_glu[0].astype(BF16), b_glu, y_a)
    h1 = _outproj(mix, w_out[0].astype(BF16), x2, ada3, seq)
    out = _mlp(h1, g_mlp, ada3, w_up[0].astype(BF16), w_down[0].astype(BF16),
               g_final.reshape(1, D_MODEL), seq)
    return out.reshape(bsz, seq, D_MODEL)
```

```python
import functools

import jax
import jax.numpy as jnp
from jax import lax
from jax.experimental import pallas as pl
from jax.experimental.pallas import tpu as pltpu

F32 = jnp.float32
BF16 = jnp.bfloat16

D_MODEL = 2048
N_HEADS = 16
HEAD_DIM = 128
N_KV_GROUPS = 4
GROUP_SIZE = 4
KV_DIM = N_KV_GROUPS * HEAD_DIM
CMP_LEN = 32
CMP_STRIDE = 16
SEL_BLOCK = 64
SEL_TOPN = 16
WINDOW = 512
SSM_GROUP = 16
SSM_NGROUPS = D_MODEL // SSM_GROUP
SSM_STATE = 64
EPS = 1e-6
NEG = -1e30
BIG = 1e6
LOG2E = 1.4426950408889634

LANES = 128
SSM_CHUNK = 16
GROUPS_PER_BLOCK = LANES // SSM_GROUP
N_SSM_BLOCKS = SSM_NGROUPS // GROUPS_PER_BLOCK

OFF_Q = 0
OFF_KC = OFF_Q + D_MODEL
OFF_VC = OFF_KC + KV_DIM
OFF_KS = OFF_VC + KV_DIM
OFF_KW = OFF_KS + KV_DIM
OFF_GATE = OFF_KW + KV_DIM
OFF_SSM = OFF_GATE + N_KV_GROUPS * LANES
OFF_GA = OFF_SSM + D_MODEL
OFF_GB = OFF_GA + D_MODEL
PROJ_PAD = OFF_GB + D_MODEL
RAW_VS = D_MODEL + 3 * KV_DIM
RAW_KW = RAW_VS + KV_DIM
RAW_VW = RAW_KW + KV_DIM
RAW_GATE = RAW_VW + KV_DIM
RAW_SSM = RAW_GATE + 3 * N_HEADS

VMEM_LIMIT = 56 * 1024 * 1024


def _cparams(sem):
    return pltpu.CompilerParams(dimension_semantics=sem, vmem_limit_bytes=VMEM_LIMIT)


def _t(x):
    r, c = x.shape
    rows = []
    for j in range(c // LANES):
        rows.append(jnp.concatenate([x[i * LANES:(i + 1) * LANES, j * LANES:(j + 1) * LANES].T
                                     for i in range(r // LANES)], axis=1))
    return jnp.concatenate(rows, axis=0)


def _nt_dot(a, b):
    return lax.dot_general(a, b, (((1,), (1,)), ((), ())), preferred_element_type=F32)


def _ada_kernel(c_ref, w_ref, b_ref, o_ref):
    c = c_ref[...]
    cond = c * jax.nn.sigmoid(c)
    o_ref[...] = jnp.dot(cond.astype(BF16), w_ref[...].astype(BF16),
                         preferred_element_type=F32) + b_ref[...]


def _ada(c_pad, w_ada, b_ada):
    n = w_ada.shape[1]
    tn = 1024
    return pl.pallas_call(
        _ada_kernel,
        grid=(n // tn,),
        in_specs=[pl.BlockSpec((8, D_MODEL), lambda j: (0, 0)),
                  pl.BlockSpec((D_MODEL, tn), lambda j: (0, j)),
                  pl.BlockSpec((1, tn), lambda j: (0, j))],
        out_specs=pl.BlockSpec((8, tn), lambda j: (0, j)),
        out_shape=jax.ShapeDtypeStruct((8, n), F32),
        compiler_params=_cparams(("parallel",)),
        name="ada",
    )(c_pad, w_ada, b_ada)


INPROJ_TN = 512
N_HEAD_TILES = OFF_GATE // INPROJ_TN
HEAD_SKIP = RAW_VS // INPROJ_TN


def _inproj_kernel(x_ref, g_ref, sc_ref, sh_ref, wh_ref, wg_ref, wt_ref, wvt_ref, o_ref, vt_ref, u_scr):
    j = pl.program_id(1)

    @pl.when(j == 0)
    def _():
        x = x_ref[...]
        ms = jnp.mean(x * x, axis=-1, keepdims=True)
        y = x * lax.rsqrt(ms + EPS) * g_ref[...]
        u_scr[...] = (y * (1.0 + sc_ref[...]) + sh_ref[...]).astype(BF16)
        vt_ref[...] = _nt_dot(wvt_ref[...], u_scr[...]).astype(vt_ref.dtype)

    def emit(w_ref):
        o_ref[...] = jnp.dot(u_scr[...], w_ref[...], preferred_element_type=F32).astype(o_ref.dtype)

    pl.when(j < N_HEAD_TILES)(lambda: emit(wh_ref))
    pl.when(j == N_HEAD_TILES)(lambda: emit(wg_ref))
    pl.when(j > N_HEAD_TILES)(lambda: emit(wt_ref))


def _inproj(x2, g_mix, ada3, w_head, w_gate, w_tail, w_vt, seq):
    m = x2.shape[0]
    tm, tn = 1024, INPROJ_TN
    per_b = seq // tm
    n_tail = w_tail.shape[1] // tn
    return pl.pallas_call(
        _inproj_kernel,
        grid=(m // tm, PROJ_PAD // tn),
        in_specs=[pl.BlockSpec((tm, D_MODEL), lambda i, j: (i, 0)),
                  pl.BlockSpec((1, D_MODEL), lambda i, j: (0, 0)),
                  pl.BlockSpec((None, 1, D_MODEL), lambda i, j: ((i // per_b) * 6 + 1, 0, 0)),
                  pl.BlockSpec((None, 1, D_MODEL), lambda i, j: ((i // per_b) * 6 + 0, 0, 0)),
                  pl.BlockSpec((D_MODEL, tn), lambda i, j: (0, jnp.where(j < HEAD_SKIP, j, HEAD_SKIP + 1))),
                  pl.BlockSpec((D_MODEL, tn), lambda i, j: (0, 0)),
                  pl.BlockSpec((D_MODEL, tn), lambda i, j: (0, jnp.clip(j - N_HEAD_TILES - 1, 0, n_tail - 1))),
                  pl.BlockSpec((2 * KV_DIM, D_MODEL), lambda i, j: (0, 0))],
        out_specs=[pl.BlockSpec((tm, tn), lambda i, j: (i, j)),
                   pl.BlockSpec((2 * KV_DIM, tm), lambda i, j: (0, i))],
        out_shape=[jax.ShapeDtypeStruct((m, PROJ_PAD), BF16),
                   jax.ShapeDtypeStruct((2 * KV_DIM, m), BF16)],
        scratch_shapes=[pltpu.VMEM((tm, D_MODEL), BF16)],
        compiler_params=_cparams(("parallel", "arbitrary")),
        name="inproj",
    )(x2, g_mix, ada3, ada3, w_head, w_gate, w_tail, w_vt)


def _compress_kernel(x_ref, w1_ref, pe_ref, w2_ref, o_ref, ot_ref, xf_scr):
    xf_scr[...] = x_ref[...].astype(F32)
    nc = x_ref.shape[0] // CMP_STRIDE
    x = jnp.concatenate([xf_scr[pl.ds(l, nc, stride=CMP_STRIDE), :].astype(BF16)
                         for l in range(CMP_STRIDE)], axis=1)
    half = CMP_STRIDE * HEAD_DIM
    w1 = w1_ref[...]
    ha = jnp.dot(x, w1[:half], preferred_element_type=F32)
    hb = jnp.dot(x, w1[half:], preferred_element_type=F32)
    hb = pltpu.roll(hb, nc - 1, axis=0)
    pe_t = jnp.dot(pe_ref[...], w1, preferred_element_type=F32)[0:1]
    hid = jax.nn.gelu(ha + hb + pe_t)
    out = jnp.dot(hid.astype(BF16), w2_ref[...], preferred_element_type=F32)
    row = lax.broadcasted_iota(jnp.int32, out.shape, 0)
    out = jnp.where(row < nc - 1, out, 0.0)
    o_ref[...] = out.astype(o_ref.dtype)
    ot_ref[...] = _t(out).astype(ot_ref.dtype)


def _compress(proj, w1s, pes, w2s, bsz, seq):
    nc = seq // CMP_STRIDE
    g = N_KV_GROUPS
    return pl.pallas_call(
        _compress_kernel,
        grid=(2, bsz, g),
        in_specs=[pl.BlockSpec((seq, HEAD_DIM), lambda a, i, j: (i, OFF_KC // HEAD_DIM + a * g + j)),
                  pl.BlockSpec((None, CMP_LEN * HEAD_DIM, HEAD_DIM), lambda a, i, j: (a, 0, 0)),
                  pl.BlockSpec((None, 8, CMP_LEN * HEAD_DIM), lambda a, i, j: (a, 0, 0)),
                  pl.BlockSpec((None, HEAD_DIM, HEAD_DIM), lambda a, i, j: (a, 0, 0))],
        out_specs=[pl.BlockSpec((None, None, None, nc, HEAD_DIM), lambda a, i, j: (a, i, j, 0, 0)),
                   pl.BlockSpec((None, None, None, HEAD_DIM, nc), lambda a, i, j: (a, i, j, 0, 0))],
        out_shape=[jax.ShapeDtypeStruct((2, bsz, g, nc, HEAD_DIM), BF16),
                   jax.ShapeDtypeStruct((2, bsz, g, HEAD_DIM, nc), BF16)],
        scratch_shapes=[pltpu.VMEM((seq, HEAD_DIM), F32)],
        compiler_params=_cparams(("parallel", "parallel", "parallel")),
        name="compress",
    )(proj, w1s, pes, w2s)


def _nsa_kernel(q_ref, gate_ref, kc_ref, vct_ref, ks_ref, e_ref, vst_ref, kw_ref, vwt_ref, o_ref,
                *, tq, tk, seq):
    s0 = pl.program_id(2) * tq
    rows = GROUP_SIZE * tq
    nsel = seq // SEL_BLOCK
    topn = min(SEL_TOPN, nsel)
    qscale = HEAD_DIM ** -0.5 * LOG2E

    q = q_ref[...]
    q4 = jnp.concatenate([q[:, r * HEAD_DIM:(r + 1) * HEAD_DIM] for r in range(GROUP_SIZE)], axis=0)
    q4 = (q4.astype(F32) * qscale).astype(BF16)
    t_lane = s0 + (lax.broadcasted_iota(jnp.int32, (1, rows), 1) & (tq - 1))

    kc = kc_ref[...]
    nc = kc.shape[0]
    sc = _nt_dot(kc, q4)
    n_col = lax.broadcasted_iota(jnp.int32, (nc, 1), 0)
    valid_c = (n_col * CMP_STRIDE + (CMP_LEN - 1)) <= t_lane
    sc = jnp.where(valid_c, sc, NEG)
    mc = jnp.max(sc, axis=0, keepdims=True)
    pc = jnp.where(valid_c, jnp.exp2(sc - mc), 0.0)
    pc = pc * (1.0 / jnp.maximum(jnp.sum(pc, axis=0, keepdims=True), 1e-30))
    o_c = jnp.dot(vct_ref[...], pc.astype(BF16), preferred_element_type=F32)
    psum = pc[:, 0:tq] + pc[:, tq:2 * tq] + pc[:, 2 * tq:3 * tq] + pc[:, 3 * tq:4 * tq]

    jj = lax.broadcasted_iota(jnp.int32, (nsel, nc), 0)
    nn = lax.broadcasted_iota(jnp.int32, (nsel, nc), 1)
    ov = ((nn * CMP_STRIDE < jj * SEL_BLOCK + SEL_BLOCK)
          & (nn * CMP_STRIDE + CMP_LEN > jj * SEL_BLOCK)
          & (nn < nc - 1))
    ov = jnp.where(ov, 1.0, 0.0).astype(BF16)
    p_hi = psum.astype(BF16)
    p_lo = (psum - p_hi.astype(F32)).astype(BF16)
    imp_t = (jnp.dot(ov, p_hi, preferred_element_type=F32)
             + jnp.dot(ov, p_lo, preferred_element_type=F32))

    jcol = lax.broadcasted_iota(jnp.int32, (nsel, tq), 0)
    cur = lax.shift_right_logical(s0 + lax.broadcasted_iota(jnp.int32, (nsel, tq), 1), 6)
    valid = jcol <= cur
    forced = (jcol == 0) | (jcol == cur) | (jcol == cur - 1)
    score = jnp.where(forced, BIG, jnp.where(valid, imp_t, -BIG))

    def rank_of(sc_):
        slabs = []
        for sb in range(nsel // 8):
            sj = sc_[sb * 8:(sb + 1) * 8]
            jr = jcol[sb * 8:(sb + 1) * 8, :LANES]
            rank = jnp.zeros((8, LANES), F32)
            for i in range(nsel):
                si = sc_[i:i + 1]
                if i < sb * 8:
                    cond = si >= sj
                elif i >= sb * 8 + 8:
                    cond = si > sj
                else:
                    cond = (si > sj) | ((jr > i) & (si == sj))
                rank = rank + jnp.where(cond, 1.0, 0.0)
            slabs.append(rank)
        return jnp.concatenate(slabs, axis=0)

    rank = jnp.concatenate([rank_of(score[:, c * LANES:(c + 1) * LANES]) for c in range(tq // LANES)], axis=1)
    bias_t = jnp.where((rank < topn) & valid, 0.0, NEG)
    bias_t = jnp.concatenate([bias_t, jnp.zeros((LANES - nsel, tq), F32)], axis=0)
    bias_q = _t(bias_t).astype(BF16)
    q_aug = jnp.concatenate([q4, jnp.concatenate([bias_q] * GROUP_SIZE, axis=0)], axis=1)

    def flash(lo, hi, key_fn, vt_ref, rhs, mask_fn, carry=None):
        def body(kb, carry):
            m, l, acc = carry
            off = pl.multiple_of(kb * tk, tk)
            s = _nt_dot(key_fn(off), rhs)
            if mask_fn is not None:
                kpos = off + lax.broadcasted_iota(jnp.int32, (tk, 1), 0)
                s = jnp.where(mask_fn(kpos), s, NEG)
            m_new = jnp.maximum(m, jnp.max(s, axis=0, keepdims=True))
            alpha = jnp.exp2(m - m_new)
            p = jnp.exp2(s - m_new)
            l = alpha * l + jnp.sum(p, axis=0, keepdims=True)
            acc = alpha * acc + jnp.dot(vt_ref[:, pl.ds(off, tk)], p.astype(BF16),
                                        preferred_element_type=F32)
            return m_new, l, acc

        if carry is None:
            carry = (jnp.full((1, rows), NEG, F32), jnp.zeros((1, rows), F32),
                     jnp.zeros((HEAD_DIM, rows), F32))
        return lax.fori_loop(lo, hi, body, carry)

    def finish(carry):
        _, l, acc = carry
        return acc * (1.0 / jnp.maximum(l, 1e-30))

    def sel_keys(off):
        return jnp.concatenate([ks_ref[pl.ds(off, tk), :], e_ref[pl.ds(off, tk), :]], axis=1)

    hi = (s0 + tq - 1) // tk + 1
    carry = flash(0, hi - 1, sel_keys, vst_ref, q_aug, None)
    o_s = finish(flash(hi - 1, hi, sel_keys, vst_ref, q_aug, lambda kpos: kpos <= t_lane, carry))

    def win_mask(kpos):
        diff = t_lane - kpos
        return (diff >= 0) & (diff < WINDOW)

    lo_w = jnp.maximum(s0 - (WINDOW - 1), 0) // tk
    o_w = finish(flash(lo_w, hi, lambda off: kw_ref[pl.ds(off, tk), :], vwt_ref, q4, win_mask))

    sg = _t(jax.nn.sigmoid(gate_ref[...].astype(F32)))
    outs = []
    for r in range(GROUP_SIZE):
        sl = slice(r * tq, (r + 1) * tq)
        o_r = (sg[3 * r:3 * r + 1] * o_c[:, sl] + sg[3 * r + 1:3 * r + 2] * o_s[:, sl]
               + sg[3 * r + 2:3 * r + 3] * o_w[:, sl])
        outs.append(_t(o_r))
    o_ref[...] = jnp.concatenate(outs, axis=1).astype(o_ref.dtype)


def _nsa(proj, kvc, kvct, vt, e_tab, bsz, seq):
    tq, tk = 256, 512
    nq = seq // tq
    gw = GROUP_SIZE * HEAD_DIM
    nc = kvc.shape[3]
    g4 = N_KV_GROUPS

    def k_spec(off):
        cb = off // HEAD_DIM
        return pl.BlockSpec((seq, HEAD_DIM), lambda b, g, i: (b, cb + g))

    return pl.pallas_call(
        functools.partial(_nsa_kernel, tq=tq, tk=tk, seq=seq),
        grid=(bsz, g4, nq),
        in_specs=[pl.BlockSpec((tq, gw), lambda b, g, i: (b * nq + i, g)),
                  pl.BlockSpec((tq, LANES), lambda b, g, i: (b * nq + i, OFF_GATE // LANES + g)),
                  pl.BlockSpec((None, None, None, nc, HEAD_DIM), lambda b, g, i: (0, b, g, 0, 0)),
                  pl.BlockSpec((None, None, None, HEAD_DIM, nc), lambda b, g, i: (1, b, g, 0, 0)),
                  k_spec(OFF_KS),
                  pl.BlockSpec((seq, LANES), lambda b, g, i: (0, 0)),
                  pl.BlockSpec((HEAD_DIM, seq), lambda b, g, i: (g, b)),
                  k_spec(OFF_KW),
                  pl.BlockSpec((HEAD_DIM, seq), lambda b, g, i: (g4 + g, b))],
        out_specs=pl.BlockSpec((tq, gw), lambda b, g, i: (b * nq + i, g)),
        out_shape=jax.ShapeDtypeStruct((bsz * seq, D_MODEL), BF16),
        compiler_params=_cparams(("parallel", "parallel", "arbitrary")),
        name="nsa",
    )(proj, proj, kvc, kvct, proj, e_tab, vt, proj, vt)


def _s5_kernel(u_ref, pg_ref, bt_ref, pl_ref, dsk_ref, o_ref, wt_scr, ws_scr, wo_scr, xp_scr, tok_scr,
               *, chunks_per_seq):
    L = SSM_CHUNK
    hp = lax.Precision.HIGHEST
    are_g, aim_g, ldt_g = pg_ref[0], pg_ref[1], pg_ref[2]
    bt = bt_ref[...]
    are_l, aim_l, ldt_l, cr_l, ci_l = pl_ref[0], pl_ref[1], pl_ref[2], pl_ref[3], pl_ref[4]

    lane = lax.broadcasted_iota(jnp.int32, (LANES, LANES), 1)
    rowi = lax.broadcasted_iota(jnp.int32, (LANES, LANES), 0)
    is_re = lane < SSM_STATE
    row_grp = rowi // SSM_GROUP
    lane_grp = lane // SSM_GROUP

    def rows16(z):
        return jnp.concatenate([jnp.broadcast_to(z[g:g + 1], (SSM_GROUP, LANES))
                                for g in range(GROUPS_PER_BLOCK)], axis=0)

    def cmul(x, zr, zi):
        return x * zr + pltpu.roll(x, SSM_STATE, axis=1) * jnp.where(is_re, -zi, zi)

    dt_g = jnp.exp(ldt_g)
    l1r = jnp.exp(are_g * dt_g) * jnp.cos(aim_g * dt_g)
    l1i = jnp.exp(are_g * dt_g) * jnp.sin(aim_g * dt_g)
    den = are_g * are_g + aim_g * aim_g
    nr = l1r - 1.0
    coef_r = (nr * are_g + l1i * aim_g) / den
    coef_i = (l1i * are_g - nr * aim_g) / den
    bbar = cmul(bt, rows16(coef_r), rows16(coef_i))

    pw_g = [(jnp.ones_like(l1r), jnp.zeros_like(l1r))]
    for _ in range(L):
        pr, pi = pw_g[-1]
        pw_g.append((pr * l1r - pi * l1i, pr * l1i + pi * l1r))

    dt_l = jnp.exp(ldt_l)
    m1r = jnp.exp(are_l * dt_l) * jnp.cos(aim_l * dt_l)
    m1i = jnp.exp(are_l * dt_l) * jnp.sin(aim_l * dt_l)
    gs = []
    pr, pi = jnp.ones_like(m1r), jnp.zeros_like(m1r)
    for _ in range(L + 1):
        gs.append(jnp.concatenate([cr_l * pr - ci_l * pi, -(cr_l * pi + ci_l * pr)], axis=0))
        pr, pi = pr * m1r - pi * m1i, pr * m1i + pi * m1r

    same_grp = row_grp == lane_grp
    zeros_tile = jnp.zeros((LANES, LANES), BF16)

    for t in range(L):
        d_t = jnp.dot(bbar, gs[t], precision=hp, preferred_element_type=F32)
        d_t = jnp.where(same_grp, d_t, 0.0).astype(BF16)
        for j in range(L - t):
            i = j + t
            wt_scr[j * LANES:(j + 1) * LANES, i * LANES:(i + 1) * LANES] = d_t
    for j in range(L):
        for i in range(j):
            wt_scr[j * LANES:(j + 1) * LANES, i * LANES:(i + 1) * LANES] = zeros_tile

    for j in range(L):
        zr, zi = pw_g[L - 1 - j]
        s_j = cmul(bbar, rows16(zr), rows16(zi))
        for g in range(GROUPS_PER_BLOCK):
            ws_scr[j * LANES:(j + 1) * LANES, g * LANES:(g + 1) * LANES] = (
                jnp.where(row_grp == g, s_j, 0.0).astype(BF16))

    for g in range(GROUPS_PER_BLOCK):
        for i in range(L):
            wo_scr[g * LANES:(g + 1) * LANES, i * LANES:(i + 1) * LANES] = (
                jnp.where(lane_grp == g, gs[i + 1], 0.0).astype(BF16))

    tok_scr[...] = u_ref[...].astype(F32)
    nrow = u_ref.shape[0] // L
    u = jnp.concatenate([tok_scr[pl.ds(j, nrow, stride=L), :].astype(BF16) for j in range(L)], axis=1)
    y = jnp.dot(u, wt_scr[...], preferred_element_type=F32)
    st = jnp.dot(u, ws_scr[...], preferred_element_type=F32)

    kk = lax.broadcasted_iota(jnp.int32, (nrow, 1), 0) & (chunks_per_seq - 1)
    lane1 = lax.broadcasted_iota(jnp.int32, (1, LANES), 1)
    for g in range(GROUPS_PER_BLOCK):
        xs = st[:, g * LANES:(g + 1) * LANES]
        zr = pw_g[L][0][g:g + 1]
        zi = pw_g[L][1][g:g + 1]
        d = 1
        while d < chunks_per_seq:
            sh = jnp.where(kk >= d, pltpu.roll(xs, d, axis=0), 0.0)
            zmix = jnp.where(lane1 < SSM_STATE, -zi, zi)
            xs = xs + sh * zr + pltpu.roll(sh, SSM_STATE, axis=1) * zmix
            zr, zi = zr * zr - zi * zi, 2.0 * zr * zi
            d *= 2
        prev = jnp.where(kk >= 1, pltpu.roll(xs, 1, axis=0), 0.0)
        xp_scr[:, g * LANES:(g + 1) * LANES] = prev.astype(BF16)

    y = y + jnp.dot(xp_scr[...], wo_scr[...], preferred_element_type=F32)
    y = jax.nn.gelu(y + dsk_ref[...] * u.astype(F32))
    for i in range(L):
        tok_scr[pl.ds(i, nrow, stride=L), :] = y[:, i * LANES:(i + 1) * LANES]
    o_ref[...] = tok_scr[...].astype(o_ref.dtype)


def _s5(proj, p_grp, p_bt, p_lanes, dsk, chunks_per_seq):
    m = proj.shape[0]
    nrow, kk = m // SSM_CHUNK, SSM_CHUNK * LANES
    nst = GROUPS_PER_BLOCK * LANES
    return pl.pallas_call(
        functools.partial(_s5_kernel, chunks_per_seq=chunks_per_seq),
        grid=(N_SSM_BLOCKS,),
        in_specs=[pl.BlockSpec((m, LANES), lambda i: (0, OFF_SSM // LANES + i)),
                  pl.BlockSpec((None, 3, GROUPS_PER_BLOCK, LANES), lambda i: (i, 0, 0, 0)),
                  pl.BlockSpec((None, LANES, LANES), lambda i: (i, 0, 0)),
                  pl.BlockSpec((None, 5, SSM_STATE, LANES), lambda i: (i, 0, 0, 0)),
                  pl.BlockSpec((None, 1, kk), lambda i: (i, 0, 0))],
        out_specs=pl.BlockSpec((m, LANES), lambda i: (0, i)),
        out_shape=jax.ShapeDtypeStruct((m, D_MODEL), BF16),
        scratch_shapes=[pltpu.VMEM((kk, kk), BF16), pltpu.VMEM((kk, nst), BF16),
                        pltpu.VMEM((nst, kk), BF16), pltpu.VMEM((nrow, nst), BF16),
                        pltpu.VMEM((m, LANES), F32)],
        compiler_params=_cparams(("parallel",)),
        name="s5",
    )(proj, p_grp, p_bt, p_lanes, dsk)


def _glu_kernel(ys_ref, w_ref, b_ref, ya_ref, ga_ref, gb_ref, o_ref, *, tn):
    j = pl.program_id(1)
    z = jnp.dot(ys_ref[...], w_ref[...], preferred_element_type=F32) + b_ref[...]
    ys_n = ys_ref[:, pl.ds(pl.multiple_of(j * tn, tn), tn)].astype(F32)
    yb = ys_n * jax.nn.sigmoid(z)
    out = (jax.nn.sigmoid(ga_ref[...].astype(F32)) * ya_ref[...].astype(F32)
           + jax.nn.sigmoid(gb_ref[...].astype(F32)) * yb)
    o_ref[...] = out.astype(o_ref.dtype)


def _glu(ys, proj, w_glu, b_glu, ya):
    m = ys.shape[0]
    tm, tn = 512, 512
    return pl.pallas_call(
        functools.partial(_glu_kernel, tn=tn),
        grid=(m // tm, D_MODEL // tn),
        in_specs=[pl.BlockSpec((tm, D_MODEL), lambda i, j: (i, 0)),
                  pl.BlockSpec((D_MODEL, tn), lambda i, j: (0, j)),
                  pl.BlockSpec((1, tn), lambda i, j: (0, j)),
                  pl.BlockSpec((tm, tn), lambda i, j: (i, j)),
                  pl.BlockSpec((tm, tn), lambda i, j: (i, OFF_GA // tn + j)),
                  pl.BlockSpec((tm, tn), lambda i, j: (i, OFF_GB // tn + j))],
        out_specs=pl.BlockSpec((tm, tn), lambda i, j: (i, j)),
        out_shape=jax.ShapeDtypeStruct((m, D_MODEL), BF16),
        compiler_params=_cparams(("parallel", "parallel")),
        name="glu",
    )(ys, w_glu, b_glu, ya, proj, proj)


def _outproj_kernel(m_ref, w_ref, x_ref, gt_ref, o_ref):
    acc = jnp.dot(m_ref[...], w_ref[...], preferred_element_type=F32)
    o_ref[...] = x_ref[...] + gt_ref[...] * acc


def _outproj(mix, w_out, x2, ada3, seq):
    m = mix.shape[0]
    tm, tn = 512, 512
    per_b = seq // tm
    return pl.pallas_call(
        _outproj_kernel,
        grid=(m // tm, D_MODEL // tn),
        in_specs=[pl.BlockSpec((tm, D_MODEL), lambda i, j: (i, 0)),
                  pl.BlockSpec((D_MODEL, tn), lambda i, j: (0, j)),
                  pl.BlockSpec((tm, tn), lambda i, j: (i, j)),
                  pl.BlockSpec((None, 1, tn), lambda i, j: ((i // per_b) * 6 + 2, 0, j))],
        out_specs=pl.BlockSpec((tm, tn), lambda i, j: (i, j)),
        out_shape=jax.ShapeDtypeStruct((m, D_MODEL), F32),
        compiler_params=_cparams(("parallel", "parallel")),
        name="outproj",
    )(mix, w_out, x2, ada3)


def _mlp_kernel(h_ref, g_ref, sc_ref, sh_ref, gt_ref, wu_ref, wd_ref, gf_ref, o_ref, u_scr, acc_scr):
    f = pl.program_id(1)

    @pl.when(f == 0)
    def _():
        h = h_ref[...]
        ms = jnp.mean(h * h, axis=-1, keepdims=True)
        y = h * lax.rsqrt(ms + EPS) * g_ref[...]
        u_scr[...] = (y * (1.0 + sc_ref[...]) + sh_ref[...]).astype(BF16)
        acc_scr[...] = jnp.zeros_like(acc_scr)

    a = jnp.dot(u_scr[...], wu_ref[...], preferred_element_type=F32)
    a = jnp.square(jnp.maximum(a, 0.0))
    acc_scr[...] += jnp.dot(a.astype(BF16), wd_ref[...], preferred_element_type=F32)

    @pl.when(f == pl.num_programs(1) - 1)
    def _():
        h2 = h_ref[...] + gt_ref[...] * acc_scr[...]
        ms = jnp.mean(h2 * h2, axis=-1, keepdims=True)
        o_ref[...] = h2 * lax.rsqrt(ms + EPS) * gf_ref[...]


def _mlp(h1, g_mlp, ada3, w_up, w_down, g_final, seq):
    m = h1.shape[0]
    tm, tf = 512, 512
    per_b = seq // tm
    ff = w_up.shape[1]

    def ada_spec(k):
        return pl.BlockSpec((None, 1, D_MODEL), lambda i, f: ((i // per_b) * 6 + k, 0, 0))

    return pl.pallas_call(
        _mlp_kernel,
        grid=(m // tm, ff // tf),
        in_specs=[pl.BlockSpec((tm, D_MODEL), lambda i, f: (i, 0)),
                  pl.BlockSpec((1, D_MODEL), lambda i, f: (0, 0)),
                  ada_spec(4), ada_spec(3), ada_spec(5),
                  pl.BlockSpec((D_MODEL, tf), lambda i, f: (0, f)),
                  pl.BlockSpec((tf, D_MODEL), lambda i, f: (f, 0)),
                  pl.BlockSpec((1, D_MODEL), lambda i, f: (0, 0))],
        out_specs=pl.BlockSpec((tm, D_MODEL), lambda i, f: (i, 0)),
        out_shape=jax.ShapeDtypeStruct((m, D_MODEL), F32),
        scratch_shapes=[pltpu.VMEM((tm, D_MODEL), BF16), pltpu.VMEM((tm, D_MODEL), F32)],
        compiler_params=_cparams(("parallel", "arbitrary")),
        name="mlp",
    )(h1, g_mlp, ada3, ada3, ada3, w_up, w_down, g_final)


def _split_w_in(w):
    wg = w[:, RAW_GATE:RAW_SSM].reshape(D_MODEL, N_KV_GROUPS, 3 * GROUP_SIZE)
    wg = jnp.pad(wg, ((0, 0), (0, 0), (0, LANES - 3 * GROUP_SIZE))).reshape(D_MODEL, N_KV_GROUPS * LANES)
    w_vt = jnp.concatenate([w[:, RAW_VS:RAW_KW], w[:, RAW_VW:RAW_GATE]], axis=1).T
    return (w[:, :RAW_GATE].astype(BF16), wg.astype(BF16), w[:, RAW_SSM:].astype(BF16), w_vt.astype(BF16))


def _s5_params(a_re, a_im, log_dt, b_re, b_im, c_re, c_im, d_skip):
    nb, gp, p, cg = N_SSM_BLOCKS, GROUPS_PER_BLOCK, SSM_STATE, SSM_GROUP
    ldt = jnp.broadcast_to(log_dt[:, None], (SSM_NGROUPS, p))

    def grp(a):
        return jnp.concatenate([a, a], axis=-1).reshape(nb, gp, LANES)

    def lanes(a):
        a = jnp.broadcast_to(a.reshape(nb, gp, p, 1).transpose(0, 2, 1, 3), (nb, p, gp, cg))
        return a.reshape(nb, p, LANES)

    def ct(cm):
        return cm.reshape(nb, gp, cg, p).transpose(0, 3, 1, 2).reshape(nb, p, LANES)

    p_grp = jnp.stack([grp(a_re), grp(a_im), grp(ldt)], axis=1)
    p_bt = jnp.stack([b_re, b_im], axis=1).transpose(0, 3, 1, 2).reshape(nb, LANES, LANES)
    p_lanes = jnp.stack([lanes(a_re), lanes(a_im), lanes(ldt), ct(c_re), ct(c_im)], axis=1)
    dsk = jnp.tile(d_skip.reshape(nb, 1, LANES), (1, 1, SSM_CHUNK))
    return p_grp, p_bt, p_lanes, dsk


def kernel(x, c, w_ada, b_ada, g_mix, w_in, w_ck1, w_ck2, pe_ck, w_cv1, w_cv2, pe_cv,
           a_re, a_im, log_dt, b_re, b_im, c_re, c_im, d_skip, w_glu, b_glu,
           w_out, g_mlp, w_up, w_down, g_final):
    bsz, seq, _ = x.shape
    assert w_ada.shape[0] == 1, "single-layer block"
    assert seq % 1024 == 0 and (seq & (seq - 1)) == 0
    m = bsz * seq
    x2 = x.reshape(m, D_MODEL)

    c_pad = jnp.pad(c, ((0, 8 - bsz), (0, 0)))
    ada = _ada(c_pad, w_ada[0], b_ada)
    ada3 = ada[:bsz].reshape(bsz * 6, 1, D_MODEL)

    proj, vt = _inproj(x2, g_mix, ada3, *_split_w_in(w_in[0]), seq)

    w1s = jnp.stack([w_ck1[0], w_cv1[0]]).reshape(2, CMP_LEN * HEAD_DIM, HEAD_DIM).astype(BF16)
    pes = jnp.stack([pe_ck[0], pe_cv[0]]).reshape(2, 1, CMP_LEN * HEAD_DIM)
    pes = jnp.pad(pes, ((0, 0), (0, 7), (0, 0))).astype(BF16)
    w2s = jnp.stack([w_ck2[0], w_cv2[0]]).astype(BF16)
    kvc, kvct = _compress(proj, w1s, pes, w2s, bsz, seq)

    e_tab = (jnp.arange(seq)[:, None] // SEL_BLOCK == jnp.arange(LANES)[None, :]).astype(BF16)
    y_a = _nsa(proj, kvc, kvct, vt, e_tab, bsz, seq)

    p_grp, p_bt, p_lanes, dsk = _s5_params(a_re[0], a_im[0], log_dt[0], b_re[0], b_im[0],
                                           c_re[0], c_im[0], d_skip[0])
    ys = _s5(proj, p_grp, p_bt, p_lanes, dsk, seq // SSM_CHUNK)

    mix = _glu(ys, proj, w_glu[0].astype(BF16), b_glu, y_a)
    h1 = _outproj(mix, w_out[0].astype(BF16), x2, ada3, seq)
    out = _mlp(h1, g_mlp, ada3, w_up[0].astype(BF16), w_down[0].astype(BF16),
               g_final.reshape(1, D_MODEL), seq)
    return out.reshape(bsz, seq, D_MODEL)
```

```python
import functools

import jax
import jax.numpy as jnp
from jax import lax
from jax.experimental import pallas as pl
from jax.experimental.pallas import tpu as pltpu

F32 = jnp.float32
BF16 = jnp.bfloat16

D_MODEL = 2048
N_HEADS = 16
HEAD_DIM = 128
N_KV_GROUPS = 4
GROUP_SIZE = 4
KV_DIM = N_KV_GROUPS * HEAD_DIM
CMP_LEN = 32
CMP_STRIDE = 16
SEL_BLOCK = 64
SEL_TOPN = 16
WINDOW = 512
SSM_GROUP = 16
SSM_NGROUPS = D_MODEL // SSM_GROUP
SSM_STATE = 64
EPS = 1e-6
NEG = -1e30
BIG = 1e6
LOG2E = 1.4426950408889634
ONES_ROWS = 16

LANES = 128
SSM_CHUNK = 16
GROUPS_PER_BLOCK = LANES // SSM_GROUP
N_SSM_BLOCKS = SSM_NGROUPS // GROUPS_PER_BLOCK

OFF_Q = 0
OFF_KC = OFF_Q + D_MODEL
OFF_VC = OFF_KC + KV_DIM
OFF_KS = OFF_VC + KV_DIM
OFF_KW = OFF_KS + KV_DIM
OFF_GATE = OFF_KW + KV_DIM
OFF_SSM = OFF_GATE + N_KV_GROUPS * LANES
OFF_GA = OFF_SSM + D_MODEL
OFF_GB = OFF_GA + D_MODEL
PROJ_PAD = OFF_GB + D_MODEL
RAW_VS = D_MODEL + 3 * KV_DIM
RAW_KW = RAW_VS + KV_DIM
RAW_VW = RAW_KW + KV_DIM
RAW_GATE = RAW_VW + KV_DIM
RAW_SSM = RAW_GATE + 3 * N_HEADS

VMEM_LIMIT = 56 * 1024 * 1024


def _cparams(sem):
    return pltpu.CompilerParams(dimension_semantics=sem, vmem_limit_bytes=VMEM_LIMIT)


def _t(x):
    r, c = x.shape
    rows = []
    for j in range(c // LANES):
        rows.append(jnp.concatenate([x[i * LANES:(i + 1) * LANES, j * LANES:(j + 1) * LANES].T
                                     for i in range(r // LANES)], axis=1))
    return jnp.concatenate(rows, axis=0)


def _nt_dot(a, b):
    return lax.dot_general(a, b, (((1,), (1,)), ((), ())), preferred_element_type=F32)


def _ada_kernel(c_ref, w_ref, b_ref, o_ref):
    c = c_ref[...]
    cond = c * jax.nn.sigmoid(c)
    o_ref[...] = jnp.dot(cond.astype(BF16), w_ref[...].astype(BF16),
                         preferred_element_type=F32) + b_ref[...]


def _ada(c_pad, w_ada, b_ada):
    n = w_ada.shape[1]
    tn = 1024
    return pl.pallas_call(
        _ada_kernel,
        grid=(n // tn,),
        in_specs=[pl.BlockSpec((8, D_MODEL), lambda j: (0, 0)),
                  pl.BlockSpec((D_MODEL, tn), lambda j: (0, j)),
                  pl.BlockSpec((1, tn), lambda j: (0, j))],
        out_specs=pl.BlockSpec((8, tn), lambda j: (0, j)),
        out_shape=jax.ShapeDtypeStruct((8, n), F32),
        compiler_params=_cparams(("parallel",)),
        name="ada",
    )(c_pad, w_ada, b_ada)


INPROJ_TN = 512
N_HEAD_TILES = OFF_GATE // INPROJ_TN
HEAD_SKIP = RAW_VS // INPROJ_TN


def _inproj_kernel(x_ref, g_ref, sc_ref, sh_ref, wh_ref, wg_ref, wt_ref, wvt_ref, o_ref, vt_ref, u_scr):
    j = pl.program_id(1)

    @pl.when(j == 0)
    def _():
        x = x_ref[...]
        ms = jnp.mean(x * x, axis=-1, keepdims=True)
        y = x * lax.rsqrt(ms + EPS) * g_ref[...]
        u_scr[...] = (y * (1.0 + sc_ref[...]) + sh_ref[...]).astype(BF16)
        vt_ref[...] = _nt_dot(wvt_ref[...], u_scr[...]).astype(vt_ref.dtype)

    def emit(w_ref):
        o_ref[...] = jnp.dot(u_scr[...], w_ref[...], preferred_element_type=F32).astype(o_ref.dtype)

    pl.when(j < N_HEAD_TILES)(lambda: emit(wh_ref))
    pl.when(j == N_HEAD_TILES)(lambda: emit(wg_ref))
    pl.when(j > N_HEAD_TILES)(lambda: emit(wt_ref))


def _inproj(x2, g_mix, ada3, w_head, w_gate, w_tail, w_vt, seq):
    m = x2.shape[0]
    tm, tn = 1024, INPROJ_TN
    per_b = seq // tm
    n_tail = w_tail.shape[1] // tn
    return pl.pallas_call(
        _inproj_kernel,
        grid=(m // tm, PROJ_PAD // tn),
        in_specs=[pl.BlockSpec((tm, D_MODEL), lambda i, j: (i, 0)),
                  pl.BlockSpec((1, D_MODEL), lambda i, j: (0, 0)),
                  pl.BlockSpec((None, 1, D_MODEL), lambda i, j: ((i // per_b) * 6 + 1, 0, 0)),
                  pl.BlockSpec((None, 1, D_MODEL), lambda i, j: ((i // per_b) * 6 + 0, 0, 0)),
                  pl.BlockSpec((D_MODEL, tn), lambda i, j: (0, jnp.where(j < HEAD_SKIP, j, HEAD_SKIP + 1))),
                  pl.BlockSpec((D_MODEL, tn), lambda i, j: (0, 0)),
                  pl.BlockSpec((D_MODEL, tn), lambda i, j: (0, jnp.clip(j - N_HEAD_TILES - 1, 0, n_tail - 1))),
                  pl.BlockSpec((2 * KV_DIM, D_MODEL), lambda i, j: (0, 0))],
        out_specs=[pl.BlockSpec((tm, tn), lambda i, j: (i, j)),
                   pl.BlockSpec((2 * KV_DIM, tm), lambda i, j: (0, i))],
        out_shape=[jax.ShapeDtypeStruct((m, PROJ_PAD), BF16),
                   jax.ShapeDtypeStruct((2 * KV_DIM, m), BF16)],
        scratch_shapes=[pltpu.VMEM((tm, D_MODEL), BF16)],
        compiler_params=_cparams(("parallel", "arbitrary")),
        name="inproj",
    )(x2, g_mix, ada3, ada3, w_head, w_gate, w_tail, w_vt)


def _compress_kernel(x_ref, w1_ref, pe_ref, w2_ref, o_ref, ot_ref, xf_scr):
    xf_scr[...] = x_ref[...].astype(F32)
    nc = x_ref.shape[0] // CMP_STRIDE
    x = jnp.concatenate([xf_scr[pl.ds(l, nc, stride=CMP_STRIDE), :].astype(BF16)
                         for l in range(CMP_STRIDE)], axis=1)
    half = CMP_STRIDE * HEAD_DIM
    w1 = w1_ref[...]
    ha = jnp.dot(x, w1[:half], preferred_element_type=F32)
    hb = jnp.dot(x, w1[half:], preferred_element_type=F32)
    hb = pltpu.roll(hb, nc - 1, axis=0)
    pe_t = jnp.dot(pe_ref[...], w1, preferred_element_type=F32)[0:1]
    hid = jax.nn.gelu(ha + hb + pe_t)
    out = jnp.dot(hid.astype(BF16), w2_ref[...], preferred_element_type=F32)
    row = lax.broadcasted_iota(jnp.int32, out.shape, 0)
    out = jnp.where(row < nc - 1, out, 0.0)
    o_ref[...] = out.astype(o_ref.dtype)
    ot_ref[...] = _t(out).astype(ot_ref.dtype)


def _compress(proj, w1s, pes, w2s, bsz, seq):
    nc = seq // CMP_STRIDE
    g = N_KV_GROUPS
    return pl.pallas_call(
        _compress_kernel,
        grid=(2, bsz, g),
        in_specs=[pl.BlockSpec((seq, HEAD_DIM), lambda a, i, j: (i, OFF_KC // HEAD_DIM + a * g + j)),
                  pl.BlockSpec((None, CMP_LEN * HEAD_DIM, HEAD_DIM), lambda a, i, j: (a, 0, 0)),
                  pl.BlockSpec((None, 8, CMP_LEN * HEAD_DIM), lambda a, i, j: (a, 0, 0)),
                  pl.BlockSpec((None, HEAD_DIM, HEAD_DIM), lambda a, i, j: (a, 0, 0))],
        out_specs=[pl.BlockSpec((None, None, None, nc, HEAD_DIM), lambda a, i, j: (a, i, j, 0, 0)),
                   pl.BlockSpec((None, None, None, HEAD_DIM, nc), lambda a, i, j: (a, i, j, 0, 0))],
        out_shape=[jax.ShapeDtypeStruct((2, bsz, g, nc, HEAD_DIM), BF16),
                   jax.ShapeDtypeStruct((2, bsz, g, HEAD_DIM, nc), BF16)],
        scratch_shapes=[pltpu.VMEM((seq, HEAD_DIM), F32)],
        compiler_params=_cparams(("parallel", "parallel", "parallel")),
        name="compress",
    )(proj, w1s, pes, w2s)


def _nsa_kernel(q_ref, gate_ref, kc_ref, vct_ref, ks_ref, e_ref, vst_ref, kw_ref, vwt_ref, o_ref,
                sa_scr, sb_scr, m_scr, acc_scr, *, tq, tk, seq):
    s0 = pl.program_id(2) * tq
    rows = GROUP_SIZE * tq
    nsel = seq // SEL_BLOCK
    topn = min(SEL_TOPN, nsel)
    qscale = HEAD_DIM ** -0.5 * LOG2E

    q = q_ref[...]
    q4 = jnp.concatenate([q[:, r * HEAD_DIM:(r + 1) * HEAD_DIM] for r in range(GROUP_SIZE)], axis=0)
    q4 = (q4.astype(F32) * qscale).astype(BF16)
    t_lane = s0 + (lax.broadcasted_iota(jnp.int32, (1, rows), 1) & (tq - 1))

    kc = kc_ref[...]
    nc = kc.shape[0]
    sc = _nt_dot(kc, q4)
    n_col = lax.broadcasted_iota(jnp.int32, (nc, 1), 0)
    valid_c = (n_col * CMP_STRIDE + (CMP_LEN - 1)) <= t_lane
    sc = jnp.where(valid_c, sc, NEG)
    mc = jnp.max(sc, axis=0, keepdims=True)
    pc = jnp.where(valid_c, jnp.exp2(sc - mc), 0.0)
    pc = pc * (1.0 / jnp.maximum(jnp.sum(pc, axis=0, keepdims=True), 1e-30))
    o_c = jnp.dot(vct_ref[...], pc.astype(BF16), preferred_element_type=F32)
    psum = pc[:, 0:tq] + pc[:, tq:2 * tq] + pc[:, 2 * tq:3 * tq] + pc[:, 3 * tq:4 * tq]

    jj = lax.broadcasted_iota(jnp.int32, (nsel, nc), 0)
    nn = lax.broadcasted_iota(jnp.int32, (nsel, nc), 1)
    ov = ((nn * CMP_STRIDE < jj * SEL_BLOCK + SEL_BLOCK)
          & (nn * CMP_STRIDE + CMP_LEN > jj * SEL_BLOCK)
          & (nn < nc - 1))
    ov = jnp.where(ov, 1.0, 0.0).astype(BF16)
    p_hi = psum.astype(BF16)
    p_lo = (psum - p_hi.astype(F32)).astype(BF16)
    imp_t = (jnp.dot(ov, p_hi, preferred_element_type=F32)
             + jnp.dot(ov, p_lo, preferred_element_type=F32))

    jcol = lax.broadcasted_iota(jnp.int32, (nsel, tq), 0)
    cur = lax.shift_right_logical(s0 + lax.broadcasted_iota(jnp.int32, (nsel, tq), 1), 6)
    valid = jcol <= cur
    forced = (jcol == 0) | (jcol == cur) | (jcol == cur - 1)
    score = jnp.where(forced, BIG, jnp.where(valid, imp_t, -BIG))

    def rank_of(sc_):
        slabs = []
        for sb in range(nsel // 8):
            sj = sc_[sb * 8:(sb + 1) * 8]
            jr = jcol[sb * 8:(sb + 1) * 8, :LANES]
            rank = jnp.zeros((8, LANES), F32)
            for i in range(nsel):
                si = sc_[i:i + 1]
                if i < sb * 8:
                    cond = si >= sj
                elif i >= sb * 8 + 8:
                    cond = si > sj
                else:
                    cond = (si > sj) | ((jr > i) & (si == sj))
                rank = rank + jnp.where(cond, 1.0, 0.0)
            slabs.append(rank)
        return jnp.concatenate(slabs, axis=0)

    rank = jnp.concatenate([rank_of(score[:, c * LANES:(c + 1) * LANES]) for c in range(tq // LANES)], axis=1)
    bias_t = jnp.where((rank < topn) & valid, 0.0, NEG)
    bias_t = jnp.concatenate([bias_t, jnp.zeros((LANES - nsel, tq), F32)], axis=0)
    bias_q = _t(bias_t).astype(BF16)
    q_aug = jnp.concatenate([q4, jnp.concatenate([bias_q] * GROUP_SIZE, axis=0)], axis=1)

    def chunk_off(kb):
        return pl.multiple_of(kb * tk, tk)

    ones_rows = jnp.ones((ONES_ROWS, tk), BF16)

    def reset_stats():
        m_scr[...] = jnp.full((1, rows), NEG, F32)
        acc_scr[...] = jnp.zeros((HEAD_DIM + ONES_ROWS, rows), F32)

    def update(s, kb, vt_ref, mask):
        off = chunk_off(kb)
        if mask is not None:
            kpos = off + lax.broadcasted_iota(jnp.int32, (tk, 1), 0)
            s = jnp.where(mask(kpos), s, NEG)
        m = m_scr[...]
        m_new = jnp.maximum(m, jnp.max(s, axis=0, keepdims=True))
        p = jnp.exp2(s - m_new).astype(BF16)
        vt1 = jnp.concatenate([vt_ref[:, pl.ds(off, tk)], ones_rows], axis=0)
        acc_scr[...] = jnp.exp2(m - m_new) * acc_scr[...] + jnp.dot(vt1, p, preferred_element_type=F32)
        m_scr[...] = m_new

    def result():
        acc = acc_scr[...]
        return acc[:HEAD_DIM] * (1.0 / jnp.maximum(acc[HEAD_DIM:HEAD_DIM + 1], 1e-30))

    hi = (s0 + tq - 1) // tk + 1
    causal = lambda kpos: kpos <= t_lane

    k_lo = hi - 2
    w_lo = jnp.where(k_lo >= 0, t_lane - (WINDOW - 1), seq)
    k_lo = jnp.maximum(k_lo, 0)
    reset_stats()
    update(_nt_dot(kw_ref[pl.ds(chunk_off(k_lo), tk), :], q4), k_lo, vwt_ref, lambda kpos: kpos >= w_lo)
    update(_nt_dot(kw_ref[pl.ds(chunk_off(hi - 1), tk), :], q4), hi - 1, vwt_ref, causal)
    o_w = result()

    def sel_scores_into(dst, kb):
        off = chunk_off(kb)
        keys = jnp.concatenate([ks_ref[pl.ds(off, tk), :], e_ref[pl.ds(off, tk), :]], axis=1)
        dst[...] = _nt_dot(keys, q_aug)

    reset_stats()
    sel_scores_into(sa_scr, 0)
    n_pairs = lax.shift_right_logical(hi - 1, 1)

    def pair(i, c):
        k0 = 2 * i
        sel_scores_into(sb_scr, k0 + 1)
        update(sa_scr[...], k0, vst_ref, None)
        sel_scores_into(sa_scr, k0 + 2)
        update(sb_scr[...], k0 + 1, vst_ref, None)
        return c

    lax.fori_loop(0, n_pairs, pair, 0)

    @pl.when(((hi - 1) & 1) == 1)
    def _():
        sel_scores_into(sb_scr, hi - 1)
        update(sa_scr[...], hi - 2, vst_ref, None)
        update(sb_scr[...], hi - 1, vst_ref, causal)

    @pl.when(((hi - 1) & 1) == 0)
    def _():
        update(sa_scr[...], hi - 1, vst_ref, causal)

    o_s = result()

    sg = _t(jax.nn.sigmoid(gate_ref[...].astype(F32)))
    outs = []
    for r in range(GROUP_SIZE):
        sl = slice(r * tq, (r + 1) * tq)
        o_r = (sg[3 * r:3 * r + 1] * o_c[:, sl] + sg[3 * r + 1:3 * r + 2] * o_s[:, sl]
               + sg[3 * r + 2:3 * r + 3] * o_w[:, sl])
        outs.append(_t(o_r))
    o_ref[...] = jnp.concatenate(outs, axis=1).astype(o_ref.dtype)


def _nsa(proj, kvc, kvct, vt, e_tab, bsz, seq):
    tq, tk = 256, 512
    assert tk == WINDOW and tk % tq == 0
    nq = seq // tq
    gw = GROUP_SIZE * HEAD_DIM
    nc = kvc.shape[3]
    g4 = N_KV_GROUPS

    def k_spec(off):
        cb = off // HEAD_DIM
        return pl.BlockSpec((seq, HEAD_DIM), lambda b, g, i: (b, cb + g))

    return pl.pallas_call(
        functools.partial(_nsa_kernel, tq=tq, tk=tk, seq=seq),
        grid=(bsz, g4, nq),
        in_specs=[pl.BlockSpec((tq, gw), lambda b, g, i: (b * nq + i, g)),
                  pl.BlockSpec((tq, LANES), lambda b, g, i: (b * nq + i, OFF_GATE // LANES + g)),
                  pl.BlockSpec((None, None, None, nc, HEAD_DIM), lambda b, g, i: (0, b, g, 0, 0)),
                  pl.BlockSpec((None, None, None, HEAD_DIM, nc), lambda b, g, i: (1, b, g, 0, 0)),
                  k_spec(OFF_KS),
                  pl.BlockSpec((seq, LANES), lambda b, g, i: (0, 0)),
                  pl.BlockSpec((HEAD_DIM, seq), lambda b, g, i: (g, b)),
                  k_spec(OFF_KW),
                  pl.BlockSpec((HEAD_DIM, seq), lambda b, g, i: (g4 + g, b))],
        out_specs=pl.BlockSpec((tq, gw), lambda b, g, i: (b * nq + i, g)),
        out_shape=jax.ShapeDtypeStruct((bsz * seq, D_MODEL), BF16),
        scratch_shapes=[pltpu.VMEM((tk, GROUP_SIZE * tq), F32), pltpu.VMEM((tk, GROUP_SIZE * tq), F32),
                        pltpu.VMEM((1, GROUP_SIZE * tq), F32),
                        pltpu.VMEM((HEAD_DIM + ONES_ROWS, GROUP_SIZE * tq), F32)],
        compiler_params=_cparams(("parallel", "parallel", "arbitrary")),
        name="nsa",
    )(proj, proj, kvc, kvct, proj, e_tab, vt, proj, vt)


def _s5_kernel(u_ref, pg_ref, bt_ref, pl_ref, dsk_ref, o_ref, wt_scr, ws_scr, wo_scr, xp_scr, tok_scr,
               *, chunks_per_seq):
    L = SSM_CHUNK
    hp = lax.Precision.HIGHEST
    are_g, aim_g, ldt_g = pg_ref[0], pg_ref[1], pg_ref[2]
    bt = bt_ref[...]
    are_l, aim_l, ldt_l, cr_l, ci_l = pl_ref[0], pl_ref[1], pl_ref[2], pl_ref[3], pl_ref[4]

    lane = lax.broadcasted_iota(jnp.int32, (LANES, LANES), 1)
    rowi = lax.broadcasted_iota(jnp.int32, (LANES, LANES), 0)
    is_re = lane < SSM_STATE
    row_grp = rowi // SSM_GROUP
    lane_grp = lane // SSM_GROUP

    def rows16(z):
        return jnp.concatenate([jnp.broadcast_to(z[g:g + 1], (SSM_GROUP, LANES))
                                for g in range(GROUPS_PER_BLOCK)], axis=0)

    def cmul(x, zr, zi):
        return x * zr + pltpu.roll(x, SSM_STATE, axis=1) * jnp.where(is_re, -zi, zi)

    dt_g = jnp.exp(ldt_g)
    l1r = jnp.exp(are_g * dt_g) * jnp.cos(aim_g * dt_g)
    l1i = jnp.exp(are_g * dt_g) * jnp.sin(aim_g * dt_g)
    den = are_g * are_g + aim_g * aim_g
    nr = l1r - 1.0
    coef_r = (nr * are_g + l1i * aim_g) / den
    coef_i = (l1i * are_g - nr * aim_g) / den
    bbar = cmul(bt, rows16(coef_r), rows16(coef_i))

    pw_g = [(jnp.ones_like(l1r), jnp.zeros_like(l1r))]
    for _ in range(L):
        pr, pi = pw_g[-1]
        pw_g.append((pr * l1r - pi * l1i, pr * l1i + pi * l1r))

    dt_l = jnp.exp(ldt_l)
    m1r = jnp.exp(are_l * dt_l) * jnp.cos(aim_l * dt_l)
    m1i = jnp.exp(are_l * dt_l) * jnp.sin(aim_l * dt_l)
    gs = []
    pr, pi = jnp.ones_like(m1r), jnp.zeros_like(m1r)
    for _ in range(L + 1):
        gs.append(jnp.concatenate([cr_l * pr - ci_l * pi, -(cr_l * pi + ci_l * pr)], axis=0))
        pr, pi = pr * m1r - pi * m1i, pr * m1i + pi * m1r

    same_grp = row_grp == lane_grp
    zeros_tile = jnp.zeros((LANES, LANES), BF16)

    for t in range(L):
        d_t = jnp.dot(bbar, gs[t], precision=hp, preferred_element_type=F32)
        d_t = jnp.where(same_grp, d_t, 0.0).astype(BF16)
        for j in range(L - t):
            i = j + t
            wt_scr[j * LANES:(j + 1) * LANES, i * LANES:(i + 1) * LANES] = d_t
    for j in range(L):
        for i in range(j):
            wt_scr[j * LANES:(j + 1) * LANES, i * LANES:(i + 1) * LANES] = zeros_tile

    for j in range(L):
        zr, zi = pw_g[L - 1 - j]
        s_j = cmul(bbar, rows16(zr), rows16(zi))
        for g in range(GROUPS_PER_BLOCK):
            ws_scr[j * LANES:(j + 1) * LANES, g * LANES:(g + 1) * LANES] = (
                jnp.where(row_grp == g, s_j, 0.0).astype(BF16))

    for g in range(GROUPS_PER_BLOCK):
        for i in range(L):
            wo_scr[g * LANES:(g + 1) * LANES, i * LANES:(i + 1) * LANES] = (
                jnp.where(lane_grp == g, gs[i + 1], 0.0).astype(BF16))

    tok_scr[...] = u_ref[...].astype(F32)
    nrow = u_ref.shape[0] // L
    u = jnp.concatenate([tok_scr[pl.ds(j, nrow, stride=L), :].astype(BF16) for j in range(L)], axis=1)
    y = jnp.dot(u, wt_scr[...], preferred_element_type=F32)
    st = jnp.dot(u, ws_scr[...], preferred_element_type=F32)

    kk = lax.broadcasted_iota(jnp.int32, (nrow, 1), 0) & (chunks_per_seq - 1)
    lane1 = lax.broadcasted_iota(jnp.int32, (1, LANES), 1)
    for g in range(GROUPS_PER_BLOCK):
        xs = st[:, g * LANES:(g + 1) * LANES]
        zr = pw_g[L][0][g:g + 1]
        zi = pw_g[L][1][g:g + 1]
        d = 1
        while d < chunks_per_seq:
            sh = jnp.where(kk >= d, pltpu.roll(xs, d, axis=0), 0.0)
            zmix = jnp.where(lane1 < SSM_STATE, -zi, zi)
            xs = xs + sh * zr + pltpu.roll(sh, SSM_STATE, axis=1) * zmix
            zr, zi = zr * zr - zi * zi, 2.0 * zr * zi
            d *= 2
        prev = jnp.where(kk >= 1, pltpu.roll(xs, 1, axis=0), 0.0)
        xp_scr[:, g * LANES:(g + 1) * LANES] = prev.astype(BF16)

    y = y + jnp.dot(xp_scr[...], wo_scr[...], preferred_element_type=F32)
    y = jax.nn.gelu(y + dsk_ref[...] * u.astype(F32))
    for i in range(L):
        tok_scr[pl.ds(i, nrow, stride=L), :] = y[:, i * LANES:(i + 1) * LANES]
    o_ref[...] = tok_scr[...].astype(o_ref.dtype)


def _s5(proj, p_grp, p_bt, p_lanes, dsk, chunks_per_seq):
    m = proj.shape[0]
    nrow, kk = m // SSM_CHUNK, SSM_CHUNK * LANES
    nst = GROUPS_PER_BLOCK * LANES
    return pl.pallas_call(
        functools.partial(_s5_kernel, chunks_per_seq=chunks_per_seq),
        grid=(N_SSM_BLOCKS,),
        in_specs=[pl.BlockSpec((m, LANES), lambda i: (0, OFF_SSM // LANES + i)),
                  pl.BlockSpec((None, 3, GROUPS_PER_BLOCK, LANES), lambda i: (i, 0, 0, 0)),
                  pl.BlockSpec((None, LANES, LANES), lambda i: (i, 0, 0)),
                  pl.BlockSpec((None, 5, SSM_STATE, LANES), lambda i: (i, 0, 0, 0)),
                  pl.BlockSpec((None, 1, kk), lambda i: (i, 0, 0))],
        out_specs=pl.BlockSpec((m, LANES), lambda i: (0, i)),
        out_shape=jax.ShapeDtypeStruct((m, D_MODEL), BF16),
        scratch_shapes=[pltpu.VMEM((kk, kk), BF16), pltpu.VMEM((kk, nst), BF16),
                        pltpu.VMEM((nst, kk), BF16), pltpu.VMEM((nrow, nst), BF16),
                        pltpu.VMEM((m, LANES), F32)],
        compiler_params=_cparams(("parallel",)),
        name="s5",
    )(proj, p_grp, p_bt, p_lanes, dsk)


def _glu_kernel(ys_ref, w_ref, b_ref, ya_ref, ga_ref, gb_ref, o_ref, *, tn):
    j = pl.program_id(1)
    z = jnp.dot(ys_ref[...], w_ref[...], preferred_element_type=F32) + b_ref[...]
    ys_n = ys_ref[:, pl.ds(pl.multiple_of(j * tn, tn), tn)].astype(F32)
    yb = ys_n * jax.nn.sigmoid(z)
    out = (jax.nn.sigmoid(ga_ref[...].astype(F32)) * ya_ref[...].astype(F32)
           + jax.nn.sigmoid(gb_ref[...].astype(F32)) * yb)
    o_ref[...] = out.astype(o_ref.dtype)


def _glu(ys, proj, w_glu, b_glu, ya):
    m = ys.shape[0]
    tm, tn = 512, 512
    return pl.pallas_call(
        functools.partial(_glu_kernel, tn=tn),
        grid=(m // tm, D_MODEL // tn),
        in_specs=[pl.BlockSpec((tm, D_MODEL), lambda i, j: (i, 0)),
                  pl.BlockSpec((D_MODEL, tn), lambda i, j: (0, j)),
                  pl.BlockSpec((1, tn), lambda i, j: (0, j)),
                  pl.BlockSpec((tm, tn), lambda i, j: (i, j)),
                  pl.BlockSpec((tm, tn), lambda i, j: (i, OFF_GA // tn + j)),
                  pl.BlockSpec((tm, tn), lambda i, j: (i, OFF_GB // tn + j))],
        out_specs=pl.BlockSpec((tm, tn), lambda i, j: (i, j)),
        out_shape=jax.ShapeDtypeStruct((m, D_MODEL), BF16),
        compiler_params=_cparams(("parallel", "parallel")),
        name="glu",
    )(ys, w_glu, b_glu, ya, proj, proj)


def _outproj_kernel(m_ref, w_ref, x_ref, gt_ref, o_ref):
    acc = jnp.dot(m_ref[...], w_ref[...], preferred_element_type=F32)
    o_ref[...] = x_ref[...] + gt_ref[...] * acc


def _outproj(mix, w_out, x2, ada3, seq):
    m = mix.shape[0]
    tm, tn = 512, 512
    per_b = seq // tm
    return pl.pallas_call(
        _outproj_kernel,
        grid=(m // tm, D_MODEL // tn),
        in_specs=[pl.BlockSpec((tm, D_MODEL), lambda i, j: (i, 0)),
                  pl.BlockSpec((D_MODEL, tn), lambda i, j: (0, j)),
                  pl.BlockSpec((tm, tn), lambda i, j: (i, j)),
                  pl.BlockSpec((None, 1, tn), lambda i, j: ((i // per_b) * 6 + 2, 0, j))],
        out_specs=pl.BlockSpec((tm, tn), lambda i, j: (i, j)),
        out_shape=jax.ShapeDtypeStruct((m, D_MODEL), F32),
        compiler_params=_cparams(("parallel", "parallel")),
        name="outproj",
    )(mix, w_out, x2, ada3)


def _mlp_kernel(h_ref, g_ref, sc_ref, sh_ref, gt_ref, wu_ref, wd_ref, gf_ref, o_ref, u_scr, acc_scr):
    f = pl.program_id(1)

    @pl.when(f == 0)
    def _():
        h = h_ref[...]
        ms = jnp.mean(h * h, axis=-1, keepdims=True)
        y = h * lax.rsqrt(ms + EPS) * g_ref[...]
        u_scr[...] = (y * (1.0 + sc_ref[...]) + sh_ref[...]).astype(BF16)
        acc_scr[...] = jnp.zeros_like(acc_scr)

    a = jnp.dot(u_scr[...], wu_ref[...], preferred_element_type=F32)
    a = jnp.square(jnp.maximum(a, 0.0))
    acc_scr[...] += jnp.dot(a.astype(BF16), wd_ref[...], preferred_element_type=F32)

    @pl.when(f == pl.num_programs(1) - 1)
    def _():
        h2 = h_ref[...] + gt_ref[...] * acc_scr[...]
        ms = jnp.mean(h2 * h2, axis=-1, keepdims=True)
        o_ref[...] = h2 * lax.rsqrt(ms + EPS) * gf_ref[...]


def _mlp(h1, g_mlp, ada3, w_up, w_down, g_final, seq):
    m = h1.shape[0]
    tm, tf = 512, 512
    per_b = seq // tm
    ff = w_up.shape[1]

    def ada_spec(k):
        return pl.BlockSpec((None, 1, D_MODEL), lambda i, f: ((i // per_b) * 6 + k, 0, 0))

    return pl.pallas_call(
        _mlp_kernel,
        grid=(m // tm, ff // tf),
        in_specs=[pl.BlockSpec((tm, D_MODEL), lambda i, f: (i, 0)),
                  pl.BlockSpec((1, D_MODEL), lambda i, f: (0, 0)),
                  ada_spec(4), ada_spec(3), ada_spec(5),
                  pl.BlockSpec((D_MODEL, tf), lambda i, f: (0, f)),
                  pl.BlockSpec((tf, D_MODEL), lambda i, f: (f, 0)),
                  pl.BlockSpec((1, D_MODEL), lambda i, f: (0, 0))],
        out_specs=pl.BlockSpec((tm, D_MODEL), lambda i, f: (i, 0)),
        out_shape=jax.ShapeDtypeStruct((m, D_MODEL), F32),
        scratch_shapes=[pltpu.VMEM((tm, D_MODEL), BF16), pltpu.VMEM((tm, D_MODEL), F32)],
        compiler_params=_cparams(("parallel", "arbitrary")),
        name="mlp",
    )(h1, g_mlp, ada3, ada3, ada3, w_up, w_down, g_final)


def _split_w_in(w):
    wg = w[:, RAW_GATE:RAW_SSM].reshape(D_MODEL, N_KV_GROUPS, 3 * GROUP_SIZE)
    wg = jnp.pad(wg, ((0, 0), (0, 0), (0, LANES - 3 * GROUP_SIZE))).reshape(D_MODEL, N_KV_GROUPS * LANES)
    w_vt = jnp.concatenate([w[:, RAW_VS:RAW_KW], w[:, RAW_VW:RAW_GATE]], axis=1).T
    return (w[:, :RAW_GATE].astype(BF16), wg.astype(BF16), w[:, RAW_SSM:].astype(BF16), w_vt.astype(BF16))


def _s5_params(a_re, a_im, log_dt, b_re, b_im, c_re, c_im, d_skip):
    nb, gp, p, cg = N_SSM_BLOCKS, GROUPS_PER_BLOCK, SSM_STATE, SSM_GROUP
    ldt = jnp.broadcast_to(log_dt[:, None], (SSM_NGROUPS, p))

    def grp(a):
        return jnp.concatenate([a, a], axis=-1).reshape(nb, gp, LANES)

    def lanes(a):
        a = jnp.broadcast_to(a.reshape(nb, gp, p, 1).transpose(0, 2, 1, 3), (nb, p, gp, cg))
        return a.reshape(nb, p, LANES)

    def ct(cm):
        return cm.reshape(nb, gp, cg, p).transpose(0, 3, 1, 2).reshape(nb, p, LANES)

    p_grp = jnp.stack([grp(a_re), grp(a_im), grp(ldt)], axis=1)
    p_bt = jnp.stack([b_re, b_im], axis=1).transpose(0, 3, 1, 2).reshape(nb, LANES, LANES)
    p_lanes = jnp.stack([lanes(a_re), lanes(a_im), lanes(ldt), ct(c_re), ct(c_im)], axis=1)
    dsk = jnp.tile(d_skip.reshape(nb, 1, LANES), (1, 1, SSM_CHUNK))
    return p_grp, p_bt, p_lanes, dsk


def kernel(x, c, w_ada, b_ada, g_mix, w_in, w_ck1, w_ck2, pe_ck, w_cv1, w_cv2, pe_cv,
           a_re, a_im, log_dt, b_re, b_im, c_re, c_im, d_skip, w_glu, b_glu,
           w_out, g_mlp, w_up, w_down, g_final):
    bsz, seq, _ = x.shape
    assert w_ada.shape[0] == 1, "single-layer block"
    assert seq % 1024 == 0 and (seq & (seq - 1)) == 0
    m = bsz * seq
    x2 = x.reshape(m, D_MODEL)

    c_pad = jnp.pad(c, ((0, 8 - bsz), (0, 0)))
    ada = _ada(c_pad, w_ada[0], b_ada)
    ada3 = ada[:bsz].reshape(bsz * 6, 1, D_MODEL)

    proj, vt = _inproj(x2, g_mix, ada3, *_split_w_in(w_in[0]), seq)

    w1s = jnp.stack([w_ck1[0], w_cv1[0]]).reshape(2, CMP_LEN * HEAD_DIM, HEAD_DIM).astype(BF16)
    pes = jnp.stack([pe_ck[0], pe_cv[0]]).reshape(2, 1, CMP_LEN * HEAD_DIM)
    pes = jnp.pad(pes, ((0, 0), (0, 7), (0, 0))).astype(BF16)
    w2s = jnp.stack([w_ck2[0], w_cv2[0]]).astype(BF16)
    kvc, kvct = _compress(proj, w1s, pes, w2s, bsz, seq)

    e_tab = (jnp.arange(seq)[:, None] // SEL_BLOCK == jnp.arange(LANES)[None, :]).astype(BF16)
    y_a = _nsa(proj, kvc, kvct, vt, e_tab, bsz, seq)

    p_grp, p_bt, p_lanes, dsk = _s5_params(a_re[0], a_im[0], log_dt[0], b_re[0], b_im[0],
                                           c_re[0], c_im[0], d_skip[0])
    ys = _s5(proj, p_grp, p_bt, p_lanes, dsk, seq // SSM_CHUNK)

    mix = _glu(ys, proj, w_glu[0].astype(BF16), b_glu, y_a)
    h1 = _outproj(mix, w_out[0].astype(BF16), x2, ada3, seq)
    out = _mlp(h1, g_mlp, ada3, w_up[0].astype(BF16), w_down[0].astype(BF16),
               g_final.reshape(1, D_MODEL), seq)
    return out.reshape(bsz, seq, D_MODEL)
```

```python
import functools

import jax
import jax.numpy as jnp
from jax import lax
from jax.experimental import pallas as pl
from jax.experimental.pallas import tpu as pltpu

F32 = jnp.float32
BF16 = jnp.bfloat16

D_MODEL = 2048
N_HEADS = 16
HEAD_DIM = 128
N_KV_GROUPS = 4
GROUP_SIZE = 4
KV_DIM = N_KV_GROUPS * HEAD_DIM
CMP_LEN = 32
CMP_STRIDE = 16
SEL_BLOCK = 64
SEL_TOPN = 16
WINDOW = 512
SSM_GROUP = 16
SSM_NGROUPS = D_MODEL // SSM_GROUP
SSM_STATE = 64
EPS = 1e-6
NEG = -1e30
BIG = 1e6
LOG2E = 1.4426950408889634
ONES_ROWS = 16

LANES = 128
SSM_CHUNK = 16
GROUPS_PER_BLOCK = LANES // SSM_GROUP
N_SSM_BLOCKS = SSM_NGROUPS // GROUPS_PER_BLOCK

OFF_Q = 0
OFF_SSM = OFF_Q + D_MODEL
OFF_GA = OFF_SSM + D_MODEL
OFF_GB = OFF_GA + D_MODEL
OFF_KC = OFF_GB + D_MODEL
OFF_VC = OFF_KC + KV_DIM
OFF_KS = OFF_VC + KV_DIM
OFF_KW = OFF_KS + KV_DIM
OFF_GATE = OFF_KW + KV_DIM
PROJ_PAD = OFF_GATE + N_KV_GROUPS * LANES
RAW_VS = D_MODEL + 3 * KV_DIM
RAW_KW = RAW_VS + KV_DIM
RAW_VW = RAW_KW + KV_DIM
RAW_GATE = RAW_VW + KV_DIM
RAW_SSM = RAW_GATE + 3 * N_HEADS

VMEM_LIMIT = 56 * 1024 * 1024


def _cparams(sem):
    return pltpu.CompilerParams(dimension_semantics=sem, vmem_limit_bytes=VMEM_LIMIT)


def _t(x):
    r, c = x.shape
    rows = []
    for j in range(c // LANES):
        rows.append(jnp.concatenate([x[i * LANES:(i + 1) * LANES, j * LANES:(j + 1) * LANES].T
                                     for i in range(r // LANES)], axis=1))
    return jnp.concatenate(rows, axis=0)


def _nt_dot(a, b):
    return lax.dot_general(a, b, (((1,), (1,)), ((), ())), preferred_element_type=F32)


def _ada_kernel(c_ref, w_ref, b_ref, o_ref):
    c = c_ref[...]
    cond = c * jax.nn.sigmoid(c)
    o_ref[...] = jnp.dot(cond.astype(BF16), w_ref[...].astype(BF16),
                         preferred_element_type=F32) + b_ref[...]


def _ada(c_pad, w_ada, b_ada):
    n = w_ada.shape[1]
    tn = 1024
    return pl.pallas_call(
        _ada_kernel,
        grid=(n // tn,),
        in_specs=[pl.BlockSpec((8, D_MODEL), lambda j: (0, 0)),
                  pl.BlockSpec((D_MODEL, tn), lambda j: (0, j)),
                  pl.BlockSpec((1, tn), lambda j: (0, j))],
        out_specs=pl.BlockSpec((8, tn), lambda j: (0, j)),
        out_shape=jax.ShapeDtypeStruct((8, n), F32),
        compiler_params=_cparams(("parallel",)),
        name="ada",
    )(c_pad, w_ada, b_ada)


INPROJ_TN = 512
_Q_TILES = D_MODEL // INPROJ_TN
_TAIL_TILES = 3 * D_MODEL // INPROJ_TN
_KV_HEAD_BLOCKS = (4, 5, 6, 8)
_GATE_TILE = _Q_TILES + _TAIL_TILES + len(_KV_HEAD_BLOCKS)


def _head_block(j):
    k = j - (_Q_TILES + _TAIL_TILES)
    kv = jnp.where(k < 3, 4 + k, 8)
    return jnp.where(j < _Q_TILES, j, jnp.where(k < 0, _Q_TILES - 1, kv))


def _inproj_kernel(x_ref, g_ref, sc_ref, sh_ref, wh_ref, wg_ref, wt_ref, wvt_ref, o_ref, vt_ref, u_scr):
    j = pl.program_id(1)

    @pl.when(j == 0)
    def _():
        x = x_ref[...]
        ms = jnp.mean(x * x, axis=-1, keepdims=True)
        y = x * lax.rsqrt(ms + EPS) * g_ref[...]
        u_scr[...] = (y * (1.0 + sc_ref[...]) + sh_ref[...]).astype(BF16)
        vt_ref[...] = _nt_dot(wvt_ref[...], u_scr[...]).astype(vt_ref.dtype)

    def emit(w):
        o_ref[...] = jnp.dot(u_scr[...], w, preferred_element_type=F32).astype(o_ref.dtype)

    in_tail = (j >= _Q_TILES) & (j < _Q_TILES + _TAIL_TILES)
    pl.when(in_tail)(lambda: emit(wt_ref[...]))
    pl.when(j == _GATE_TILE)(lambda: emit(wg_ref[...]))
    pl.when(jnp.logical_not(in_tail) & (j != _GATE_TILE))(lambda: emit(wh_ref[...].astype(BF16)))


def _inproj(x2, g_mix, ada3, w_in_f32, w_gate, w_tail, w_vt, seq):
    m = x2.shape[0]
    tm, tn = 1024, INPROJ_TN
    per_b = seq // tm
    const = dict(pipeline_mode=pl.Buffered(1))
    return pl.pallas_call(
        _inproj_kernel,
        grid=(m // tm, PROJ_PAD // tn),
        in_specs=[pl.BlockSpec((tm, D_MODEL), lambda i, j: (i, 0)),
                  pl.BlockSpec((1, D_MODEL), lambda i, j: (0, 0)),
                  pl.BlockSpec((None, 1, D_MODEL), lambda i, j: ((i // per_b) * 6 + 1, 0, 0)),
                  pl.BlockSpec((None, 1, D_MODEL), lambda i, j: ((i // per_b) * 6 + 0, 0, 0)),
                  pl.BlockSpec((D_MODEL, tn), lambda i, j: (0, _head_block(j))),
                  pl.BlockSpec((D_MODEL, tn), lambda i, j: (0, 0), **const),
                  pl.BlockSpec((D_MODEL, tn), lambda i, j: (0, jnp.clip(j - _Q_TILES, 0, _TAIL_TILES - 1))),
                  pl.BlockSpec((2 * KV_DIM, D_MODEL), lambda i, j: (0, 0), **const)],
        out_specs=[pl.BlockSpec((tm, tn), lambda i, j: (i, j)),
                   pl.BlockSpec((2 * KV_DIM, tm), lambda i, j: (0, i))],
        out_shape=[jax.ShapeDtypeStruct((m, PROJ_PAD), BF16),
                   jax.ShapeDtypeStruct((2 * KV_DIM, m), BF16)],
        scratch_shapes=[pltpu.VMEM((tm, D_MODEL), BF16)],
        compiler_params=_cparams(("parallel", "arbitrary")),
        name="inproj",
    )(x2, g_mix, ada3, ada3, w_in_f32, w_gate, w_tail, w_vt)


def _compress_kernel(x_ref, w1_ref, pe_ref, w2_ref, o_ref, ot_ref, xf_scr):
    xf_scr[...] = x_ref[...].astype(F32)
    nc = x_ref.shape[0] // CMP_STRIDE
    x = jnp.concatenate([xf_scr[pl.ds(l, nc, stride=CMP_STRIDE), :].astype(BF16)
                         for l in range(CMP_STRIDE)], axis=1)
    half = CMP_STRIDE * HEAD_DIM
    w1 = w1_ref[...]
    ha = jnp.dot(x, w1[:half], preferred_element_type=F32)
    hb = jnp.dot(x, w1[half:], preferred_element_type=F32)
    hb = pltpu.roll(hb, nc - 1, axis=0)
    pe_t = jnp.dot(pe_ref[...], w1, preferred_element_type=F32)[0:1]
    hid = jax.nn.gelu(ha + hb + pe_t)
    out = jnp.dot(hid.astype(BF16), w2_ref[...], preferred_element_type=F32)
    row = lax.broadcasted_iota(jnp.int32, out.shape, 0)
    out = jnp.where(row < nc - 1, out, 0.0)
    o_ref[...] = out.astype(o_ref.dtype)
    ot_ref[...] = _t(out).astype(ot_ref.dtype)


def _compress(proj, w1s, pes, w2s, bsz, seq):
    nc = seq // CMP_STRIDE
    g = N_KV_GROUPS
    return pl.pallas_call(
        _compress_kernel,
        grid=(2, bsz, g),
        in_specs=[pl.BlockSpec((seq, HEAD_DIM), lambda a, i, j: (i, OFF_KC // HEAD_DIM + a * g + j)),
                  pl.BlockSpec((None, CMP_LEN * HEAD_DIM, HEAD_DIM), lambda a, i, j: (a, 0, 0)),
                  pl.BlockSpec((None, 8, CMP_LEN * HEAD_DIM), lambda a, i, j: (a, 0, 0)),
                  pl.BlockSpec((None, HEAD_DIM, HEAD_DIM), lambda a, i, j: (a, 0, 0))],
        out_specs=[pl.BlockSpec((None, None, None, nc, HEAD_DIM), lambda a, i, j: (a, i, j, 0, 0)),
                   pl.BlockSpec((None, None, None, HEAD_DIM, nc), lambda a, i, j: (a, i, j, 0, 0))],
        out_shape=[jax.ShapeDtypeStruct((2, bsz, g, nc, HEAD_DIM), BF16),
                   jax.ShapeDtypeStruct((2, bsz, g, HEAD_DIM, nc), BF16)],
        scratch_shapes=[pltpu.VMEM((seq, HEAD_DIM), F32)],
        compiler_params=_cparams(("parallel", "parallel", "parallel")),
        name="compress",
    )(proj, w1s, pes, w2s)


def _nsa_kernel(q_ref, gate_ref, kc_ref, vct_ref, ks_ref, e_ref, vst_ref, kw_ref, vwt_ref, o_ref,
                sa_scr, sb_scr, m_scr, acc_scr, *, tq, tk, seq):
    s0 = pl.program_id(2) * tq
    rows = GROUP_SIZE * tq
    nsel = seq // SEL_BLOCK
    topn = min(SEL_TOPN, nsel)
    qscale = HEAD_DIM ** -0.5 * LOG2E

    q = q_ref[...]
    q4 = jnp.concatenate([q[:, r * HEAD_DIM:(r + 1) * HEAD_DIM] for r in range(GROUP_SIZE)], axis=0)
    q4 = (q4.astype(F32) * qscale).astype(BF16)
    t_lane = s0 + (lax.broadcasted_iota(jnp.int32, (1, rows), 1) & (tq - 1))

    kc = kc_ref[...]
    nc = kc.shape[0]
    sc = _nt_dot(kc, q4)
    n_col = lax.broadcasted_iota(jnp.int32, (nc, 1), 0)
    valid_c = (n_col * CMP_STRIDE + (CMP_LEN - 1)) <= t_lane
    sc = jnp.where(valid_c, sc, NEG)
    mc = jnp.max(sc, axis=0, keepdims=True)
    pc = jnp.where(valid_c, jnp.exp2(sc - mc), 0.0)
    pc = pc * (1.0 / jnp.maximum(jnp.sum(pc, axis=0, keepdims=True), 1e-30))
    o_c = jnp.dot(vct_ref[...], pc.astype(BF16), preferred_element_type=F32)
    psum = pc[:, 0:tq] + pc[:, tq:2 * tq] + pc[:, 2 * tq:3 * tq] + pc[:, 3 * tq:4 * tq]

    jj = lax.broadcasted_iota(jnp.int32, (nsel, nc), 0)
    nn = lax.broadcasted_iota(jnp.int32, (nsel, nc), 1)
    ov = ((nn * CMP_STRIDE < jj * SEL_BLOCK + SEL_BLOCK)
          & (nn * CMP_STRIDE + CMP_LEN > jj * SEL_BLOCK)
          & (nn < nc - 1))
    ov = jnp.where(ov, 1.0, 0.0).astype(BF16)
    p_hi = psum.astype(BF16)
    p_lo = (psum - p_hi.astype(F32)).astype(BF16)
    imp_t = (jnp.dot(ov, p_hi, preferred_element_type=F32)
             + jnp.dot(ov, p_lo, preferred_element_type=F32))

    jcol = lax.broadcasted_iota(jnp.int32, (nsel, tq), 0)
    cur = lax.shift_right_logical(s0 + lax.broadcasted_iota(jnp.int32, (nsel, tq), 1), 6)
    valid = jcol <= cur
    forced = (jcol == 0) | (jcol == cur) | (jcol == cur - 1)
    score = jnp.where(forced, BIG, jnp.where(valid, imp_t, -BIG))

    def rank_of(sc_):
        slabs = []
        for sb in range(nsel // 8):
            sj = sc_[sb * 8:(sb + 1) * 8]
            jr = jcol[sb * 8:(sb + 1) * 8, :LANES]
            rank = jnp.zeros((8, LANES), F32)
            for i in range(nsel):
                si = sc_[i:i + 1]
                if i < sb * 8:
                    cond = si >= sj
                elif i >= sb * 8 + 8:
                    cond = si > sj
                else:
                    cond = (si > sj) | ((jr > i) & (si == sj))
                rank = rank + jnp.where(cond, 1.0, 0.0)
            slabs.append(rank)
        return jnp.concatenate(slabs, axis=0)

    rank = jnp.concatenate([rank_of(score[:, c * LANES:(c + 1) * LANES]) for c in range(tq // LANES)], axis=1)
    bias_t = jnp.where((rank < topn) & valid, 0.0, NEG)
    bias_t = jnp.concatenate([bias_t, jnp.zeros((LANES - nsel, tq), F32)], axis=0)
    bias_q = _t(bias_t).astype(BF16)
    q_aug = jnp.concatenate([q4, jnp.concatenate([bias_q] * GROUP_SIZE, axis=0)], axis=1)

    def chunk_off(kb):
        return pl.multiple_of(kb * tk, tk)

    ones_rows = jnp.ones((ONES_ROWS, tk), BF16)

    def reset_stats():
        m_scr[...] = jnp.full((1, rows), NEG, F32)
        acc_scr[...] = jnp.zeros((HEAD_DIM + ONES_ROWS, rows), F32)

    def update(s, kb, vt_ref, mask):
        off = chunk_off(kb)
        if mask is not None:
            kpos = off + lax.broadcasted_iota(jnp.int32, (tk, 1), 0)
            s = jnp.where(mask(kpos), s, NEG)
        m = m_scr[...]
        m_new = jnp.maximum(m, jnp.max(s, axis=0, keepdims=True))
        p = jnp.exp2(s - m_new).astype(BF16)
        vt1 = jnp.concatenate([vt_ref[:, pl.ds(off, tk)], ones_rows], axis=0)
        acc_scr[...] = jnp.exp2(m - m_new) * acc_scr[...] + jnp.dot(vt1, p, preferred_element_type=F32)
        m_scr[...] = m_new

    def result():
        acc = acc_scr[...]
        return acc[:HEAD_DIM] * (1.0 / jnp.maximum(acc[HEAD_DIM:HEAD_DIM + 1], 1e-30))

    hi = (s0 + tq - 1) // tk + 1
    causal = lambda kpos: kpos <= t_lane

    k_lo = hi - 2
    w_lo = jnp.where(k_lo >= 0, t_lane - (WINDOW - 1), seq)
    k_lo = jnp.maximum(k_lo, 0)
    reset_stats()
    update(_nt_dot(kw_ref[pl.ds(chunk_off(k_lo), tk), :], q4), k_lo, vwt_ref, lambda kpos: kpos >= w_lo)
    update(_nt_dot(kw_ref[pl.ds(chunk_off(hi - 1), tk), :], q4), hi - 1, vwt_ref, causal)
    o_w = result()

    def sel_scores_into(dst, kb):
        off = chunk_off(kb)
        keys = jnp.concatenate([ks_ref[pl.ds(off, tk), :], e_ref[pl.ds(off, tk), :]], axis=1)
        dst[...] = _nt_dot(keys, q_aug)

    reset_stats()
    sel_scores_into(sa_scr, 0)
    n_pairs = lax.shift_right_logical(hi - 1, 1)

    def pair(i, c):
        k0 = 2 * i
        sel_scores_into(sb_scr, k0 + 1)
        update(sa_scr[...], k0, vst_ref, None)
        sel_scores_into(sa_scr, k0 + 2)
        update(sb_scr[...], k0 + 1, vst_ref, None)
        return c

    lax.fori_loop(0, n_pairs, pair, 0)

    @pl.when(((hi - 1) & 1) == 1)
    def _():
        sel_scores_into(sb_scr, hi - 1)
        update(sa_scr[...], hi - 2, vst_ref, None)
        update(sb_scr[...], hi - 1, vst_ref, causal)

    @pl.when(((hi - 1) & 1) == 0)
    def _():
        update(sa_scr[...], hi - 1, vst_ref, causal)

    o_s = result()

    sg = _t(jax.nn.sigmoid(gate_ref[...].astype(F32)))
    outs = []
    for r in range(GROUP_SIZE):
        sl = slice(r * tq, (r + 1) * tq)
        o_r = (sg[3 * r:3 * r + 1] * o_c[:, sl] + sg[3 * r + 1:3 * r + 2] * o_s[:, sl]
               + sg[3 * r + 2:3 * r + 3] * o_w[:, sl])
        outs.append(_t(o_r))
    o_ref[...] = jnp.concatenate(outs, axis=1).astype(o_ref.dtype)


def _nsa(proj, kvc, kvct, vt, e_tab, bsz, seq):
    tq, tk = 256, 512
    assert tk == WINDOW and tk % tq == 0
    nq = seq // tq
    gw = GROUP_SIZE * HEAD_DIM
    nc = kvc.shape[3]
    g4 = N_KV_GROUPS

    def k_spec(off):
        cb = off // HEAD_DIM
        return pl.BlockSpec((seq, HEAD_DIM), lambda b, g, i: (b, cb + g))

    return pl.pallas_call(
        functools.partial(_nsa_kernel, tq=tq, tk=tk, seq=seq),
        grid=(bsz, g4, nq),
        in_specs=[pl.BlockSpec((tq, gw), lambda b, g, i: (b * nq + i, g)),
                  pl.BlockSpec((tq, LANES), lambda b, g, i: (b * nq + i, OFF_GATE // LANES + g)),
                  pl.BlockSpec((None, None, None, nc, HEAD_DIM), lambda b, g, i: (0, b, g, 0, 0)),
                  pl.BlockSpec((None, None, None, HEAD_DIM, nc), lambda b, g, i: (1, b, g, 0, 0)),
                  k_spec(OFF_KS),
                  pl.BlockSpec((seq, LANES), lambda b, g, i: (0, 0)),
                  pl.BlockSpec((HEAD_DIM, seq), lambda b, g, i: (g, b)),
                  k_spec(OFF_KW),
                  pl.BlockSpec((HEAD_DIM, seq), lambda b, g, i: (g4 + g, b))],
        out_specs=pl.BlockSpec((tq, gw), lambda b, g, i: (b * nq + i, g)),
        out_shape=jax.ShapeDtypeStruct((bsz * seq, D_MODEL), BF16),
        scratch_shapes=[pltpu.VMEM((tk, GROUP_SIZE * tq), F32), pltpu.VMEM((tk, GROUP_SIZE * tq), F32),
                        pltpu.VMEM((1, GROUP_SIZE * tq), F32),
                        pltpu.VMEM((HEAD_DIM + ONES_ROWS, GROUP_SIZE * tq), F32)],
        compiler_params=_cparams(("parallel", "parallel", "arbitrary")),
        name="nsa",
    )(proj, proj, kvc, kvct, proj, e_tab, vt, proj, vt)


def _s5_kernel(u_ref, pg_ref, bt_ref, pl_ref, dsk_ref, o_ref, wt_scr, ws_scr, wo_scr, xp_scr, tok_scr,
               *, chunks_per_seq):
    L = SSM_CHUNK
    hp = lax.Precision.HIGHEST
    are_g, aim_g, ldt_g = pg_ref[0], pg_ref[1], pg_ref[2]
    bt = bt_ref[...]
    are_l, aim_l, ldt_l, cr_l, ci_l = pl_ref[0], pl_ref[1], pl_ref[2], pl_ref[3], pl_ref[4]

    lane = lax.broadcasted_iota(jnp.int32, (LANES, LANES), 1)
    rowi = lax.broadcasted_iota(jnp.int32, (LANES, LANES), 0)
    is_re = lane < SSM_STATE
    row_grp = rowi // SSM_GROUP
    lane_grp = lane // SSM_GROUP

    def rows16(z):
        return jnp.concatenate([jnp.broadcast_to(z[g:g + 1], (SSM_GROUP, LANES))
                                for g in range(GROUPS_PER_BLOCK)], axis=0)

    def cmul(x, zr, zi):
        return x * zr + pltpu.roll(x, SSM_STATE, axis=1) * jnp.where(is_re, -zi, zi)

    dt_g = jnp.exp(ldt_g)
    l1r = jnp.exp(are_g * dt_g) * jnp.cos(aim_g * dt_g)
    l1i = jnp.exp(are_g * dt_g) * jnp.sin(aim_g * dt_g)
    den = are_g * are_g + aim_g * aim_g
    nr = l1r - 1.0
    coef_r = (nr * are_g + l1i * aim_g) / den
    coef_i = (l1i * are_g - nr * aim_g) / den
    bbar = cmul(bt, rows16(coef_r), rows16(coef_i))

    pw_g = [(jnp.ones_like(l1r), jnp.zeros_like(l1r))]
    for _ in range(L):
        pr, pi = pw_g[-1]
        pw_g.append((pr * l1r - pi * l1i, pr * l1i + pi * l1r))

    dt_l = jnp.exp(ldt_l)
    m1r = jnp.exp(are_l * dt_l) * jnp.cos(aim_l * dt_l)
    m1i = jnp.exp(are_l * dt_l) * jnp.sin(aim_l * dt_l)
    gs = []
    pr, pi = jnp.ones_like(m1r), jnp.zeros_like(m1r)
    for _ in range(L + 1):
        gs.append(jnp.concatenate([cr_l * pr - ci_l * pi, -(cr_l * pi + ci_l * pr)], axis=0))
        pr, pi = pr * m1r - pi * m1i, pr * m1i + pi * m1r

    same_grp = row_grp == lane_grp
    zeros_tile = jnp.zeros((LANES, LANES), BF16)

    for t in range(L):
        d_t = jnp.dot(bbar, gs[t], precision=hp, preferred_element_type=F32)
        d_t = jnp.where(same_grp, d_t, 0.0).astype(BF16)
        for j in range(L - t):
            i = j + t
            wt_scr[j * LANES:(j + 1) * LANES, i * LANES:(i + 1) * LANES] = d_t
    for j in range(L):
        for i in range(j):
            wt_scr[j * LANES:(j + 1) * LANES, i * LANES:(i + 1) * LANES] = zeros_tile

    for j in range(L):
        zr, zi = pw_g[L - 1 - j]
        s_j = cmul(bbar, rows16(zr), rows16(zi))
        for g in range(GROUPS_PER_BLOCK):
            ws_scr[j * LANES:(j + 1) * LANES, g * LANES:(g + 1) * LANES] = (
                jnp.where(row_grp == g, s_j, 0.0).astype(BF16))

    for g in range(GROUPS_PER_BLOCK):
        for i in range(L):
            wo_scr[g * LANES:(g + 1) * LANES, i * LANES:(i + 1) * LANES] = (
                jnp.where(lane_grp == g, gs[i + 1], 0.0).astype(BF16))

    tok_scr[...] = u_ref[...].astype(F32)
    nrow = u_ref.shape[0] // L
    u = jnp.concatenate([tok_scr[pl.ds(j, nrow, stride=L), :].astype(BF16) for j in range(L)], axis=1)
    y = jnp.dot(u, wt_scr[...], preferred_element_type=F32)
    st = jnp.dot(u, ws_scr[...], preferred_element_type=F32)

    kk = lax.broadcasted_iota(jnp.int32, (nrow, 1), 0) & (chunks_per_seq - 1)
    lane1 = lax.broadcasted_iota(jnp.int32, (1, LANES), 1)
    for g in range(GROUPS_PER_BLOCK):
        xs = st[:, g * LANES:(g + 1) * LANES]
        zr = pw_g[L][0][g:g + 1]
        zi = pw_g[L][1][g:g + 1]
        d = 1
        while d < chunks_per_seq:
            sh = jnp.where(kk >= d, pltpu.roll(xs, d, axis=0), 0.0)
            zmix = jnp.where(lane1 < SSM_STATE, -zi, zi)
            xs = xs + sh * zr + pltpu.roll(sh, SSM_STATE, axis=1) * zmix
            zr, zi = zr * zr - zi * zi, 2.0 * zr * zi
            d *= 2
        prev = jnp.where(kk >= 1, pltpu.roll(xs, 1, axis=0), 0.0)
        xp_scr[:, g * LANES:(g + 1) * LANES] = prev.astype(BF16)

    y = y + jnp.dot(xp_scr[...], wo_scr[...], preferred_element_type=F32)
    y = jax.nn.gelu(y + dsk_ref[...] * u.astype(F32))
    for i in range(L):
        tok_scr[pl.ds(i, nrow, stride=L), :] = y[:, i * LANES:(i + 1) * LANES]
    o_ref[...] = tok_scr[...].astype(o_ref.dtype)


def _s5(proj, p_grp, p_bt, p_lanes, dsk, chunks_per_seq):
    m = proj.shape[0]
    nrow, kk = m // SSM_CHUNK, SSM_CHUNK * LANES
    nst = GROUPS_PER_BLOCK * LANES
    return pl.pallas_call(
        functools.partial(_s5_kernel, chunks_per_seq=chunks_per_seq),
        grid=(N_SSM_BLOCKS,),
        in_specs=[pl.BlockSpec((m, LANES), lambda i: (0, OFF_SSM // LANES + i)),
                  pl.BlockSpec((None, 3, GROUPS_PER_BLOCK, LANES), lambda i: (i, 0, 0, 0)),
                  pl.BlockSpec((None, LANES, LANES), lambda i: (i, 0, 0)),
                  pl.BlockSpec((None, 5, SSM_STATE, LANES), lambda i: (i, 0, 0, 0)),
                  pl.BlockSpec((None, 1, kk), lambda i: (i, 0, 0))],
        out_specs=pl.BlockSpec((m, LANES), lambda i: (0, i)),
        out_shape=jax.ShapeDtypeStruct((m, D_MODEL), BF16),
        scratch_shapes=[pltpu.VMEM((kk, kk), BF16), pltpu.VMEM((kk, nst), BF16),
                        pltpu.VMEM((nst, kk), BF16), pltpu.VMEM((nrow, nst), BF16),
                        pltpu.VMEM((m, LANES), F32)],
        compiler_params=_cparams(("parallel",)),
        name="s5",
    )(proj, p_grp, p_bt, p_lanes, dsk)


def _mix_kernel(ys_ref, ya_ref, ga_ref, gb_ref, x_ref, wg_ref, bg_ref, wo_ref, gt_ref, gm_ref, sc_ref, sh_ref,
                h_ref, u_ref):
    ys = ys_ref[...]
    z = jnp.dot(ys, wg_ref[...], preferred_element_type=F32) + bg_ref[...]
    yb = ys.astype(F32) * jax.nn.sigmoid(z)
    mix = (jax.nn.sigmoid(ga_ref[...].astype(F32)) * ya_ref[...].astype(F32)
           + jax.nn.sigmoid(gb_ref[...].astype(F32)) * yb)
    h = x_ref[...] + gt_ref[...] * jnp.dot(mix.astype(BF16), wo_ref[...], preferred_element_type=F32)
    h_ref[...] = h
    ms = jnp.mean(h * h, axis=-1, keepdims=True)
    y = h * lax.rsqrt(ms + EPS) * gm_ref[...]
    u_ref[...] = (y * (1.0 + sc_ref[...]) + sh_ref[...]).astype(u_ref.dtype)


def _mix(ys, ya, proj, x2, w_glu, b_glu, w_out, g_mlp, ada3, seq):
    m = ys.shape[0]
    tm = 256
    per_b = seq // tm
    const = dict(pipeline_mode=pl.Buffered(1))
    row = lambda cb: pl.BlockSpec((tm, D_MODEL), lambda i: (i, cb))

    def ada_spec(k):
        return pl.BlockSpec((None, 1, D_MODEL), lambda i: ((i // per_b) * 6 + k, 0, 0))

    return pl.pallas_call(
        _mix_kernel,
        grid=(m // tm,),
        in_specs=[row(0), row(0), row(OFF_GA // D_MODEL), row(OFF_GB // D_MODEL), row(0),
                  pl.BlockSpec((D_MODEL, D_MODEL), lambda i: (0, 0), **const),
                  pl.BlockSpec((1, D_MODEL), lambda i: (0, 0)),
                  pl.BlockSpec((D_MODEL, D_MODEL), lambda i: (0, 0), **const),
                  ada_spec(2),
                  pl.BlockSpec((1, D_MODEL), lambda i: (0, 0)),
                  ada_spec(4), ada_spec(3)],
        out_specs=[row(0), row(0)],
        out_shape=[jax.ShapeDtypeStruct((m, D_MODEL), F32), jax.ShapeDtypeStruct((m, D_MODEL), BF16)],
        compiler_params=_cparams(("parallel",)),
        name="mix",
    )(ys, ya, proj, proj, x2, w_glu, b_glu, w_out, ada3, g_mlp, ada3, ada3)


def _mlp_kernel(h_ref, u_ref, gt_ref, wu_ref, wd_ref, gf_ref, o_ref, acc_scr):
    f = pl.program_id(1)

    @pl.when(f == 0)
    def _():
        acc_scr[...] = jnp.zeros_like(acc_scr)

    a = jnp.dot(u_ref[...], wu_ref[...], preferred_element_type=F32)
    a = jnp.square(jnp.maximum(a, 0.0))
    acc_scr[...] += jnp.dot(a.astype(BF16), wd_ref[...], preferred_element_type=F32)

    @pl.when(f == pl.num_programs(1) - 1)
    def _():
        h2 = h_ref[...] + gt_ref[...] * acc_scr[...]
        ms = jnp.mean(h2 * h2, axis=-1, keepdims=True)
        o_ref[...] = h2 * lax.rsqrt(ms + EPS) * gf_ref[...]


def _mlp(h1, u2, ada3, w_up, w_down, g_final, seq):
    m = h1.shape[0]
    tm, tf = 512, 512
    per_b = seq // tm
    ff = w_up.shape[1]
    return pl.pallas_call(
        _mlp_kernel,
        grid=(m // tm, ff // tf),
        in_specs=[pl.BlockSpec((tm, D_MODEL), lambda i, f: (i, 0)),
                  pl.BlockSpec((tm, D_MODEL), lambda i, f: (i, 0)),
                  pl.BlockSpec((None, 1, D_MODEL), lambda i, f: ((i // per_b) * 6 + 5, 0, 0)),
                  pl.BlockSpec((D_MODEL, tf), lambda i, f: (0, f)),
                  pl.BlockSpec((tf, D_MODEL), lambda i, f: (f, 0)),
                  pl.BlockSpec((1, D_MODEL), lambda i, f: (0, 0))],
        out_specs=pl.BlockSpec((tm, D_MODEL), lambda i, f: (i, 0)),
        out_shape=jax.ShapeDtypeStruct((m, D_MODEL), F32),
        scratch_shapes=[pltpu.VMEM((tm, D_MODEL), F32)],
        compiler_params=_cparams(("parallel", "arbitrary")),
        name="mlp",
    )(h1, u2, ada3, w_up, w_down, g_final)


def _split_w_in(w):
    wg = w[:, RAW_GATE:RAW_SSM].reshape(D_MODEL, N_KV_GROUPS, 3 * GROUP_SIZE)
    wg = jnp.pad(wg, ((0, 0), (0, 0), (0, LANES - 3 * GROUP_SIZE))).reshape(D_MODEL, N_KV_GROUPS * LANES)
    w_vt = jnp.concatenate([w[:, RAW_VS:RAW_KW], w[:, RAW_VW:RAW_GATE]], axis=1).T
    return wg.astype(BF16), w[:, RAW_SSM:].astype(BF16), w_vt.astype(BF16)


def _s5_params(a_re, a_im, log_dt, b_re, b_im, c_re, c_im, d_skip):
    nb, gp, p, cg = N_SSM_BLOCKS, GROUPS_PER_BLOCK, SSM_STATE, SSM_GROUP
    ldt = jnp.broadcast_to(log_dt[:, None], (SSM_NGROUPS, p))

    def grp(a):
        return jnp.concatenate([a, a], axis=-1).reshape(nb, gp, LANES)

    def lanes(a):
        a = jnp.broadcast_to(a.reshape(nb, gp, p, 1).transpose(0, 2, 1, 3), (nb, p, gp, cg))
        return a.reshape(nb, p, LANES)

    def ct(cm):
        return cm.reshape(nb, gp, cg, p).transpose(0, 3, 1, 2).reshape(nb, p, LANES)

    p_grp = jnp.stack([grp(a_re), grp(a_im), grp(ldt)], axis=1)
    p_bt = jnp.stack([b_re, b_im], axis=1).transpose(0, 3, 1, 2).reshape(nb, LANES, LANES)
    p_lanes = jnp.stack([lanes(a_re), lanes(a_im), lanes(ldt), ct(c_re), ct(c_im)], axis=1)
    dsk = jnp.tile(d_skip.reshape(nb, 1, LANES), (1, 1, SSM_CHUNK))
    return p_grp, p_bt, p_lanes, dsk


def kernel(x, c, w_ada, b_ada, g_mix, w_in, w_ck1, w_ck2, pe_ck, w_cv1, w_cv2, pe_cv,
           a_re, a_im, log_dt, b_re, b_im, c_re, c_im, d_skip, w_glu, b_glu,
           w_out, g_mlp, w_up, w_down, g_final):
    bsz, seq, _ = x.shape
    assert w_ada.shape[0] == 1, "single-layer block"
    assert seq % 1024 == 0 and (seq & (seq - 1)) == 0
    m = bsz * seq
    x2 = x.reshape(m, D_MODEL)

    c_pad = jnp.pad(c, ((0, 8 - bsz), (0, 0)))
    ada = _ada(c_pad, w_ada[0], b_ada)
    ada3 = ada[:bsz].reshape(bsz * 6, 1, D_MODEL)

    proj, vt = _inproj(x2, g_mix, ada3, w_in[0], *_split_w_in(w_in[0]), seq)

    w1s = jnp.stack([w_ck1[0], w_cv1[0]]).reshape(2, CMP_LEN * HEAD_DIM, HEAD_DIM).astype(BF16)
    pes = jnp.stack([pe_ck[0], pe_cv[0]]).reshape(2, 1, CMP_LEN * HEAD_DIM)
    pes = jnp.pad(pes, ((0, 0), (0, 7), (0, 0))).astype(BF16)
    w2s = jnp.stack([w_ck2[0], w_cv2[0]]).astype(BF16)
    kvc, kvct = _compress(proj, w1s, pes, w2s, bsz, seq)

    e_tab = (jnp.arange(seq)[:, None] // SEL_BLOCK == jnp.arange(LANES)[None, :]).astype(BF16)
    y_a = _nsa(proj, kvc, kvct, vt, e_tab, bsz, seq)

    p_grp, p_bt, p_lanes, dsk = _s5_params(a_re[0], a_im[0], log_dt[0], b_re[0], b_im[0],
                                           c_re[0], c_im[0], d_skip[0])
    ys = _s5(proj, p_grp, p_bt, p_lanes, dsk, seq // SSM_CHUNK)

    h1, u2 = _mix(ys, y_a, proj, x2, w_glu[0].astype(BF16), b_glu, w_out[0].astype(BF16), g_mlp, ada3, seq)
    out = _mlp(h1, u2, ada3, w_up[0].astype(BF16), w_down[0].astype(BF16), g_final.reshape(1, D_MODEL), seq)
    return out.reshape(bsz, seq, D_MODEL)
```

```python
import functools

import jax
import jax.numpy as jnp
from jax import lax
from jax.experimental import pallas as pl
from jax.experimental.pallas import tpu as pltpu

F32 = jnp.float32
BF16 = jnp.bfloat16

D_MODEL = 2048
N_HEADS = 16
HEAD_DIM = 128
N_KV_GROUPS = 4
GROUP_SIZE = 4
KV_DIM = N_KV_GROUPS * HEAD_DIM
CMP_LEN = 32
CMP_STRIDE = 16
SEL_BLOCK = 64
SEL_TOPN = 16
WINDOW = 512
SSM_GROUP = 16
SSM_NGROUPS = D_MODEL // SSM_GROUP
SSM_STATE = 64
EPS = 1e-6
NEG = -1e30
BIG = 1e6
LOG2E = 1.4426950408889634
ONES_ROWS = 16

LANES = 128
SSM_CHUNK = 16
GROUPS_PER_BLOCK = LANES // SSM_GROUP
N_SSM_BLOCKS = SSM_NGROUPS // GROUPS_PER_BLOCK

OFF_Q = 0
OFF_SSM = OFF_Q + D_MODEL
OFF_GA = OFF_SSM + D_MODEL
OFF_GB = OFF_GA + D_MODEL
OFF_KC = OFF_GB + D_MODEL
OFF_VC = OFF_KC + KV_DIM
OFF_KS = OFF_VC + KV_DIM
OFF_KW = OFF_KS + KV_DIM
OFF_GATE = OFF_KW + KV_DIM
PROJ_PAD = OFF_GATE + N_KV_GROUPS * LANES
RAW_VS = D_MODEL + 3 * KV_DIM
RAW_KW = RAW_VS + KV_DIM
RAW_VW = RAW_KW + KV_DIM
RAW_GATE = RAW_VW + KV_DIM
RAW_SSM = RAW_GATE + 3 * N_HEADS

VMEM_LIMIT = 56 * 1024 * 1024


def _cparams(sem):
    return pltpu.CompilerParams(dimension_semantics=sem, vmem_limit_bytes=VMEM_LIMIT)


def _t(x):
    r, c = x.shape
    rows = []
    for j in range(c // LANES):
        rows.append(jnp.concatenate([x[i * LANES:(i + 1) * LANES, j * LANES:(j + 1) * LANES].T
                                     for i in range(r // LANES)], axis=1))
    return jnp.concatenate(rows, axis=0)


def _nt_dot(a, b):
    return lax.dot_general(a, b, (((1,), (1,)), ((), ())), preferred_element_type=F32)


def _ada_kernel(c_ref, w_ref, b_ref, o_ref):
    c = c_ref[...]
    cond = c * jax.nn.sigmoid(c)
    o_ref[...] = jnp.dot(cond.astype(BF16), w_ref[...].astype(BF16),
                         preferred_element_type=F32) + b_ref[...]


def _ada(c_pad, w_ada, b_ada):
    n = w_ada.shape[1]
    tn = 1024
    return pl.pallas_call(
        _ada_kernel,
        grid=(n // tn,),
        in_specs=[pl.BlockSpec((8, D_MODEL), lambda j: (0, 0)),
                  pl.BlockSpec((D_MODEL, tn), lambda j: (0, j)),
                  pl.BlockSpec((1, tn), lambda j: (0, j))],
        out_specs=pl.BlockSpec((8, tn), lambda j: (0, j)),
        out_shape=jax.ShapeDtypeStruct((8, n), F32),
        compiler_params=_cparams(("parallel",)),
        name="ada",
    )(c_pad, w_ada, b_ada)


INPROJ_TN = 512
_Q_TILES = D_MODEL // INPROJ_TN
_TAIL_TILES = 3 * D_MODEL // INPROJ_TN
_GATE_TILE = _Q_TILES + _TAIL_TILES + 4


_ROW_UNIT = 16


def _w_row(j):
    k = j - (_Q_TILES + _TAIL_TILES)
    kv_block = jnp.where(k < 3, 4 + k, 8)
    head = jnp.where(j < _Q_TILES, j, kv_block) * (INPROJ_TN // _ROW_UNIT)
    tail = RAW_SSM // _ROW_UNIT + (j - _Q_TILES) * (INPROJ_TN // _ROW_UNIT)
    return jnp.where((j >= _Q_TILES) & (k < 0), tail, head) * _ROW_UNIT


def _inproj_kernel(x_ref, g_ref, sc_ref, sh_ref, w_ref, wg_ref, wvs_ref, wvw_ref, o_ref, vt_ref, u_scr):
    j = pl.program_id(1)

    @pl.when(j == 0)
    def _():
        x = x_ref[...]
        ms = jnp.mean(x * x, axis=-1, keepdims=True)
        y = x * lax.rsqrt(ms + EPS) * g_ref[...]
        u = (y * (1.0 + sc_ref[...]) + sh_ref[...]).astype(BF16)
        u_scr[...] = u
        vt_ref[:KV_DIM, :] = _nt_dot(wvs_ref[...].astype(BF16), u).astype(vt_ref.dtype)
        vt_ref[KV_DIM:, :] = _nt_dot(wvw_ref[...].astype(BF16), u).astype(vt_ref.dtype)

    def emit(w):
        o_ref[...] = _nt_dot(u_scr[...], w).astype(o_ref.dtype)

    pl.when(j == _GATE_TILE)(lambda: emit(wg_ref[...]))
    pl.when(j != _GATE_TILE)(lambda: emit(w_ref[...].astype(BF16)))


def _inproj(x2, g_mix, ada3, w_in_t, w_gate_t, seq):
    m = x2.shape[0]
    tm, tn = 1024, INPROJ_TN
    per_b = seq // tm
    const = dict(pipeline_mode=pl.Buffered(1))
    wblk = (pl.Element(tn), pl.Element(D_MODEL))
    return pl.pallas_call(
        _inproj_kernel,
        grid=(m // tm, PROJ_PAD // tn),
        in_specs=[pl.BlockSpec((tm, D_MODEL), lambda i, j: (i, 0)),
                  pl.BlockSpec((1, D_MODEL), lambda i, j: (0, 0)),
                  pl.BlockSpec((None, 1, D_MODEL), lambda i, j: ((i // per_b) * 6 + 1, 0, 0)),
                  pl.BlockSpec((None, 1, D_MODEL), lambda i, j: ((i // per_b) * 6 + 0, 0, 0)),
                  pl.BlockSpec(wblk, lambda i, j: (_w_row(j), 0)),
                  pl.BlockSpec((tn, D_MODEL), lambda i, j: (0, 0), **const),
                  pl.BlockSpec((tn, D_MODEL), lambda i, j: (RAW_VS // tn, 0), **const),
                  pl.BlockSpec((tn, D_MODEL), lambda i, j: (RAW_VW // tn, 0), **const)],
        out_specs=[pl.BlockSpec((tm, tn), lambda i, j: (i, j)),
                   pl.BlockSpec((2 * KV_DIM, tm), lambda i, j: (0, i))],
        out_shape=[jax.ShapeDtypeStruct((m, PROJ_PAD), BF16),
                   jax.ShapeDtypeStruct((2 * KV_DIM, m), BF16)],
        scratch_shapes=[pltpu.VMEM((tm, D_MODEL), BF16)],
        compiler_params=_cparams(("parallel", "arbitrary")),
        name="inproj",
    )(x2, g_mix, ada3, ada3, w_in_t, w_gate_t, w_in_t, w_in_t)


def _compress_kernel(x_ref, w1_ref, pe_ref, w2_ref, o_ref, ot_ref, xf_scr):
    xf_scr[...] = x_ref[...].astype(F32)
    nc = x_ref.shape[0] // CMP_STRIDE
    x = jnp.concatenate([xf_scr[pl.ds(l, nc, stride=CMP_STRIDE), :].astype(BF16)
                         for l in range(CMP_STRIDE)], axis=1)
    half = CMP_STRIDE * HEAD_DIM
    w1 = w1_ref[...]
    ha = jnp.dot(x, w1[:half], preferred_element_type=F32)
    hb = jnp.dot(x, w1[half:], preferred_element_type=F32)
    hb = pltpu.roll(hb, nc - 1, axis=0)
    pe_t = jnp.dot(pe_ref[...], w1, preferred_element_type=F32)[0:1]
    hid = jax.nn.gelu(ha + hb + pe_t)
    out = jnp.dot(hid.astype(BF16), w2_ref[...], preferred_element_type=F32)
    row = lax.broadcasted_iota(jnp.int32, out.shape, 0)
    out = jnp.where(row < nc - 1, out, 0.0)
    o_ref[...] = out.astype(o_ref.dtype)
    ot_ref[...] = _t(out).astype(ot_ref.dtype)


def _compress(proj, w1s, pes, w2s, bsz, seq):
    nc = seq // CMP_STRIDE
    g = N_KV_GROUPS
    return pl.pallas_call(
        _compress_kernel,
        grid=(2, bsz, g),
        in_specs=[pl.BlockSpec((seq, HEAD_DIM), lambda a, i, j: (i, OFF_KC // HEAD_DIM + a * g + j)),
                  pl.BlockSpec((None, CMP_LEN * HEAD_DIM, HEAD_DIM), lambda a, i, j: (a, 0, 0)),
                  pl.BlockSpec((None, 8, CMP_LEN * HEAD_DIM), lambda a, i, j: (a, 0, 0)),
                  pl.BlockSpec((None, HEAD_DIM, HEAD_DIM), lambda a, i, j: (a, 0, 0))],
        out_specs=[pl.BlockSpec((None, None, None, nc, HEAD_DIM), lambda a, i, j: (a, i, j, 0, 0)),
                   pl.BlockSpec((None, None, None, HEAD_DIM, nc), lambda a, i, j: (a, i, j, 0, 0))],
        out_shape=[jax.ShapeDtypeStruct((2, bsz, g, nc, HEAD_DIM), BF16),
                   jax.ShapeDtypeStruct((2, bsz, g, HEAD_DIM, nc), BF16)],
        scratch_shapes=[pltpu.VMEM((seq, HEAD_DIM), F32)],
        compiler_params=_cparams(("parallel", "parallel", "parallel")),
        name="compress",
    )(proj, w1s, pes, w2s)


def _nsa_kernel(q_ref, gate_ref, kc_ref, vct_ref, ks_ref, e_ref, vst_ref, kw_ref, vwt_ref, o_ref,
                sa_scr, sb_scr, m_scr, acc_scr, *, tq, tk, seq):
    s0 = pl.program_id(2) * tq
    rows = GROUP_SIZE * tq
    nsel = seq // SEL_BLOCK
    topn = min(SEL_TOPN, nsel)
    qscale = HEAD_DIM ** -0.5 * LOG2E

    q = q_ref[...]
    q4 = jnp.concatenate([q[:, r * HEAD_DIM:(r + 1) * HEAD_DIM] for r in range(GROUP_SIZE)], axis=0)
    q4 = (q4.astype(F32) * qscale).astype(BF16)
    t_lane = s0 + (lax.broadcasted_iota(jnp.int32, (1, rows), 1) & (tq - 1))

    kc = kc_ref[...]
    nc = kc.shape[0]
    sc = _nt_dot(kc, q4)
    n_col = lax.broadcasted_iota(jnp.int32, (nc, 1), 0)
    valid_c = (n_col * CMP_STRIDE + (CMP_LEN - 1)) <= t_lane
    sc = jnp.where(valid_c, sc, NEG)
    mc = jnp.max(sc, axis=0, keepdims=True)
    pc = jnp.where(valid_c, jnp.exp2(sc - mc), 0.0)
    pc = pc * (1.0 / jnp.maximum(jnp.sum(pc, axis=0, keepdims=True), 1e-30))
    o_c = jnp.dot(vct_ref[...], pc.astype(BF16), preferred_element_type=F32)
    psum = pc[:, 0:tq] + pc[:, tq:2 * tq] + pc[:, 2 * tq:3 * tq] + pc[:, 3 * tq:4 * tq]

    jj = lax.broadcasted_iota(jnp.int32, (nsel, nc), 0)
    nn = lax.broadcasted_iota(jnp.int32, (nsel, nc), 1)
    ov = ((nn * CMP_STRIDE < jj * SEL_BLOCK + SEL_BLOCK)
          & (nn * CMP_STRIDE + CMP_LEN > jj * SEL_BLOCK)
          & (nn < nc - 1))
    ov = jnp.where(ov, 1.0, 0.0).astype(BF16)
    p_hi = psum.astype(BF16)
    p_lo = (psum - p_hi.astype(F32)).astype(BF16)
    imp_t = (jnp.dot(ov, p_hi, preferred_element_type=F32)
             + jnp.dot(ov, p_lo, preferred_element_type=F32))

    jcol = lax.broadcasted_iota(jnp.int32, (nsel, tq), 0)
    cur = lax.shift_right_logical(s0 + lax.broadcasted_iota(jnp.int32, (nsel, tq), 1), 6)
    valid = jcol <= cur
    forced = (jcol == 0) | (jcol == cur) | (jcol == cur - 1)
    score = jnp.where(forced, BIG, jnp.where(valid, imp_t, -BIG))

    def rank_of(sc_):
        slabs = []
        for sb in range(nsel // 8):
            sj = sc_[sb * 8:(sb + 1) * 8]
            jr = jcol[sb * 8:(sb + 1) * 8, :LANES]
            rank = jnp.zeros((8, LANES), F32)
            for i in range(nsel):
                si = sc_[i:i + 1]
                if i < sb * 8:
                    cond = si >= sj
                elif i >= sb * 8 + 8:
                    cond = si > sj
                else:
                    cond = (si > sj) | ((jr > i) & (si == sj))
                rank = rank + jnp.where(cond, 1.0, 0.0)
            slabs.append(rank)
        return jnp.concatenate(slabs, axis=0)

    rank = jnp.concatenate([rank_of(score[:, c * LANES:(c + 1) * LANES]) for c in range(tq // LANES)], axis=1)
    bias_t = jnp.where((rank < topn) & valid, 0.0, NEG)
    bias_t = jnp.concatenate([bias_t, jnp.zeros((LANES - nsel, tq), F32)], axis=0)
    bias_q = _t(bias_t).astype(BF16)
    q_aug = jnp.concatenate([q4, jnp.concatenate([bias_q] * GROUP_SIZE, axis=0)], axis=1)

    def chunk_off(kb):
        return pl.multiple_of(kb * tk, tk)

    ones_rows = jnp.ones((ONES_ROWS, tk), BF16)

    def reset_stats():
        m_scr[...] = jnp.full((1, rows), NEG, F32)
        acc_scr[...] = jnp.zeros((HEAD_DIM + ONES_ROWS, rows), F32)

    def update(s, kb, vt_ref, mask):
        off = chunk_off(kb)
        if mask is not None:
            kpos = off + lax.broadcasted_iota(jnp.int32, (tk, 1), 0)
            s = jnp.where(mask(kpos), s, NEG)
        m = m_scr[...]
        m_new = jnp.maximum(m, jnp.max(s, axis=0, keepdims=True))
        p = jnp.exp2(s - m_new).astype(BF16)
        vt1 = jnp.concatenate([vt_ref[:, pl.ds(off, tk)], ones_rows], axis=0)
        acc_scr[...] = jnp.exp2(m - m_new) * acc_scr[...] + jnp.dot(vt1, p, preferred_element_type=F32)
        m_scr[...] = m_new

    def result():
        acc = acc_scr[...]
        return acc[:HEAD_DIM] * (1.0 / jnp.maximum(acc[HEAD_DIM:HEAD_DIM + 1], 1e-30))

    hi = (s0 + tq - 1) // tk + 1
    causal = lambda kpos: kpos <= t_lane

    k_lo = hi - 2
    w_lo = jnp.where(k_lo >= 0, t_lane - (WINDOW - 1), seq)
    k_lo = jnp.maximum(k_lo, 0)
    reset_stats()
    update(_nt_dot(kw_ref[pl.ds(chunk_off(k_lo), tk), :], q4), k_lo, vwt_ref, lambda kpos: kpos >= w_lo)
    update(_nt_dot(kw_ref[pl.ds(chunk_off(hi - 1), tk), :], q4), hi - 1, vwt_ref, causal)
    o_w = result()

    def sel_scores_into(dst, kb):
        off = chunk_off(kb)
        keys = jnp.concatenate([ks_ref[pl.ds(off, tk), :], e_ref[pl.ds(off, tk), :]], axis=1)
        dst[...] = _nt_dot(keys, q_aug)

    reset_stats()
    sel_scores_into(sa_scr, 0)
    n_pairs = lax.shift_right_logical(hi - 1, 1)

    def pair(i, c):
        k0 = 2 * i
        sel_scores_into(sb_scr, k0 + 1)
        update(sa_scr[...], k0, vst_ref, None)
        sel_scores_into(sa_scr, k0 + 2)
        update(sb_scr[...], k0 + 1, vst_ref, None)
        return c

    lax.fori_loop(0, n_pairs, pair, 0)

    @pl.when(((hi - 1) & 1) == 1)
    def _():
        sel_scores_into(sb_scr, hi - 1)
        update(sa_scr[...], hi - 2, vst_ref, None)
        update(sb_scr[...], hi - 1, vst_ref, causal)

    @pl.when(((hi - 1) & 1) == 0)
    def _():
        update(sa_scr[...], hi - 1, vst_ref, causal)

    o_s = result()

    sg = _t(jax.nn.sigmoid(gate_ref[...].astype(F32)))
    outs = []
    for r in range(GROUP_SIZE):
        sl = slice(r * tq, (r + 1) * tq)
        o_r = (sg[3 * r:3 * r + 1] * o_c[:, sl] + sg[3 * r + 1:3 * r + 2] * o_s[:, sl]
               + sg[3 * r + 2:3 * r + 3] * o_w[:, sl])
        outs.append(_t(o_r))
    o_ref[...] = jnp.concatenate(outs, axis=1).astype(o_ref.dtype)


def _nsa(proj, kvc, kvct, vt, e_tab, bsz, seq):
    tq, tk = 256, 512
    assert tk == WINDOW and tk % tq == 0
    nq = seq // tq
    gw = GROUP_SIZE * HEAD_DIM
    nc = kvc.shape[3]
    g4 = N_KV_GROUPS

    def k_spec(off):
        cb = off // HEAD_DIM
        return pl.BlockSpec((seq, HEAD_DIM), lambda b, g, i: (b, cb + g))

    return pl.pallas_call(
        functools.partial(_nsa_kernel, tq=tq, tk=tk, seq=seq),
        grid=(bsz, g4, nq),
        in_specs=[pl.BlockSpec((tq, gw), lambda b, g, i: (b * nq + i, g)),
                  pl.BlockSpec((tq, LANES), lambda b, g, i: (b * nq + i, OFF_GATE // LANES + g)),
                  pl.BlockSpec((None, None, None, nc, HEAD_DIM), lambda b, g, i: (0, b, g, 0, 0)),
                  pl.BlockSpec((None, None, None, HEAD_DIM, nc), lambda b, g, i: (1, b, g, 0, 0)),
                  k_spec(OFF_KS),
                  pl.BlockSpec((seq, LANES), lambda b, g, i: (0, 0)),
                  pl.BlockSpec((HEAD_DIM, seq), lambda b, g, i: (g, b)),
                  k_spec(OFF_KW),
                  pl.BlockSpec((HEAD_DIM, seq), lambda b, g, i: (g4 + g, b))],
        out_specs=pl.BlockSpec((tq, gw), lambda b, g, i: (b * nq + i, g)),
        out_shape=jax.ShapeDtypeStruct((bsz * seq, D_MODEL), BF16),
        scratch_shapes=[pltpu.VMEM((tk, GROUP_SIZE * tq), F32), pltpu.VMEM((tk, GROUP_SIZE * tq), F32),
                        pltpu.VMEM((1, GROUP_SIZE * tq), F32),
                        pltpu.VMEM((HEAD_DIM + ONES_ROWS, GROUP_SIZE * tq), F32)],
        compiler_params=_cparams(("parallel", "parallel", "arbitrary")),
        name="nsa",
    )(proj, proj, kvc, kvct, proj, e_tab, vt, proj, vt)


def _s5_kernel(u_ref, pg_ref, bt_ref, pl_ref, dsk_ref, o_ref, wt_scr, ws_scr, wo_scr, xp_scr, tok_scr,
               *, chunks_per_seq):
    L = SSM_CHUNK
    hp = lax.Precision.HIGHEST
    are_g, aim_g, ldt_g = pg_ref[0], pg_ref[1], pg_ref[2]
    bt = bt_ref[...]
    are_l, aim_l, ldt_l, cr_l, ci_l = pl_ref[0], pl_ref[1], pl_ref[2], pl_ref[3], pl_ref[4]

    lane = lax.broadcasted_iota(jnp.int32, (LANES, LANES), 1)
    rowi = lax.broadcasted_iota(jnp.int32, (LANES, LANES), 0)
    is_re = lane < SSM_STATE
    row_grp = rowi // SSM_GROUP
    lane_grp = lane // SSM_GROUP

    def rows16(z):
        return jnp.concatenate([jnp.broadcast_to(z[g:g + 1], (SSM_GROUP, LANES))
                                for g in range(GROUPS_PER_BLOCK)], axis=0)

    def cmul(x, zr, zi):
        return x * zr + pltpu.roll(x, SSM_STATE, axis=1) * jnp.where(is_re, -zi, zi)

    dt_g = jnp.exp(ldt_g)
    l1r = jnp.exp(are_g * dt_g) * jnp.cos(aim_g * dt_g)
    l1i = jnp.exp(are_g * dt_g) * jnp.sin(aim_g * dt_g)
    den = are_g * are_g + aim_g * aim_g
    nr = l1r - 1.0
    coef_r = (nr * are_g + l1i * aim_g) / den
    coef_i = (l1i * are_g - nr * aim_g) / den
    bbar = cmul(bt, rows16(coef_r), rows16(coef_i))

    pw_g = [(jnp.ones_like(l1r), jnp.zeros_like(l1r))]
    for _ in range(L):
        pr, pi = pw_g[-1]
        pw_g.append((pr * l1r - pi * l1i, pr * l1i + pi * l1r))

    dt_l = jnp.exp(ldt_l)
    m1r = jnp.exp(are_l * dt_l) * jnp.cos(aim_l * dt_l)
    m1i = jnp.exp(are_l * dt_l) * jnp.sin(aim_l * dt_l)
    gs = []
    pr, pi = jnp.ones_like(m1r), jnp.zeros_like(m1r)
    for _ in range(L + 1):
        gs.append(jnp.concatenate([cr_l * pr - ci_l * pi, -(cr_l * pi + ci_l * pr)], axis=0))
        pr, pi = pr * m1r - pi * m1i, pr * m1i + pi * m1r

    same_grp = row_grp == lane_grp
    zeros_tile = jnp.zeros((LANES, LANES), BF16)

    for t in range(L):
        d_t = jnp.dot(bbar, gs[t], precision=hp, preferred_element_type=F32)
        d_t = jnp.where(same_grp, d_t, 0.0).astype(BF16)
        for j in range(L - t):
            i = j + t
            wt_scr[j * LANES:(j + 1) * LANES, i * LANES:(i + 1) * LANES] = d_t
    for j in range(L):
        for i in range(j):
            wt_scr[j * LANES:(j + 1) * LANES, i * LANES:(i + 1) * LANES] = zeros_tile

    for j in range(L):
        zr, zi = pw_g[L - 1 - j]
        s_j = cmul(bbar, rows16(zr), rows16(zi))
        for g in range(GROUPS_PER_BLOCK):
            ws_scr[j * LANES:(j + 1) * LANES, g * LANES:(g + 1) * LANES] = (
                jnp.where(row_grp == g, s_j, 0.0).astype(BF16))

    for g in range(GROUPS_PER_BLOCK):
        for i in range(L):
            wo_scr[g * LANES:(g + 1) * LANES, i * LANES:(i + 1) * LANES] = (
                jnp.where(lane_grp == g, gs[i + 1], 0.0).astype(BF16))

    tok_scr[...] = u_ref[...].astype(F32)
    nrow = u_ref.shape[0] // L
    u = jnp.concatenate([tok_scr[pl.ds(j, nrow, stride=L), :].astype(BF16) for j in range(L)], axis=1)
    y = jnp.dot(u, wt_scr[...], preferred_element_type=F32)
    st = jnp.dot(u, ws_scr[...], preferred_element_type=F32)

    kk = lax.broadcasted_iota(jnp.int32, (nrow, 1), 0) & (chunks_per_seq - 1)
    lane1 = lax.broadcasted_iota(jnp.int32, (1, LANES), 1)
    for g in range(GROUPS_PER_BLOCK):
        xs = st[:, g * LANES:(g + 1) * LANES]
        zr = pw_g[L][0][g:g + 1]
        zi = pw_g[L][1][g:g + 1]
        d = 1
        while d < chunks_per_seq:
            sh = jnp.where(kk >= d, pltpu.roll(xs, d, axis=0), 0.0)
            zmix = jnp.where(lane1 < SSM_STATE, -zi, zi)
            xs = xs + sh * zr + pltpu.roll(sh, SSM_STATE, axis=1) * zmix
            zr, zi = zr * zr - zi * zi, 2.0 * zr * zi
            d *= 2
        prev = jnp.where(kk >= 1, pltpu.roll(xs, 1, axis=0), 0.0)
        xp_scr[:, g * LANES:(g + 1) * LANES] = prev.astype(BF16)

    y = y + jnp.dot(xp_scr[...], wo_scr[...], preferred_element_type=F32)
    y = jax.nn.gelu(y + dsk_ref[...] * u.astype(F32))
    for i in range(L):
        tok_scr[pl.ds(i, nrow, stride=L), :] = y[:, i * LANES:(i + 1) * LANES]
    o_ref[...] = tok_scr[...].astype(o_ref.dtype)


def _s5(proj, p_grp, p_bt, p_lanes, dsk, chunks_per_seq):
    m = proj.shape[0]
    nrow, kk = m // SSM_CHUNK, SSM_CHUNK * LANES
    nst = GROUPS_PER_BLOCK * LANES
    return pl.pallas_call(
        functools.partial(_s5_kernel, chunks_per_seq=chunks_per_seq),
        grid=(N_SSM_BLOCKS,),
        in_specs=[pl.BlockSpec((m, LANES), lambda i: (0, OFF_SSM // LANES + i)),
                  pl.BlockSpec((None, 3, GROUPS_PER_BLOCK, LANES), lambda i: (i, 0, 0, 0)),
                  pl.BlockSpec((None, LANES, LANES), lambda i: (i, 0, 0)),
                  pl.BlockSpec((None, 5, SSM_STATE, LANES), lambda i: (i, 0, 0, 0)),
                  pl.BlockSpec((None, 1, kk), lambda i: (i, 0, 0))],
        out_specs=pl.BlockSpec((m, LANES), lambda i: (0, i)),
        out_shape=jax.ShapeDtypeStruct((m, D_MODEL), BF16),
        scratch_shapes=[pltpu.VMEM((kk, kk), BF16), pltpu.VMEM((kk, nst), BF16),
                        pltpu.VMEM((nst, kk), BF16), pltpu.VMEM((nrow, nst), BF16),
                        pltpu.VMEM((m, LANES), F32)],
        compiler_params=_cparams(("parallel",)),
        name="s5",
    )(proj, p_grp, p_bt, p_lanes, dsk)


def _mix_kernel(ys_ref, ya_ref, ga_ref, gb_ref, x_ref, wg_ref, bg_ref, wo_ref, gt_ref, gm_ref, sc_ref, sh_ref,
                h_ref, u_ref):
    ys = ys_ref[...]
    z = jnp.dot(ys, wg_ref[...], preferred_element_type=F32) + bg_ref[...]
    yb = ys.astype(F32) * jax.nn.sigmoid(z)
    mix = (jax.nn.sigmoid(ga_ref[...].astype(F32)) * ya_ref[...].astype(F32)
           + jax.nn.sigmoid(gb_ref[...].astype(F32)) * yb)
    h = x_ref[...] + gt_ref[...] * jnp.dot(mix.astype(BF16), wo_ref[...], preferred_element_type=F32)
    h_ref[...] = h
    ms = jnp.mean(h * h, axis=-1, keepdims=True)
    y = h * lax.rsqrt(ms + EPS) * gm_ref[...]
    u_ref[...] = (y * (1.0 + sc_ref[...]) + sh_ref[...]).astype(u_ref.dtype)


def _mix(ys, ya, proj, x2, w_glu, b_glu, w_out, g_mlp, ada3, seq):
    m = ys.shape[0]
    tm = 256
    per_b = seq // tm
    const = dict(pipeline_mode=pl.Buffered(1))
    row = lambda cb: pl.BlockSpec((tm, D_MODEL), lambda i: (i, cb))

    def ada_spec(k):
        return pl.BlockSpec((None, 1, D_MODEL), lambda i: ((i // per_b) * 6 + k, 0, 0))

    return pl.pallas_call(
        _mix_kernel,
        grid=(m // tm,),
        in_specs=[row(0), row(0), row(OFF_GA // D_MODEL), row(OFF_GB // D_MODEL), row(0),
                  pl.BlockSpec((D_MODEL, D_MODEL), lambda i: (0, 0), **const),
                  pl.BlockSpec((1, D_MODEL), lambda i: (0, 0)),
                  pl.BlockSpec((D_MODEL, D_MODEL), lambda i: (0, 0), **const),
                  ada_spec(2),
                  pl.BlockSpec((1, D_MODEL), lambda i: (0, 0)),
                  ada_spec(4), ada_spec(3)],
        out_specs=[row(0), row(0)],
        out_shape=[jax.ShapeDtypeStruct((m, D_MODEL), F32), jax.ShapeDtypeStruct((m, D_MODEL), BF16)],
        compiler_params=_cparams(("parallel",)),
        name="mix",
    )(ys, ya, proj, proj, x2, w_glu, b_glu, w_out, ada3, g_mlp, ada3, ada3)


def _mlp_kernel(h_ref, u_ref, gt_ref, wu_ref, wd_ref, gf_ref, o_ref, acc_scr):
    f = pl.program_id(1)

    @pl.when(f == 0)
    def _():
        acc_scr[...] = jnp.zeros_like(acc_scr)

    a = jnp.dot(u_ref[...], wu_ref[...], preferred_element_type=F32)
    a = jnp.square(jnp.maximum(a, 0.0))
    acc_scr[...] += jnp.dot(a.astype(BF16), wd_ref[...], preferred_element_type=F32)

    @pl.when(f == pl.num_programs(1) - 1)
    def _():
        h2 = h_ref[...] + gt_ref[...] * acc_scr[...]
        ms = jnp.mean(h2 * h2, axis=-1, keepdims=True)
        o_ref[...] = h2 * lax.rsqrt(ms + EPS) * gf_ref[...]


def _mlp(h1, u2, ada3, w_up, w_down, g_final, seq):
    m = h1.shape[0]
    tm, tf = 512, 512
    per_b = seq // tm
    ff = w_up.shape[1]
    return pl.pallas_call(
        _mlp_kernel,
        grid=(m // tm, ff // tf),
        in_specs=[pl.BlockSpec((tm, D_MODEL), lambda i, f: (i, 0)),
                  pl.BlockSpec((tm, D_MODEL), lambda i, f: (i, 0)),
                  pl.BlockSpec((None, 1, D_MODEL), lambda i, f: ((i // per_b) * 6 + 5, 0, 0)),
                  pl.BlockSpec((D_MODEL, tf), lambda i, f: (0, f)),
                  pl.BlockSpec((tf, D_MODEL), lambda i, f: (f, 0)),
                  pl.BlockSpec((1, D_MODEL), lambda i, f: (0, 0))],
        out_specs=pl.BlockSpec((tm, D_MODEL), lambda i, f: (i, 0)),
        out_shape=jax.ShapeDtypeStruct((m, D_MODEL), F32),
        scratch_shapes=[pltpu.VMEM((tm, D_MODEL), F32)],
        compiler_params=_cparams(("parallel", "arbitrary")),
        name="mlp",
    )(h1, u2, ada3, w_up, w_down, g_final)


def _gate_rows(wt):
    wg = wt[RAW_GATE:RAW_SSM].reshape(N_KV_GROUPS, 3 * GROUP_SIZE, D_MODEL)
    wg = jnp.pad(wg, ((0, 0), (0, LANES - 3 * GROUP_SIZE), (0, 0))).reshape(N_KV_GROUPS * LANES, D_MODEL)
    return wg.astype(BF16)


def _s5_params(a_re, a_im, log_dt, b_re, b_im, c_re, c_im, d_skip):
    nb, gp, p, cg = N_SSM_BLOCKS, GROUPS_PER_BLOCK, SSM_STATE, SSM_GROUP
    ldt = jnp.broadcast_to(log_dt[:, None], (SSM_NGROUPS, p))

    def grp(a):
        return jnp.concatenate([a, a], axis=-1).reshape(nb, gp, LANES)

    def lanes(a):
        a = jnp.broadcast_to(a.reshape(nb, gp, p, 1).transpose(0, 2, 1, 3), (nb, p, gp, cg))
        return a.reshape(nb, p, LANES)

    def ct(cm):
        return cm.reshape(nb, gp, cg, p).transpose(0, 3, 1, 2).reshape(nb, p, LANES)

    p_grp = jnp.stack([grp(a_re), grp(a_im), grp(ldt)], axis=1)
    p_bt = jnp.stack([b_re, b_im], axis=1).transpose(0, 3, 1, 2).reshape(nb, LANES, LANES)
    p_lanes = jnp.stack([lanes(a_re), lanes(a_im), lanes(ldt), ct(c_re), ct(c_im)], axis=1)
    dsk = jnp.tile(d_skip.reshape(nb, 1, LANES), (1, 1, SSM_CHUNK))
    return p_grp, p_bt, p_lanes, dsk


def kernel(x, c, w_ada, b_ada, g_mix, w_in, w_ck1, w_ck2, pe_ck, w_cv1, w_cv2, pe_cv,
           a_re, a_im, log_dt, b_re, b_im, c_re, c_im, d_skip, w_glu, b_glu,
           w_out, g_mlp, w_up, w_down, g_final):
    bsz, seq, _ = x.shape
    assert w_ada.shape[0] == 1, "single-layer block"
    assert seq % 1024 == 0 and (seq & (seq - 1)) == 0
    m = bsz * seq
    x2 = x.reshape(m, D_MODEL)

    c_pad = jnp.pad(c, ((0, 8 - bsz), (0, 0)))
    ada = _ada(c_pad, w_ada[0], b_ada)
    ada3 = ada[:bsz].reshape(bsz * 6, 1, D_MODEL)

    w_in_t = w_in[0].T
    proj, vt = _inproj(x2, g_mix, ada3, w_in_t, _gate_rows(w_in_t), seq)

    w1s = jnp.stack([w_ck1[0], w_cv1[0]]).reshape(2, CMP_LEN * HEAD_DIM, HEAD_DIM).astype(BF16)
    pes = jnp.stack([pe_ck[0], pe_cv[0]]).reshape(2, 1, CMP_LEN * HEAD_DIM)
    pes = jnp.pad(pes, ((0, 0), (0, 7), (0, 0))).astype(BF16)
    w2s = jnp.stack([w_ck2[0], w_cv2[0]]).astype(BF16)
    kvc, kvct = _compress(proj, w1s, pes, w2s, bsz, seq)

    e_tab = (jnp.arange(seq)[:, None] // SEL_BLOCK == jnp.arange(LANES)[None, :]).astype(BF16)
    y_a = _nsa(proj, kvc, kvct, vt, e_tab, bsz, seq)

    p_grp, p_bt, p_lanes, dsk = _s5_params(a_re[0], a_im[0], log_dt[0], b_re[0], b_im[0],
                                           c_re[0], c_im[0], d_skip[0])
    ys = _s5(proj, p_grp, p_bt, p_lanes, dsk, seq // SSM_CHUNK)

    h1, u2 = _mix(ys, y_a, proj, x2, w_glu[0].astype(BF16), b_glu, w_out[0].astype(BF16), g_mlp, ada3, seq)
    out = _mlp(h1, u2, ada3, w_up[0].astype(BF16), w_down[0].astype(BF16), g_final.reshape(1, D_MODEL), seq)
    return out.reshape(bsz, seq, D_MODEL)
```

```python
import functools

import jax
import jax.numpy as jnp
from jax import lax
from jax.experimental import pallas as pl
from jax.experimental.pallas import tpu as pltpu

F32 = jnp.float32
BF16 = jnp.bfloat16

D_MODEL = 2048
N_HEADS = 16
HEAD_DIM = 128
N_KV_GROUPS = 4
GROUP_SIZE = 4
KV_DIM = N_KV_GROUPS * HEAD_DIM
CMP_LEN = 32
CMP_STRIDE = 16
SEL_BLOCK = 64
SEL_TOPN = 16
WINDOW = 512
SSM_GROUP = 16
SSM_NGROUPS = D_MODEL // SSM_GROUP
SSM_STATE = 64
EPS = 1e-6
NEG = -1e30
BIG = 1e6
LOG2E = 1.4426950408889634
ONES_ROWS = 16

LANES = 128
SSM_CHUNK = 16
GROUPS_PER_BLOCK = LANES // SSM_GROUP
N_SSM_BLOCKS = SSM_NGROUPS // GROUPS_PER_BLOCK

OFF_Q = 0
OFF_SSM = OFF_Q + D_MODEL
OFF_GA = OFF_SSM + D_MODEL
OFF_GB = OFF_GA + D_MODEL
OFF_KC = OFF_GB + D_MODEL
OFF_VC = OFF_KC + KV_DIM
OFF_KS = OFF_VC + KV_DIM
OFF_KW = OFF_KS + KV_DIM
OFF_GATE = OFF_KW + KV_DIM
PROJ_PAD = OFF_GATE + N_KV_GROUPS * LANES
RAW_VS = D_MODEL + 3 * KV_DIM
RAW_KW = RAW_VS + KV_DIM
RAW_VW = RAW_KW + KV_DIM
RAW_GATE = RAW_VW + KV_DIM
RAW_SSM = RAW_GATE + 3 * N_HEADS

VMEM_LIMIT = 56 * 1024 * 1024


def _cparams(sem):
    return pltpu.CompilerParams(dimension_semantics=sem, vmem_limit_bytes=VMEM_LIMIT)


def _t(x):
    r, c = x.shape
    rows = []
    for j in range(c // LANES):
        rows.append(jnp.concatenate([x[i * LANES:(i + 1) * LANES, j * LANES:(j + 1) * LANES].T
                                     for i in range(r // LANES)], axis=1))
    return jnp.concatenate(rows, axis=0)


def _nt_dot(a, b):
    return lax.dot_general(a, b, (((1,), (1,)), ((), ())), preferred_element_type=F32)


def _ada_kernel(c_ref, w_ref, b_ref, o_ref):
    c = c_ref[...]
    cond = c * jax.nn.sigmoid(c)
    o_ref[...] = jnp.dot(cond.astype(BF16), w_ref[...].astype(BF16),
                         preferred_element_type=F32) + b_ref[...]


def _ada(c_pad, w_ada, b_ada):
    n = w_ada.shape[1]
    tn = 1024
    return pl.pallas_call(
        _ada_kernel,
        grid=(n // tn,),
        in_specs=[pl.BlockSpec((8, D_MODEL), lambda j: (0, 0)),
                  pl.BlockSpec((D_MODEL, tn), lambda j: (0, j)),
                  pl.BlockSpec((1, tn), lambda j: (0, j))],
        out_specs=pl.BlockSpec((8, tn), lambda j: (0, j)),
        out_shape=jax.ShapeDtypeStruct((8, n), F32),
        compiler_params=_cparams(("parallel",)),
        name="ada",
    )(c_pad, w_ada, b_ada)


INPROJ_TN = 512
_Q_TILES = D_MODEL // INPROJ_TN
_TAIL_TILES = 3 * D_MODEL // INPROJ_TN
_GATE_TILE = _Q_TILES + _TAIL_TILES + 4


_ROW_UNIT = 16


def _w_row(j):
    k = j - (_Q_TILES + _TAIL_TILES)
    kv_block = jnp.where(k < 3, 4 + k, 8)
    head = jnp.where(j < _Q_TILES, j, kv_block) * (INPROJ_TN // _ROW_UNIT)
    tail = RAW_SSM // _ROW_UNIT + (j - _Q_TILES) * (INPROJ_TN // _ROW_UNIT)
    return jnp.where((j >= _Q_TILES) & (k < 0), tail, head) * _ROW_UNIT


def _inproj_kernel(x_ref, g_ref, sc_ref, sh_ref, w_ref, wg_ref, wvs_ref, wvw_ref, o_ref, vt_ref, u_scr):
    j = pl.program_id(1)

    @pl.when(j == 0)
    def _():
        x = x_ref[...]
        ms = jnp.mean(x * x, axis=-1, keepdims=True)
        y = x * lax.rsqrt(ms + EPS) * g_ref[...]
        u = (y * (1.0 + sc_ref[...]) + sh_ref[...]).astype(BF16)
        u_scr[...] = u
        vt_ref[:KV_DIM, :] = _nt_dot(wvs_ref[...].astype(BF16), u).astype(vt_ref.dtype)
        vt_ref[KV_DIM:, :] = _nt_dot(wvw_ref[...].astype(BF16), u).astype(vt_ref.dtype)

    def emit(w):
        o_ref[...] = _nt_dot(u_scr[...], w).astype(o_ref.dtype)

    pl.when(j == _GATE_TILE)(lambda: emit(wg_ref[...]))
    pl.when(j != _GATE_TILE)(lambda: emit(w_ref[...].astype(BF16)))


def _inproj(x2, g_mix, ada3, w_in_t, w_gate_t, seq):
    m = x2.shape[0]
    tm, tn = 1024, INPROJ_TN
    per_b = seq // tm
    const = dict(pipeline_mode=pl.Buffered(1))
    wblk = (pl.Element(tn), pl.Element(D_MODEL))
    return pl.pallas_call(
        _inproj_kernel,
        grid=(m // tm, PROJ_PAD // tn),
        in_specs=[pl.BlockSpec((tm, D_MODEL), lambda i, j: (i, 0)),
                  pl.BlockSpec((1, D_MODEL), lambda i, j: (0, 0)),
                  pl.BlockSpec((None, 1, D_MODEL), lambda i, j: ((i // per_b) * 6 + 1, 0, 0)),
                  pl.BlockSpec((None, 1, D_MODEL), lambda i, j: ((i // per_b) * 6 + 0, 0, 0)),
                  pl.BlockSpec(wblk, lambda i, j: (_w_row(j), 0)),
                  pl.BlockSpec((tn, D_MODEL), lambda i, j: (0, 0), **const),
                  pl.BlockSpec((tn, D_MODEL), lambda i, j: (RAW_VS // tn, 0), **const),
                  pl.BlockSpec((tn, D_MODEL), lambda i, j: (RAW_VW // tn, 0), **const)],
        out_specs=[pl.BlockSpec((tm, tn), lambda i, j: (i, j)),
                   pl.BlockSpec((2 * KV_DIM, tm), lambda i, j: (0, i))],
        out_shape=[jax.ShapeDtypeStruct((m, PROJ_PAD), BF16),
                   jax.ShapeDtypeStruct((2 * KV_DIM, m), BF16)],
        scratch_shapes=[pltpu.VMEM((tm, D_MODEL), BF16)],
        compiler_params=_cparams(("parallel", "arbitrary")),
        name="inproj",
    )(x2, g_mix, ada3, ada3, w_in_t, w_gate_t, w_in_t, w_in_t)


def _compress_kernel(x_ref, w1_ref, pe_ref, w2_ref, o_ref, ot_ref, xf_scr):
    xf_scr[...] = x_ref[...].astype(F32)
    nc = x_ref.shape[0] // CMP_STRIDE
    x = jnp.concatenate([xf_scr[pl.ds(l, nc, stride=CMP_STRIDE), :].astype(BF16)
                         for l in range(CMP_STRIDE)], axis=1)
    half = CMP_STRIDE * HEAD_DIM
    w1 = w1_ref[...]
    ha = jnp.dot(x, w1[:half], preferred_element_type=F32)
    hb = jnp.dot(x, w1[half:], preferred_element_type=F32)
    hb = pltpu.roll(hb, nc - 1, axis=0)
    pe_t = jnp.dot(pe_ref[...], w1, preferred_element_type=F32)[0:1]
    hid = jax.nn.gelu(ha + hb + pe_t)
    out = jnp.dot(hid.astype(BF16), w2_ref[...], preferred_element_type=F32)
    row = lax.broadcasted_iota(jnp.int32, out.shape, 0)
    out = jnp.where(row < nc - 1, out, 0.0)
    o_ref[...] = out.astype(o_ref.dtype)
    ot_ref[...] = _t(out).astype(ot_ref.dtype)


def _compress(proj, w1s, pes, w2s, bsz, seq):
    nc = seq // CMP_STRIDE
    g = N_KV_GROUPS
    return pl.pallas_call(
        _compress_kernel,
        grid=(2, bsz, g),
        in_specs=[pl.BlockSpec((seq, HEAD_DIM), lambda a, i, j: (i, OFF_KC // HEAD_DIM + a * g + j)),
                  pl.BlockSpec((None, CMP_LEN * HEAD_DIM, HEAD_DIM), lambda a, i, j: (a, 0, 0)),
                  pl.BlockSpec((None, 8, CMP_LEN * HEAD_DIM), lambda a, i, j: (a, 0, 0)),
                  pl.BlockSpec((None, HEAD_DIM, HEAD_DIM), lambda a, i, j: (a, 0, 0))],
        out_specs=[pl.BlockSpec((None, None, None, nc, HEAD_DIM), lambda a, i, j: (a, i, j, 0, 0)),
                   pl.BlockSpec((None, None, None, HEAD_DIM, nc), lambda a, i, j: (a, i, j, 0, 0))],
        out_shape=[jax.ShapeDtypeStruct((2, bsz, g, nc, HEAD_DIM), BF16),
                   jax.ShapeDtypeStruct((2, bsz, g, HEAD_DIM, nc), BF16)],
        scratch_shapes=[pltpu.VMEM((seq, HEAD_DIM), F32)],
        compiler_params=_cparams(("parallel", "parallel", "parallel")),
        name="compress",
    )(proj, w1s, pes, w2s)


NSA_STREAMS = 2


def _nsa_kernel(q_ref, gate_ref, kc_ref, vct_ref, ks_ref, e_ref, vst_ref, kw_ref, vwt_ref, o_ref,
                sa_scr, sb_scr, m_scr, acc_scr, *, tq, tk, seq):
    s0 = pl.program_id(2) * tq
    rows = GROUP_SIZE * tq
    gw = GROUP_SIZE * HEAD_DIM
    nsel = seq // SEL_BLOCK
    nc = kc_ref.shape[1]
    topn = min(SEL_TOPN, nsel)
    qscale = HEAD_DIM ** -0.5 * LOG2E
    streams = range(NSA_STREAMS)
    hd = lambda k: slice(k * HEAD_DIM, (k + 1) * HEAD_DIM)
    t_lane = s0 + (lax.broadcasted_iota(jnp.int32, (1, rows), 1) & (tq - 1))

    def reset_stats(k):
        m_scr[k] = jnp.full((1, rows), NEG, F32)
        acc_scr[k] = jnp.zeros((HEAD_DIM + ONES_ROWS, rows), F32)

    def update(k, s, off, vt_ref, mask):
        n = s.shape[0]
        if mask is not None:
            kpos = off + lax.broadcasted_iota(jnp.int32, (n, 1), 0)
            s = jnp.where(mask(kpos), s, NEG)
        m = m_scr[k]
        m_new = jnp.maximum(m, jnp.max(s, axis=0, keepdims=True))
        p = jnp.exp2(s - m_new).astype(BF16)
        vt1 = jnp.concatenate([vt_ref[hd(k), pl.ds(off, n)], jnp.ones((ONES_ROWS, n), BF16)], axis=0)
        acc_scr[k] = jnp.exp2(m - m_new) * acc_scr[k] + jnp.dot(vt1, p, preferred_element_type=F32)
        m_scr[k] = m_new

    def result(k):
        acc = acc_scr[k]
        return acc[:HEAD_DIM] * (1.0 / jnp.maximum(acc[HEAD_DIM:HEAD_DIM + 1], 1e-30))

    q4 = []
    for k in streams:
        q = q_ref[:, k * gw:(k + 1) * gw]
        qk = jnp.concatenate([q[:, hd(r)] for r in range(GROUP_SIZE)], axis=0)
        q4.append((qk.astype(F32) * qscale).astype(BF16))

    causal = lambda kpos: kpos <= t_lane

    for k in streams:
        reset_stats(k)
    n_back = WINDOW // tq
    win = []
    for c in range(n_back, 0, -1):
        off = s0 - c * tq
        if c == n_back:
            lo_edge = jnp.where(off >= 0, t_lane - (WINDOW - 1), seq)
        else:
            lo_edge = jnp.where(off >= 0, 0, seq)
        win.append((pl.multiple_of(jnp.maximum(off, 0), tq), lambda kpos, e=lo_edge: kpos >= e))
    win.append((pl.multiple_of(s0, tq), causal))

    def win_scores_into(k, dst, off):
        dst[k, :tq, :] = _nt_dot(kw_ref[pl.ds(off, tq), hd(k)], q4[k])

    bufs = (sa_scr, sb_scr)
    for k in streams:
        win_scores_into(k, bufs[0], win[0][0])
    for c, (off, mask) in enumerate(win):
        if c + 1 < len(win):
            for k in streams:
                win_scores_into(k, bufs[(c + 1) % 2], win[c + 1][0])
        for k in streams:
            update(k, bufs[c % 2][k, :tq, :], off, vwt_ref, mask)
    o_w = [result(k) for k in streams]

    n_col = lax.broadcasted_iota(jnp.int32, (nc, 1), 0)
    valid_c = (n_col * CMP_STRIDE + (CMP_LEN - 1)) <= t_lane
    jj = lax.broadcasted_iota(jnp.int32, (nsel, nc), 0)
    nn = lax.broadcasted_iota(jnp.int32, (nsel, nc), 1)
    ov = ((nn * CMP_STRIDE < jj * SEL_BLOCK + SEL_BLOCK)
          & (nn * CMP_STRIDE + CMP_LEN > jj * SEL_BLOCK)
          & (nn < nc - 1))
    ov = jnp.where(ov, 1.0, 0.0).astype(BF16)
    jcol = lax.broadcasted_iota(jnp.int32, (nsel, tq), 0)
    cur = lax.shift_right_logical(s0 + lax.broadcasted_iota(jnp.int32, (nsel, tq), 1), 6)
    valid = jcol <= cur
    forced = (jcol == 0) | (jcol == cur) | (jcol == cur - 1)
    o_c, score = [], []
    for k in streams:
        sc = jnp.where(valid_c, _nt_dot(kc_ref[k], q4[k]), NEG)
        mc = jnp.max(sc, axis=0, keepdims=True)
        pc = jnp.where(valid_c, jnp.exp2(sc - mc), 0.0)
        pc = pc * (1.0 / jnp.maximum(jnp.sum(pc, axis=0, keepdims=True), 1e-30))
        o_c.append(jnp.dot(vct_ref[k], pc.astype(BF16), preferred_element_type=F32))
        psum = pc[:, 0:tq] + pc[:, tq:2 * tq] + pc[:, 2 * tq:3 * tq] + pc[:, 3 * tq:4 * tq]
        p_hi = psum.astype(BF16)
        p_lo = (psum - p_hi.astype(F32)).astype(BF16)
        imp_t = (jnp.dot(ov, p_hi, preferred_element_type=F32)
                 + jnp.dot(ov, p_lo, preferred_element_type=F32))
        score.append(jnp.where(forced, BIG, jnp.where(valid, imp_t, -BIG)))

    jcol_f = jcol.astype(F32)
    picked = [jnp.zeros((nsel, tq), F32) for _ in streams]
    for _ in range(topn):
        for k in streams:
            best = jnp.max(score[k], axis=0, keepdims=True)
            first = jnp.min(jnp.where(score[k] == best, jcol_f, float(nsel)), axis=0, keepdims=True)
            hit = jcol_f == first
            picked[k] = jnp.where(hit, 1.0, picked[k])
            score[k] = jnp.where(hit, -jnp.inf, score[k])

    q_aug = []
    for k in streams:
        bias_t = jnp.where((picked[k] > 0.5) & valid, 0.0, NEG)
        bias_t = jnp.concatenate([bias_t, jnp.zeros((LANES - nsel, tq), F32)], axis=0)
        bias_q = _t(bias_t).astype(BF16)
        q_aug.append(jnp.concatenate([q4[k], jnp.concatenate([bias_q] * GROUP_SIZE, axis=0)], axis=1))

    def chunk_off(kb):
        return pl.multiple_of(kb * tk, tk)

    def sel_scores_into(k, dst, kb):
        off = chunk_off(kb)
        keys = jnp.concatenate([ks_ref[pl.ds(off, tk), hd(k)], e_ref[pl.ds(off, tk), :]], axis=1)
        dst[k] = _nt_dot(keys, q_aug[k])

    hi = (s0 + tq - 1) // tk + 1
    for k in streams:
        reset_stats(k)
    for k in streams:
        sel_scores_into(k, sa_scr, 0)
    n_pairs = lax.shift_right_logical(hi - 1, 1)

    def pair(i, c):
        k0 = 2 * i
        for k in streams:
            sel_scores_into(k, sb_scr, k0 + 1)
        for k in streams:
            update(k, sa_scr[k], chunk_off(k0), vst_ref, None)
        for k in streams:
            sel_scores_into(k, sa_scr, k0 + 2)
        for k in streams:
            update(k, sb_scr[k], chunk_off(k0 + 1), vst_ref, None)
        return c

    lax.fori_loop(0, n_pairs, pair, 0)

    @pl.when(((hi - 1) & 1) == 1)
    def _():
        for k in streams:
            sel_scores_into(k, sb_scr, hi - 1)
        for k in streams:
            update(k, sa_scr[k], chunk_off(hi - 2), vst_ref, None)
        for k in streams:
            update(k, sb_scr[k], chunk_off(hi - 1), vst_ref, causal)

    @pl.when(((hi - 1) & 1) == 0)
    def _():
        for k in streams:
            update(k, sa_scr[k], chunk_off(hi - 1), vst_ref, causal)

    sg = _t(jax.nn.sigmoid(gate_ref[...].astype(F32)))
    for k in streams:
        o_s = result(k)
        outs = []
        for r in range(GROUP_SIZE):
            sl = slice(r * tq, (r + 1) * tq)
            g0 = k * LANES + 3 * r
            o_r = (sg[g0:g0 + 1] * o_c[k][:, sl] + sg[g0 + 1:g0 + 2] * o_s[:, sl]
                   + sg[g0 + 2:g0 + 3] * o_w[k][:, sl])
            outs.append(_t(o_r))
        o_ref[:, k * gw:(k + 1) * gw] = jnp.concatenate(outs, axis=1).astype(o_ref.dtype)


def _nsa(proj, kvc, kvct, vt, e_tab, bsz, seq):
    tq, tk = 256, 512
    assert WINDOW % tq == 0 and tk % tq == 0
    ns = NSA_STREAMS
    nq = seq // tq
    nc = kvc.shape[3]
    gw = ns * GROUP_SIZE * HEAD_DIM
    kw_ = ns * HEAD_DIM
    rows = GROUP_SIZE * tq
    n_vs_blocks = KV_DIM // kw_

    def k_spec(off):
        cb = off // kw_
        return pl.BlockSpec((seq, kw_), lambda b, g, i: (b, cb + g))

    return pl.pallas_call(
        functools.partial(_nsa_kernel, tq=tq, tk=tk, seq=seq),
        grid=(bsz, N_KV_GROUPS // ns, nq),
        in_specs=[pl.BlockSpec((tq, gw), lambda b, g, i: (b * nq + i, g)),
                  pl.BlockSpec((tq, ns * LANES), lambda b, g, i: (b * nq + i, OFF_GATE // (ns * LANES) + g)),
                  pl.BlockSpec((None, None, ns, nc, HEAD_DIM), lambda b, g, i: (0, b, g, 0, 0)),
                  pl.BlockSpec((None, None, ns, HEAD_DIM, nc), lambda b, g, i: (1, b, g, 0, 0)),
                  k_spec(OFF_KS),
                  pl.BlockSpec((seq, LANES), lambda b, g, i: (0, 0)),
                  pl.BlockSpec((kw_, seq), lambda b, g, i: (g, b)),
                  k_spec(OFF_KW),
                  pl.BlockSpec((kw_, seq), lambda b, g, i: (n_vs_blocks + g, b))],
        out_specs=pl.BlockSpec((tq, gw), lambda b, g, i: (b * nq + i, g)),
        out_shape=jax.ShapeDtypeStruct((bsz * seq, D_MODEL), BF16),
        scratch_shapes=[pltpu.VMEM((ns, tk, rows), F32), pltpu.VMEM((ns, tk, rows), F32),
                        pltpu.VMEM((ns, 1, rows), F32),
                        pltpu.VMEM((ns, HEAD_DIM + ONES_ROWS, rows), F32)],
        compiler_params=_cparams(("parallel", "parallel", "arbitrary")),
        name="nsa",
    )(proj, proj, kvc, kvct, proj, e_tab, vt, proj, vt)


def _s5_kernel(u_ref, pg_ref, bt_ref, pl_ref, dsk_ref, o_ref, wt_scr, ws_scr, wo_scr, xp_scr, tok_scr,
               *, chunks_per_seq):
    L = SSM_CHUNK
    hp = lax.Precision.HIGHEST
    are_g, aim_g, ldt_g = pg_ref[0], pg_ref[1], pg_ref[2]
    bt = bt_ref[...]
    are_l, aim_l, ldt_l, cr_l, ci_l = pl_ref[0], pl_ref[1], pl_ref[2], pl_ref[3], pl_ref[4]

    lane = lax.broadcasted_iota(jnp.int32, (LANES, LANES), 1)
    rowi = lax.broadcasted_iota(jnp.int32, (LANES, LANES), 0)
    is_re = lane < SSM_STATE
    row_grp = rowi // SSM_GROUP
    lane_grp = lane // SSM_GROUP

    def rows16(z):
        return jnp.concatenate([jnp.broadcast_to(z[g:g + 1], (SSM_GROUP, LANES))
                                for g in range(GROUPS_PER_BLOCK)], axis=0)

    def cmul(x, zr, zi):
        return x * zr + pltpu.roll(x, SSM_STATE, axis=1) * jnp.where(is_re, -zi, zi)

    dt_g = jnp.exp(ldt_g)
    l1r = jnp.exp(are_g * dt_g) * jnp.cos(aim_g * dt_g)
    l1i = jnp.exp(are_g * dt_g) * jnp.sin(aim_g * dt_g)
    den = are_g * are_g + aim_g * aim_g
    nr = l1r - 1.0
    coef_r = (nr * are_g + l1i * aim_g) / den
    coef_i = (l1i * are_g - nr * aim_g) / den
    bbar = cmul(bt, rows16(coef_r), rows16(coef_i))

    pw_g = [(jnp.ones_like(l1r), jnp.zeros_like(l1r))]
    for _ in range(L):
        pr, pi = pw_g[-1]
        pw_g.append((pr * l1r - pi * l1i, pr * l1i + pi * l1r))

    dt_l = jnp.exp(ldt_l)
    m1r = jnp.exp(are_l * dt_l) * jnp.cos(aim_l * dt_l)
    m1i = jnp.exp(are_l * dt_l) * jnp.sin(aim_l * dt_l)
    gs = []
    pr, pi = jnp.ones_like(m1r), jnp.zeros_like(m1r)
    for _ in range(L + 1):
        gs.append(jnp.concatenate([cr_l * pr - ci_l * pi, -(cr_l * pi + ci_l * pr)], axis=0))
        pr, pi = pr * m1r - pi * m1i, pr * m1i + pi * m1r

    same_grp = row_grp == lane_grp
    zeros_tile = jnp.zeros((LANES, LANES), BF16)

    for t in range(L):
        d_t = jnp.dot(bbar, gs[t], precision=hp, preferred_element_type=F32)
        d_t = jnp.where(same_grp, d_t, 0.0).astype(BF16)
        for j in range(L - t):
            i = j + t
            wt_scr[j * LANES:(j + 1) * LANES, i * LANES:(i + 1) * LANES] = d_t
    for j in range(L):
        for i in range(j):
            wt_scr[j * LANES:(j + 1) * LANES, i * LANES:(i + 1) * LANES] = zeros_tile

    for j in range(L):
        zr, zi = pw_g[L - 1 - j]
        s_j = cmul(bbar, rows16(zr), rows16(zi))
        for g in range(GROUPS_PER_BLOCK):
            ws_scr[j * LANES:(j + 1) * LANES, g * LANES:(g + 1) * LANES] = (
                jnp.where(row_grp == g, s_j, 0.0).astype(BF16))

    for g in range(GROUPS_PER_BLOCK):
        for i in range(L):
            wo_scr[g * LANES:(g + 1) * LANES, i * LANES:(i + 1) * LANES] = (
                jnp.where(lane_grp == g, gs[i + 1], 0.0).astype(BF16))

    tok_scr[...] = u_ref[...].astype(F32)
    nrow = u_ref.shape[0] // L
    u = jnp.concatenate([tok_scr[pl.ds(j, nrow, stride=L), :].astype(BF16) for j in range(L)], axis=1)
    y = jnp.dot(u, wt_scr[...], preferred_element_type=F32)
    st = jnp.dot(u, ws_scr[...], preferred_element_type=F32)

    kk = lax.broadcasted_iota(jnp.int32, (nrow, 1), 0) & (chunks_per_seq - 1)
    lane1 = lax.broadcasted_iota(jnp.int32, (1, LANES), 1)
    for g in range(GROUPS_PER_BLOCK):
        xs = st[:, g * LANES:(g + 1) * LANES]
        zr = pw_g[L][0][g:g + 1]
        zi = pw_g[L][1][g:g + 1]
        d = 1
        while d < chunks_per_seq:
            sh = jnp.where(kk >= d, pltpu.roll(xs, d, axis=0), 0.0)
            zmix = jnp.where(lane1 < SSM_STATE, -zi, zi)
            xs = xs + sh * zr + pltpu.roll(sh, SSM_STATE, axis=1) * zmix
            zr, zi = zr * zr - zi * zi, 2.0 * zr * zi
            d *= 2
        prev = jnp.where(kk >= 1, pltpu.roll(xs, 1, axis=0), 0.0)
        xp_scr[:, g * LANES:(g + 1) * LANES] = prev.astype(BF16)

    y = y + jnp.dot(xp_scr[...], wo_scr[...], preferred_element_type=F32)
    y = jax.nn.gelu(y + dsk_ref[...] * u.astype(F32))
    for i in range(L):
        tok_scr[pl.ds(i, nrow, stride=L), :] = y[:, i * LANES:(i + 1) * LANES]
    o_ref[...] = tok_scr[...].astype(o_ref.dtype)


def _s5(proj, p_grp, p_bt, p_lanes, dsk, chunks_per_seq):
    m = proj.shape[0]
    nrow, kk = m // SSM_CHUNK, SSM_CHUNK * LANES
    nst = GROUPS_PER_BLOCK * LANES
    return pl.pallas_call(
        functools.partial(_s5_kernel, chunks_per_seq=chunks_per_seq),
        grid=(N_SSM_BLOCKS,),
        in_specs=[pl.BlockSpec((m, LANES), lambda i: (0, OFF_SSM // LANES + i)),
                  pl.BlockSpec((None, 3, GROUPS_PER_BLOCK, LANES), lambda i: (i, 0, 0, 0)),
                  pl.BlockSpec((None, LANES, LANES), lambda i: (i, 0, 0)),
                  pl.BlockSpec((None, 5, SSM_STATE, LANES), lambda i: (i, 0, 0, 0)),
                  pl.BlockSpec((None, 1, kk), lambda i: (i, 0, 0))],
        out_specs=pl.BlockSpec((m, LANES), lambda i: (0, i)),
        out_shape=jax.ShapeDtypeStruct((m, D_MODEL), BF16),
        scratch_shapes=[pltpu.VMEM((kk, kk), BF16), pltpu.VMEM((kk, nst), BF16),
                        pltpu.VMEM((nst, kk), BF16), pltpu.VMEM((nrow, nst), BF16),
                        pltpu.VMEM((m, LANES), F32)],
        compiler_params=_cparams(("parallel",)),
        name="s5",
    )(proj, p_grp, p_bt, p_lanes, dsk)


def _mix_kernel(ys_ref, ya_ref, ga_ref, gb_ref, x_ref, wg_ref, bg_ref, wo_ref, gt_ref, gm_ref, sc_ref, sh_ref,
                h_ref, u_ref):
    ys = ys_ref[...]
    z = jnp.dot(ys, wg_ref[...], preferred_element_type=F32) + bg_ref[...]
    yb = ys.astype(F32) * jax.nn.sigmoid(z)
    mix = (jax.nn.sigmoid(ga_ref[...].astype(F32)) * ya_ref[...].astype(F32)
           + jax.nn.sigmoid(gb_ref[...].astype(F32)) * yb)
    h = x_ref[...] + gt_ref[...] * jnp.dot(mix.astype(BF16), wo_ref[...], preferred_element_type=F32)
    h_ref[...] = h
    ms = jnp.mean(h * h, axis=-1, keepdims=True)
    y = h * lax.rsqrt(ms + EPS) * gm_ref[...]
    u_ref[...] = (y * (1.0 + sc_ref[...]) + sh_ref[...]).astype(u_ref.dtype)


def _mix(ys, ya, proj, x2, w_glu, b_glu, w_out, g_mlp, ada3, seq):
    m = ys.shape[0]
    tm = 256
    per_b = seq // tm
    const = dict(pipeline_mode=pl.Buffered(1))
    row = lambda cb: pl.BlockSpec((tm, D_MODEL), lambda i: (i, cb))

    def ada_spec(k):
        return pl.BlockSpec((None, 1, D_MODEL), lambda i: ((i // per_b) * 6 + k, 0, 0))

    return pl.pallas_call(
        _mix_kernel,
        grid=(m // tm,),
        in_specs=[row(0), row(0), row(OFF_GA // D_MODEL), row(OFF_GB // D_MODEL), row(0),
                  pl.BlockSpec((D_MODEL, D_MODEL), lambda i: (0, 0), **const),
                  pl.BlockSpec((1, D_MODEL), lambda i: (0, 0)),
                  pl.BlockSpec((D_MODEL, D_MODEL), lambda i: (0, 0), **const),
                  ada_spec(2),
                  pl.BlockSpec((1, D_MODEL), lambda i: (0, 0)),
                  ada_spec(4), ada_spec(3)],
        out_specs=[row(0), row(0)],
        out_shape=[jax.ShapeDtypeStruct((m, D_MODEL), F32), jax.ShapeDtypeStruct((m, D_MODEL), BF16)],
        compiler_params=_cparams(("parallel",)),
        name="mix",
    )(ys, ya, proj, proj, x2, w_glu, b_glu, w_out, ada3, g_mlp, ada3, ada3)


def _mlp_kernel(h_ref, u_ref, gt_ref, wu_ref, wd_ref, gf_ref, o_ref, acc_scr):
    f = pl.program_id(1)

    @pl.when(f == 0)
    def _():
        acc_scr[...] = jnp.zeros_like(acc_scr)

    a = jnp.dot(u_ref[...], wu_ref[...], preferred_element_type=F32)
    a = jnp.square(jnp.maximum(a, 0.0))
    acc_scr[...] += jnp.dot(a.astype(BF16), wd_ref[...], preferred_element_type=F32)

    @pl.when(f == pl.num_programs(1) - 1)
    def _():
        h2 = h_ref[...] + gt_ref[...] * acc_scr[...]
        ms = jnp.mean(h2 * h2, axis=-1, keepdims=True)
        o_ref[...] = h2 * lax.rsqrt(ms + EPS) * gf_ref[...]


def _mlp(h1, u2, ada3, w_up, w_down, g_final, seq):
    m = h1.shape[0]
    tm, tf = 512, 1024
    per_b = seq // tm
    ff = w_up.shape[1]
    return pl.pallas_call(
        _mlp_kernel,
        grid=(m // tm, ff // tf),
        in_specs=[pl.BlockSpec((tm, D_MODEL), lambda i, f: (i, 0)),
                  pl.BlockSpec((tm, D_MODEL), lambda i, f: (i, 0)),
                  pl.BlockSpec((None, 1, D_MODEL), lambda i, f: ((i // per_b) * 6 + 5, 0, 0)),
                  pl.BlockSpec((D_MODEL, tf), lambda i, f: (0, f)),
                  pl.BlockSpec((tf, D_MODEL), lambda i, f: (f, 0)),
                  pl.BlockSpec((1, D_MODEL), lambda i, f: (0, 0))],
        out_specs=pl.BlockSpec((tm, D_MODEL), lambda i, f: (i, 0)),
        out_shape=jax.ShapeDtypeStruct((m, D_MODEL), F32),
        scratch_shapes=[pltpu.VMEM((tm, D_MODEL), F32)],
        compiler_params=_cparams(("parallel", "arbitrary")),
        name="mlp",
    )(h1, u2, ada3, w_up, w_down, g_final)


def _gate_rows(wt):
    wg = wt[RAW_GATE:RAW_SSM].reshape(N_KV_GROUPS, 3 * GROUP_SIZE, D_MODEL)
    wg = jnp.pad(wg, ((0, 0), (0, LANES - 3 * GROUP_SIZE), (0, 0))).reshape(N_KV_GROUPS * LANES, D_MODEL)
    return wg.astype(BF16)


def _s5_params(a_re, a_im, log_dt, b_re, b_im, c_re, c_im, d_skip):
    nb, gp, p, cg = N_SSM_BLOCKS, GROUPS_PER_BLOCK, SSM_STATE, SSM_GROUP
    ldt = jnp.broadcast_to(log_dt[:, None], (SSM_NGROUPS, p))

    def grp(a):
        return jnp.concatenate([a, a], axis=-1).reshape(nb, gp, LANES)

    def lanes(a):
        a = jnp.broadcast_to(a.reshape(nb, gp, p, 1).transpose(0, 2, 1, 3), (nb, p, gp, cg))
        return a.reshape(nb, p, LANES)

    def ct(cm):
        return cm.reshape(nb, gp, cg, p).transpose(0, 3, 1, 2).reshape(nb, p, LANES)

    p_grp = jnp.stack([grp(a_re), grp(a_im), grp(ldt)], axis=1)
    p_bt = jnp.stack([b_re, b_im], axis=1).transpose(0, 3, 1, 2).reshape(nb, LANES, LANES)
    p_lanes = jnp.stack([lanes(a_re), lanes(a_im), lanes(ldt), ct(c_re), ct(c_im)], axis=1)
    dsk = jnp.tile(d_skip.reshape(nb, 1, LANES), (1, 1, SSM_CHUNK))
    return p_grp, p_bt, p_lanes, dsk


def kernel(x, c, w_ada, b_ada, g_mix, w_in, w_ck1, w_ck2, pe_ck, w_cv1, w_cv2, pe_cv,
           a_re, a_im, log_dt, b_re, b_im, c_re, c_im, d_skip, w_glu, b_glu,
           w_out, g_mlp, w_up, w_down, g_final):
    bsz, seq, _ = x.shape
    assert w_ada.shape[0] == 1, "single-layer block"
    assert seq % 1024 == 0 and (seq & (seq - 1)) == 0
    m = bsz * seq
    x2 = x.reshape(m, D_MODEL)

    c_pad = jnp.pad(c, ((0, 8 - bsz), (0, 0)))
    ada = _ada(c_pad, w_ada[0], b_ada)
    ada3 = ada[:bsz].reshape(bsz * 6, 1, D_MODEL)

    w_in_t = w_in[0].T
    proj, vt = _inproj(x2, g_mix, ada3, w_in_t, _gate_rows(w_in_t), seq)

    w1s = jnp.stack([w_ck1[0], w_cv1[0]]).reshape(2, CMP_LEN * HEAD_DIM, HEAD_DIM).astype(BF16)
    pes = jnp.stack([pe_ck[0], pe_cv[0]]).reshape(2, 1, CMP_LEN * HEAD_DIM)
    pes = jnp.pad(pes, ((0, 0), (0, 7), (0, 0))).astype(BF16)
    w2s = jnp.stack([w_ck2[0], w_cv2[0]]).astype(BF16)
    kvc, kvct = _compress(proj, w1s, pes, w2s, bsz, seq)

    e_tab = (jnp.arange(seq)[:, None] // SEL_BLOCK == jnp.arange(LANES)[None, :]).astype(BF16)
    y_a = _nsa(proj, kvc, kvct, vt, e_tab, bsz, seq)

    p_grp, p_bt, p_lanes, dsk = _s5_params(a_re[0], a_im[0], log_dt[0], b_re[0], b_im[0],
                                           c_re[0], c_im[0], d_skip[0])
    ys = _s5(proj, p_grp, p_bt, p_lanes, dsk, seq // SSM_CHUNK)

    h1, u2 = _mix(ys, y_a, proj, x2, w_glu[0].astype(BF16), b_glu, w_out[0].astype(BF16), g_mlp, ada3, seq)
    out = _mlp(h1, u2, ada3, w_up[0].astype(BF16), w_down[0].astype(BF16), g_final.reshape(1, D_MODEL), seq)
    return out.reshape(bsz, seq, D_MODEL)
```

```python
import functools

import jax
import jax.numpy as jnp
from jax import lax
from jax.experimental import pallas as pl
from jax.experimental.pallas import tpu as pltpu

F32 = jnp.float32
BF16 = jnp.bfloat16

D_MODEL = 2048
N_HEADS = 16
HEAD_DIM = 128
N_KV_GROUPS = 4
GROUP_SIZE = 4
KV_DIM = N_KV_GROUPS * HEAD_DIM
CMP_LEN = 32
CMP_STRIDE = 16
SEL_BLOCK = 64
SEL_TOPN = 16
WINDOW = 512
SSM_GROUP = 16
SSM_NGROUPS = D_MODEL // SSM_GROUP
SSM_STATE = 64
EPS = 1e-6
NEG = -1e30
BIG = 1e6
LOG2E = 1.4426950408889634
ONES_ROWS = 16

LANES = 128
SSM_CHUNK = 16
GROUPS_PER_BLOCK = LANES // SSM_GROUP
N_SSM_BLOCKS = SSM_NGROUPS // GROUPS_PER_BLOCK

OFF_Q = 0
OFF_SSM = OFF_Q + D_MODEL
OFF_GA = OFF_SSM + D_MODEL
OFF_GB = OFF_GA + D_MODEL
OFF_KC = OFF_GB + D_MODEL
OFF_VC = OFF_KC + KV_DIM
OFF_KS = OFF_VC + KV_DIM
OFF_KW = OFF_KS + KV_DIM
OFF_GATE = OFF_KW + KV_DIM
PROJ_PAD = OFF_GATE + N_KV_GROUPS * LANES
RAW_VS = D_MODEL + 3 * KV_DIM
RAW_KW = RAW_VS + KV_DIM
RAW_VW = RAW_KW + KV_DIM
RAW_GATE = RAW_VW + KV_DIM
RAW_SSM = RAW_GATE + 3 * N_HEADS

VMEM_LIMIT = 56 * 1024 * 1024


def _cparams(sem):
    return pltpu.CompilerParams(dimension_semantics=sem, vmem_limit_bytes=VMEM_LIMIT)


def _t(x):
    r, c = x.shape
    rows = []
    for j in range(c // LANES):
        rows.append(jnp.concatenate([x[i * LANES:(i + 1) * LANES, j * LANES:(j + 1) * LANES].T
                                     for i in range(r // LANES)], axis=1))
    return jnp.concatenate(rows, axis=0)


def _nt_dot(a, b):
    return lax.dot_general(a, b, (((1,), (1,)), ((), ())), preferred_element_type=F32)


def _ada_kernel(c_ref, w_ref, b_ref, o_ref):
    c = c_ref[...]
    cond = c * jax.nn.sigmoid(c)
    o_ref[...] = jnp.dot(cond.astype(BF16), w_ref[...].astype(BF16),
                         preferred_element_type=F32) + b_ref[...]


def _ada(c_pad, w_ada, b_ada):
    n = w_ada.shape[1]
    tn = 1024
    return pl.pallas_call(
        _ada_kernel,
        grid=(n // tn,),
        in_specs=[pl.BlockSpec((8, D_MODEL), lambda j: (0, 0)),
                  pl.BlockSpec((D_MODEL, tn), lambda j: (0, j)),
                  pl.BlockSpec((1, tn), lambda j: (0, j))],
        out_specs=pl.BlockSpec((8, tn), lambda j: (0, j)),
        out_shape=jax.ShapeDtypeStruct((8, n), F32),
        compiler_params=_cparams(("parallel",)),
        name="ada",
    )(c_pad, w_ada, b_ada)


INPROJ_TN = 512
_Q_TILES = D_MODEL // INPROJ_TN
_TAIL_TILES = 3 * D_MODEL // INPROJ_TN
_GATE_TILE = _Q_TILES + _TAIL_TILES + 4


_ROW_UNIT = 16


def _w_row(j):
    k = j - (_Q_TILES + _TAIL_TILES)
    kv_block = jnp.where(k < 3, 4 + k, 8)
    head = jnp.where(j < _Q_TILES, j, kv_block) * (INPROJ_TN // _ROW_UNIT)
    tail = RAW_SSM // _ROW_UNIT + (j - _Q_TILES) * (INPROJ_TN // _ROW_UNIT)
    return jnp.where((j >= _Q_TILES) & (k < 0), tail, head) * _ROW_UNIT


def _inproj_kernel(x_ref, g_ref, sc_ref, sh_ref, w_ref, wg_ref, wvs_ref, wvw_ref, o_ref, vt_ref, u_scr):
    j = pl.program_id(1)

    @pl.when(j == 0)
    def _():
        x = x_ref[...]
        ms = jnp.mean(x * x, axis=-1, keepdims=True)
        y = x * lax.rsqrt(ms + EPS) * g_ref[...]
        u = (y * (1.0 + sc_ref[...]) + sh_ref[...]).astype(BF16)
        u_scr[...] = u
        vt_ref[:KV_DIM, :] = _nt_dot(wvs_ref[...].astype(BF16), u).astype(vt_ref.dtype)
        vt_ref[KV_DIM:, :] = _nt_dot(wvw_ref[...].astype(BF16), u).astype(vt_ref.dtype)

    def emit(w):
        o_ref[...] = _nt_dot(u_scr[...], w).astype(o_ref.dtype)

    pl.when(j == _GATE_TILE)(lambda: emit(wg_ref[...]))
    pl.when(j != _GATE_TILE)(lambda: emit(w_ref[...].astype(BF16)))


def _inproj(x2, g_mix, ada3, w_in_t, w_gate_t, seq):
    m = x2.shape[0]
    tm, tn = 1024, INPROJ_TN
    per_b = seq // tm
    const = dict(pipeline_mode=pl.Buffered(1))
    wblk = (pl.Element(tn), pl.Element(D_MODEL))
    return pl.pallas_call(
        _inproj_kernel,
        grid=(m // tm, PROJ_PAD // tn),
        in_specs=[pl.BlockSpec((tm, D_MODEL), lambda i, j: (i, 0)),
                  pl.BlockSpec((1, D_MODEL), lambda i, j: (0, 0)),
                  pl.BlockSpec((None, 1, D_MODEL), lambda i, j: ((i // per_b) * 6 + 1, 0, 0)),
                  pl.BlockSpec((None, 1, D_MODEL), lambda i, j: ((i // per_b) * 6 + 0, 0, 0)),
                  pl.BlockSpec(wblk, lambda i, j: (_w_row(j), 0)),
                  pl.BlockSpec((tn, D_MODEL), lambda i, j: (0, 0), **const),
                  pl.BlockSpec((tn, D_MODEL), lambda i, j: (RAW_VS // tn, 0), **const),
                  pl.BlockSpec((tn, D_MODEL), lambda i, j: (RAW_VW // tn, 0), **const)],
        out_specs=[pl.BlockSpec((tm, tn), lambda i, j: (i, j)),
                   pl.BlockSpec((2 * KV_DIM, tm), lambda i, j: (0, i))],
        out_shape=[jax.ShapeDtypeStruct((m, PROJ_PAD), BF16),
                   jax.ShapeDtypeStruct((2 * KV_DIM, m), BF16)],
        scratch_shapes=[pltpu.VMEM((tm, D_MODEL), BF16)],
        compiler_params=_cparams(("parallel", "arbitrary")),
        name="inproj",
    )(x2, g_mix, ada3, ada3, w_in_t, w_gate_t, w_in_t, w_in_t)


def _compress_kernel(x_ref, w1_ref, pe_ref, w2_ref, o_ref, ot_ref, xf_scr):
    xf_scr[...] = x_ref[...].astype(F32)
    nc = x_ref.shape[0] // CMP_STRIDE
    x = jnp.concatenate([xf_scr[pl.ds(l, nc, stride=CMP_STRIDE), :].astype(BF16)
                         for l in range(CMP_STRIDE)], axis=1)
    half = CMP_STRIDE * HEAD_DIM
    w1 = w1_ref[...]
    ha = jnp.dot(x, w1[:half], preferred_element_type=F32)
    hb = jnp.dot(x, w1[half:], preferred_element_type=F32)
    hb = pltpu.roll(hb, nc - 1, axis=0)
    pe_t = jnp.dot(pe_ref[...], w1, preferred_element_type=F32)[0:1]
    hid = jax.nn.gelu(ha + hb + pe_t)
    out = jnp.dot(hid.astype(BF16), w2_ref[...], preferred_element_type=F32)
    row = lax.broadcasted_iota(jnp.int32, out.shape, 0)
    out = jnp.where(row < nc - 1, out, 0.0)
    o_ref[...] = out.astype(o_ref.dtype)
    ot_ref[...] = _t(out).astype(ot_ref.dtype)


def _compress(proj, w1s, pes, w2s, bsz, seq):
    nc = seq // CMP_STRIDE
    g = N_KV_GROUPS
    return pl.pallas_call(
        _compress_kernel,
        grid=(2, bsz, g),
        in_specs=[pl.BlockSpec((seq, HEAD_DIM), lambda a, i, j: (i, OFF_KC // HEAD_DIM + a * g + j)),
                  pl.BlockSpec((None, CMP_LEN * HEAD_DIM, HEAD_DIM), lambda a, i, j: (a, 0, 0)),
                  pl.BlockSpec((None, 8, CMP_LEN * HEAD_DIM), lambda a, i, j: (a, 0, 0)),
                  pl.BlockSpec((None, HEAD_DIM, HEAD_DIM), lambda a, i, j: (a, 0, 0))],
        out_specs=[pl.BlockSpec((None, None, None, nc, HEAD_DIM), lambda a, i, j: (a, i, j, 0, 0)),
                   pl.BlockSpec((None, None, None, HEAD_DIM, nc), lambda a, i, j: (a, i, j, 0, 0))],
        out_shape=[jax.ShapeDtypeStruct((2, bsz, g, nc, HEAD_DIM), BF16),
                   jax.ShapeDtypeStruct((2, bsz, g, HEAD_DIM, nc), BF16)],
        scratch_shapes=[pltpu.VMEM((seq, HEAD_DIM), F32)],
        compiler_params=_cparams(("parallel", "parallel", "parallel")),
        name="compress",
    )(proj, w1s, pes, w2s)


NSA_STREAMS = 2


def _nsa_kernel(q_ref, gate_ref, kc_ref, vct_ref, ks_ref, e_ref, vst_ref, kw_ref, vwt_ref, o_ref,
                sa_scr, sb_scr, m_scr, acc_scr, *, tq, tk, seq):
    s0 = pl.program_id(2) * tq
    rows = GROUP_SIZE * tq
    gw = GROUP_SIZE * HEAD_DIM
    nsel = seq // SEL_BLOCK
    nc = kc_ref.shape[1]
    topn = min(SEL_TOPN, nsel)
    qscale = HEAD_DIM ** -0.5 * LOG2E
    streams = range(NSA_STREAMS)
    hd = lambda k: slice(k * HEAD_DIM, (k + 1) * HEAD_DIM)
    t_lane = s0 + (lax.broadcasted_iota(jnp.int32, (1, rows), 1) & (tq - 1))

    def reset_stats(k):
        m_scr[k] = jnp.full((1, rows), NEG, F32)
        acc_scr[k] = jnp.zeros((HEAD_DIM + ONES_ROWS, rows), F32)

    def update(k, s, off, vt_ref, mask):
        n = s.shape[0]
        if mask is not None:
            kpos = off + lax.broadcasted_iota(jnp.int32, (n, 1), 0)
            s = jnp.where(mask(kpos), s, NEG)
        m = m_scr[k]
        m_new = jnp.maximum(m, jnp.max(s, axis=0, keepdims=True))
        p = jnp.exp2(s - m_new).astype(BF16)
        vt1 = jnp.concatenate([vt_ref[hd(k), pl.ds(off, n)], jnp.ones((ONES_ROWS, n), BF16)], axis=0)
        acc_scr[k] = jnp.exp2(m - m_new) * acc_scr[k] + jnp.dot(vt1, p, preferred_element_type=F32)
        m_scr[k] = m_new

    def result(k):
        acc = acc_scr[k]
        return acc[:HEAD_DIM] * (1.0 / jnp.maximum(acc[HEAD_DIM:HEAD_DIM + 1], 1e-30))

    q4 = []
    for k in streams:
        q = q_ref[:, k * gw:(k + 1) * gw]
        qk = jnp.concatenate([q[:, hd(r)] for r in range(GROUP_SIZE)], axis=0)
        q4.append((qk.astype(F32) * qscale).astype(BF16))

    causal = lambda kpos: kpos <= t_lane

    for k in streams:
        reset_stats(k)
    n_back = WINDOW // tq
    win = []
    for c in range(n_back, 0, -1):
        off = s0 - c * tq
        if c == n_back:
            lo_edge = jnp.where(off >= 0, t_lane - (WINDOW - 1), seq)
        else:
            lo_edge = jnp.where(off >= 0, 0, seq)
        win.append((pl.multiple_of(jnp.maximum(off, 0), tq), lambda kpos, e=lo_edge: kpos >= e))
    win.append((pl.multiple_of(s0, tq), causal))

    def win_scores_into(k, dst, off):
        dst[k, :tq, :] = _nt_dot(kw_ref[pl.ds(off, tq), hd(k)], q4[k])

    bufs = (sa_scr, sb_scr)
    for k in streams:
        win_scores_into(k, bufs[0], win[0][0])
    for c, (off, mask) in enumerate(win):
        if c + 1 < len(win):
            for k in streams:
                win_scores_into(k, bufs[(c + 1) % 2], win[c + 1][0])
        for k in streams:
            update(k, bufs[c % 2][k, :tq, :], off, vwt_ref, mask)
    o_w = [result(k) for k in streams]

    n_col = lax.broadcasted_iota(jnp.int32, (nc, 1), 0)
    valid_c = (n_col * CMP_STRIDE + (CMP_LEN - 1)) <= t_lane
    jj = lax.broadcasted_iota(jnp.int32, (nsel, nc), 0)
    nn = lax.broadcasted_iota(jnp.int32, (nsel, nc), 1)
    ov = ((nn * CMP_STRIDE < jj * SEL_BLOCK + SEL_BLOCK)
          & (nn * CMP_STRIDE + CMP_LEN > jj * SEL_BLOCK)
          & (nn < nc - 1))
    ov = jnp.where(ov, 1.0, 0.0).astype(BF16)
    jcol = lax.broadcasted_iota(jnp.int32, (nsel, tq), 0)
    cur = lax.shift_right_logical(s0 + lax.broadcasted_iota(jnp.int32, (nsel, tq), 1), 6)
    valid = jcol <= cur
    forced = (jcol == 0) | (jcol == cur) | (jcol == cur - 1)
    o_c, score = [], []
    for k in streams:
        sc = jnp.where(valid_c, _nt_dot(kc_ref[k], q4[k]), NEG)
        mc = jnp.max(sc, axis=0, keepdims=True)
        pc = jnp.exp2(sc - mc)
        inv = 1.0 / jnp.maximum(jnp.sum(pc, axis=0, keepdims=True), 1e-30)
        pc = pc * jnp.where(t_lane >= CMP_LEN - 1, inv, 0.0)
        o_c.append(jnp.dot(vct_ref[k], pc.astype(BF16), preferred_element_type=F32))
        psum = pc[:, 0:tq] + pc[:, tq:2 * tq] + pc[:, 2 * tq:3 * tq] + pc[:, 3 * tq:4 * tq]
        p_hi = psum.astype(BF16)
        p_lo = (psum - p_hi.astype(F32)).astype(BF16)
        imp_t = (jnp.dot(ov, p_hi, preferred_element_type=F32)
                 + jnp.dot(ov, p_lo, preferred_element_type=F32))
        score.append(jnp.where(forced, BIG, jnp.where(valid, imp_t, -BIG)))

    jcol_f = jcol.astype(F32)
    picked = [jnp.zeros((nsel, tq), F32) for _ in streams]
    for _ in range(topn):
        for k in streams:
            best = jnp.max(score[k], axis=0, keepdims=True)
            first = jnp.min(jnp.where(score[k] == best, jcol_f, float(nsel)), axis=0, keepdims=True)
            hit = jcol_f == first
            picked[k] = jnp.where(hit, 1.0, picked[k])
            score[k] = jnp.where(hit, -jnp.inf, score[k])

    q_aug = []
    for k in streams:
        bias_t = jnp.where((picked[k] > 0.5) & valid, 0.0, NEG)
        bias_t = jnp.concatenate([bias_t, jnp.zeros((LANES - nsel, tq), F32)], axis=0)
        bias_q = _t(bias_t).astype(BF16)
        q_aug.append(jnp.concatenate([q4[k], jnp.concatenate([bias_q] * GROUP_SIZE, axis=0)], axis=1))

    def chunk_off(kb):
        return pl.multiple_of(kb * tk, tk)

    def sel_scores_into(k, dst, kb):
        off = chunk_off(kb)
        keys = jnp.concatenate([ks_ref[pl.ds(off, tk), hd(k)], e_ref[pl.ds(off, tk), :]], axis=1)
        dst[k] = _nt_dot(keys, q_aug[k])

    hi = (s0 + tq - 1) // tk + 1
    for k in streams:
        reset_stats(k)
    for k in streams:
        sel_scores_into(k, sa_scr, 0)
    n_pairs = lax.shift_right_logical(hi - 1, 1)

    def pair(i, c):
        k0 = 2 * i
        for k in streams:
            sel_scores_into(k, sb_scr, k0 + 1)
        for k in streams:
            update(k, sa_scr[k], chunk_off(k0), vst_ref, None)
        for k in streams:
            sel_scores_into(k, sa_scr, k0 + 2)
        for k in streams:
            update(k, sb_scr[k], chunk_off(k0 + 1), vst_ref, None)
        return c

    lax.fori_loop(0, n_pairs, pair, 0)

    @pl.when(((hi - 1) & 1) == 1)
    def _():
        for k in streams:
            sel_scores_into(k, sb_scr, hi - 1)
        for k in streams:
            update(k, sa_scr[k], chunk_off(hi - 2), vst_ref, None)
        for k in streams:
            update(k, sb_scr[k], chunk_off(hi - 1), vst_ref, causal)

    @pl.when(((hi - 1) & 1) == 0)
    def _():
        for k in streams:
            update(k, sa_scr[k], chunk_off(hi - 1), vst_ref, causal)

    sg = _t(jax.nn.sigmoid(gate_ref[...].astype(F32)))
    for k in streams:
        o_s = result(k)
        outs = []
        for r in range(GROUP_SIZE):
            sl = slice(r * tq, (r + 1) * tq)
            g0 = k * LANES + 3 * r
            o_r = (sg[g0:g0 + 1] * o_c[k][:, sl] + sg[g0 + 1:g0 + 2] * o_s[:, sl]
                   + sg[g0 + 2:g0 + 3] * o_w[k][:, sl])
            outs.append(_t(o_r))
        o_ref[:, k * gw:(k + 1) * gw] = jnp.concatenate(outs, axis=1).astype(o_ref.dtype)


def _nsa(proj, kvc, kvct, vt, e_tab, bsz, seq):
    tq, tk = 256, 512
    assert WINDOW % tq == 0 and tk % tq == 0
    ns = NSA_STREAMS
    nq = seq // tq
    nc = kvc.shape[3]
    gw = ns * GROUP_SIZE * HEAD_DIM
    kw_ = ns * HEAD_DIM
    rows = GROUP_SIZE * tq
    n_vs_blocks = KV_DIM // kw_

    def k_spec(off):
        cb = off // kw_
        return pl.BlockSpec((seq, kw_), lambda b, g, i: (b, cb + g))

    return pl.pallas_call(
        functools.partial(_nsa_kernel, tq=tq, tk=tk, seq=seq),
        grid=(bsz, N_KV_GROUPS // ns, nq),
        in_specs=[pl.BlockSpec((tq, gw), lambda b, g, i: (b * nq + i, g)),
                  pl.BlockSpec((tq, ns * LANES), lambda b, g, i: (b * nq + i, OFF_GATE // (ns * LANES) + g)),
                  pl.BlockSpec((None, None, ns, nc, HEAD_DIM), lambda b, g, i: (0, b, g, 0, 0)),
                  pl.BlockSpec((None, None, ns, HEAD_DIM, nc), lambda b, g, i: (1, b, g, 0, 0)),
                  k_spec(OFF_KS),
                  pl.BlockSpec((seq, LANES), lambda b, g, i: (0, 0)),
                  pl.BlockSpec((kw_, seq), lambda b, g, i: (g, b)),
                  k_spec(OFF_KW),
                  pl.BlockSpec((kw_, seq), lambda b, g, i: (n_vs_blocks + g, b))],
        out_specs=pl.BlockSpec((tq, gw), lambda b, g, i: (b * nq + i, g)),
        out_shape=jax.ShapeDtypeStruct((bsz * seq, D_MODEL), BF16),
        scratch_shapes=[pltpu.VMEM((ns, tk, rows), F32), pltpu.VMEM((ns, tk, rows), F32),
                        pltpu.VMEM((ns, 1, rows), F32),
                        pltpu.VMEM((ns, HEAD_DIM + ONES_ROWS, rows), F32)],
        compiler_params=_cparams(("parallel", "parallel", "arbitrary")),
        name="nsa",
    )(proj, proj, kvc, kvct, proj, e_tab, vt, proj, vt)


def _s5_kernel(u_ref, pg_ref, bt_ref, pl_ref, dsk_ref, o_ref, wt_scr, ws_scr, wo_scr, xp_scr, tok_scr,
               *, chunks_per_seq):
    L = SSM_CHUNK
    hp = lax.Precision.HIGHEST
    are_g, aim_g, ldt_g = pg_ref[0], pg_ref[1], pg_ref[2]
    bt = bt_ref[...]
    are_l, aim_l, ldt_l, cr_l, ci_l = pl_ref[0], pl_ref[1], pl_ref[2], pl_ref[3], pl_ref[4]

    lane = lax.broadcasted_iota(jnp.int32, (LANES, LANES), 1)
    rowi = lax.broadcasted_iota(jnp.int32, (LANES, LANES), 0)
    is_re = lane < SSM_STATE
    row_grp = rowi // SSM_GROUP
    lane_grp = lane // SSM_GROUP

    def rows16(z):
        return jnp.concatenate([jnp.broadcast_to(z[g:g + 1], (SSM_GROUP, LANES))
                                for g in range(GROUPS_PER_BLOCK)], axis=0)

    def cmul(x, zr, zi):
        return x * zr + pltpu.roll(x, SSM_STATE, axis=1) * jnp.where(is_re, -zi, zi)

    dt_g = jnp.exp(ldt_g)
    l1r = jnp.exp(are_g * dt_g) * jnp.cos(aim_g * dt_g)
    l1i = jnp.exp(are_g * dt_g) * jnp.sin(aim_g * dt_g)
    den = are_g * are_g + aim_g * aim_g
    nr = l1r - 1.0
    coef_r = (nr * are_g + l1i * aim_g) / den
    coef_i = (l1i * are_g - nr * aim_g) / den
    bbar = cmul(bt, rows16(coef_r), rows16(coef_i))

    pw_g = [(jnp.ones_like(l1r), jnp.zeros_like(l1r))]
    for _ in range(L):
        pr, pi = pw_g[-1]
        pw_g.append((pr * l1r - pi * l1i, pr * l1i + pi * l1r))

    dt_l = jnp.exp(ldt_l)
    m1r = jnp.exp(are_l * dt_l) * jnp.cos(aim_l * dt_l)
    m1i = jnp.exp(are_l * dt_l) * jnp.sin(aim_l * dt_l)
    gs = []
    pr, pi = jnp.ones_like(m1r), jnp.zeros_like(m1r)
    for _ in range(L + 1):
        gs.append(jnp.concatenate([cr_l * pr - ci_l * pi, -(cr_l * pi + ci_l * pr)], axis=0))
        pr, pi = pr * m1r - pi * m1i, pr * m1i + pi * m1r

    same_grp = row_grp == lane_grp
    zeros_tile = jnp.zeros((LANES, LANES), BF16)

    for t in range(L):
        d_t = jnp.dot(bbar, gs[t], precision=hp, preferred_element_type=F32)
        d_t = jnp.where(same_grp, d_t, 0.0).astype(BF16)
        for j in range(L - t):
            i = j + t
            wt_scr[j * LANES:(j + 1) * LANES, i * LANES:(i + 1) * LANES] = d_t
    for p2 in range(L // 2):
        wt_scr[(2 * p2 + 1) * LANES:(2 * p2 + 2) * LANES, 2 * p2 * LANES:(2 * p2 + 1) * LANES] = zeros_tile

    ws_scr[...] = jnp.zeros(ws_scr.shape, BF16)
    for j in range(L):
        zr, zi = pw_g[L - 1 - j]
        s_j = cmul(bbar, rows16(zr), rows16(zi)).astype(BF16)
        for g in range(GROUPS_PER_BLOCK):
            r0 = j * LANES + g * SSM_GROUP
            ws_scr[r0:r0 + SSM_GROUP, g * LANES:(g + 1) * LANES] = s_j[g * SSM_GROUP:(g + 1) * SSM_GROUP]

    for g in range(GROUPS_PER_BLOCK):
        for i in range(L):
            wo_scr[g * LANES:(g + 1) * LANES, i * LANES:(i + 1) * LANES] = (
                jnp.where(lane_grp == g, gs[i + 1], 0.0).astype(BF16))

    tok_scr[...] = u_ref[...].astype(F32)
    nrow = u_ref.shape[0] // L
    u = jnp.concatenate([tok_scr[pl.ds(j, nrow, stride=L), :].astype(BF16) for j in range(L)], axis=1)
    st = jnp.dot(u, ws_scr[...], preferred_element_type=F32)

    kk = lax.broadcasted_iota(jnp.int32, (nrow, 1), 0) & (chunks_per_seq - 1)
    lane1 = lax.broadcasted_iota(jnp.int32, (1, LANES), 1)
    for g in range(GROUPS_PER_BLOCK):
        xs = st[:, g * LANES:(g + 1) * LANES]
        zr = pw_g[L][0][g:g + 1]
        zi = pw_g[L][1][g:g + 1]
        d = 1
        while d < chunks_per_seq:
            sh = jnp.where(kk >= d, pltpu.roll(xs, d, axis=0), 0.0)
            zmix = jnp.where(lane1 < SSM_STATE, -zi, zi)
            xs = xs + sh * zr + pltpu.roll(sh, SSM_STATE, axis=1) * zmix
            zr, zi = zr * zr - zi * zi, 2.0 * zr * zi
            d *= 2
        prev = jnp.where(kk >= 1, pltpu.roll(xs, 1, axis=0), 0.0)
        xp_scr[:, g * LANES:(g + 1) * LANES] = prev.astype(BF16)

    xp = xp_scr[...]
    for p2 in range(L // 2):
        cols = slice(2 * p2 * LANES, (2 * p2 + 2) * LANES)
        kr = (2 * p2 + 2) * LANES
        y = (jnp.dot(u[:, :kr], wt_scr[:kr, cols], preferred_element_type=F32)
             + jnp.dot(xp, wo_scr[:, cols], preferred_element_type=F32))
        y = jax.nn.gelu(y + dsk_ref[:, cols] * u[:, cols].astype(F32))
        for i in range(2):
            tok_scr[pl.ds(2 * p2 + i, nrow, stride=L), :] = y[:, i * LANES:(i + 1) * LANES]
    o_ref[...] = tok_scr[...].astype(o_ref.dtype)


def _s5(proj, p_grp, p_bt, p_lanes, dsk, chunks_per_seq):
    m = proj.shape[0]
    nrow, kk = m // SSM_CHUNK, SSM_CHUNK * LANES
    nst = GROUPS_PER_BLOCK * LANES
    return pl.pallas_call(
        functools.partial(_s5_kernel, chunks_per_seq=chunks_per_seq),
        grid=(N_SSM_BLOCKS,),
        in_specs=[pl.BlockSpec((m, LANES), lambda i: (0, OFF_SSM // LANES + i)),
                  pl.BlockSpec((None, 3, GROUPS_PER_BLOCK, LANES), lambda i: (i, 0, 0, 0)),
                  pl.BlockSpec((None, LANES, LANES), lambda i: (i, 0, 0)),
                  pl.BlockSpec((None, 5, SSM_STATE, LANES), lambda i: (i, 0, 0, 0)),
                  pl.BlockSpec((None, 1, kk), lambda i: (i, 0, 0))],
        out_specs=pl.BlockSpec((m, LANES), lambda i: (0, i)),
        out_shape=jax.ShapeDtypeStruct((m, D_MODEL), BF16),
        scratch_shapes=[pltpu.VMEM((kk, kk), BF16), pltpu.VMEM((kk, nst), BF16),
                        pltpu.VMEM((nst, kk), BF16), pltpu.VMEM((nrow, nst), BF16),
                        pltpu.VMEM((m, LANES), F32)],
        compiler_params=_cparams(("parallel",)),
        name="s5",
    )(proj, p_grp, p_bt, p_lanes, dsk)


def _mix_kernel(ys_ref, ya_ref, ga_ref, gb_ref, x_ref, wg_ref, bg_ref, wo_ref, gt_ref, gm_ref, sc_ref, sh_ref,
                h_ref, u_ref):
    ys = ys_ref[...]
    z = jnp.dot(ys, wg_ref[...], preferred_element_type=F32) + bg_ref[...]
    yb = ys.astype(F32) * jax.nn.sigmoid(z)
    mix = (jax.nn.sigmoid(ga_ref[...].astype(F32)) * ya_ref[...].astype(F32)
           + jax.nn.sigmoid(gb_ref[...].astype(F32)) * yb)
    h = x_ref[...] + gt_ref[...] * jnp.dot(mix.astype(BF16), wo_ref[...], preferred_element_type=F32)
    h_ref[...] = h
    ms = jnp.mean(h * h, axis=-1, keepdims=True)
    y = h * lax.rsqrt(ms + EPS) * gm_ref[...]
    u_ref[...] = (y * (1.0 + sc_ref[...]) + sh_ref[...]).astype(u_ref.dtype)


def _mix(ys, ya, proj, x2, w_glu, b_glu, w_out, g_mlp, ada3, seq):
    m = ys.shape[0]
    tm = 256
    per_b = seq // tm
    const = dict(pipeline_mode=pl.Buffered(1))
    row = lambda cb: pl.BlockSpec((tm, D_MODEL), lambda i: (i, cb))

    def ada_spec(k):
        return pl.BlockSpec((None, 1, D_MODEL), lambda i: ((i // per_b) * 6 + k, 0, 0))

    return pl.pallas_call(
        _mix_kernel,
        grid=(m // tm,),
        in_specs=[row(0), row(0), row(OFF_GA // D_MODEL), row(OFF_GB // D_MODEL), row(0),
                  pl.BlockSpec((D_MODEL, D_MODEL), lambda i: (0, 0), **const),
                  pl.BlockSpec((1, D_MODEL), lambda i: (0, 0)),
                  pl.BlockSpec((D_MODEL, D_MODEL), lambda i: (0, 0), **const),
                  ada_spec(2),
                  pl.BlockSpec((1, D_MODEL), lambda i: (0, 0)),
                  ada_spec(4), ada_spec(3)],
        out_specs=[row(0), row(0)],
        out_shape=[jax.ShapeDtypeStruct((m, D_MODEL), F32), jax.ShapeDtypeStruct((m, D_MODEL), BF16)],
        compiler_params=_cparams(("parallel",)),
        name="mix",
    )(ys, ya, proj, proj, x2, w_glu, b_glu, w_out, ada3, g_mlp, ada3, ada3)


def _mlp_kernel(h_ref, u_ref, gt_ref, wu_ref, wd_ref, gf_ref, o_ref, acc_scr):
    f = pl.program_id(1)

    @pl.when(f == 0)
    def _():
        acc_scr[...] = jnp.zeros_like(acc_scr)

    a = jnp.dot(u_ref[...], wu_ref[...], preferred_element_type=F32)
    a = jnp.square(jnp.maximum(a, 0.0))
    acc_scr[...] += jnp.dot(a.astype(BF16), wd_ref[...], preferred_element_type=F32)

    @pl.when(f == pl.num_programs(1) - 1)
    def _():
        h2 = h_ref[...] + gt_ref[...] * acc_scr[...]
        ms = jnp.mean(h2 * h2, axis=-1, keepdims=True)
        o_ref[...] = h2 * lax.rsqrt(ms + EPS) * gf_ref[...]


def _mlp(h1, u2, ada3, w_up, w_down, g_final, seq):
    m = h1.shape[0]
    tm, tf = 512, 1024
    per_b = seq // tm
    ff = w_up.shape[1]
    return pl.pallas_call(
        _mlp_kernel,
        grid=(m // tm, ff // tf),
        in_specs=[pl.BlockSpec((tm, D_MODEL), lambda i, f: (i, 0)),
                  pl.BlockSpec((tm, D_MODEL), lambda i, f: (i, 0)),
                  pl.BlockSpec((None, 1, D_MODEL), lambda i, f: ((i // per_b) * 6 + 5, 0, 0)),
                  pl.BlockSpec((D_MODEL, tf), lambda i, f: (0, f)),
                  pl.BlockSpec((tf, D_MODEL), lambda i, f: (f, 0)),
                  pl.BlockSpec((1, D_MODEL), lambda i, f: (0, 0))],
        out_specs=pl.BlockSpec((tm, D_MODEL), lambda i, f: (i, 0)),
        out_shape=jax.ShapeDtypeStruct((m, D_MODEL), F32),
        scratch_shapes=[pltpu.VMEM((tm, D_MODEL), F32)],
        compiler_params=_cparams(("parallel", "arbitrary")),
        name="mlp",
    )(h1, u2, ada3, w_up, w_down, g_final)


def _gate_rows(wt):
    wg = wt[RAW_GATE:RAW_SSM].reshape(N_KV_GROUPS, 3 * GROUP_SIZE, D_MODEL)
    wg = jnp.pad(wg, ((0, 0), (0, LANES - 3 * GROUP_SIZE), (0, 0))).reshape(N_KV_GROUPS * LANES, D_MODEL)
    return wg.astype(BF16)


def _s5_params(a_re, a_im, log_dt, b_re, b_im, c_re, c_im, d_skip):
    nb, gp, p, cg = N_SSM_BLOCKS, GROUPS_PER_BLOCK, SSM_STATE, SSM_GROUP
    ldt = jnp.broadcast_to(log_dt[:, None], (SSM_NGROUPS, p))

    def grp(a):
        return jnp.concatenate([a, a], axis=-1).reshape(nb, gp, LANES)

    def lanes(a):
        a = jnp.broadcast_to(a.reshape(nb, gp, p, 1).transpose(0, 2, 1, 3), (nb, p, gp, cg))
        return a.reshape(nb, p, LANES)

    def ct(cm):
        return cm.reshape(nb, gp, cg, p).transpose(0, 3, 1, 2).reshape(nb, p, LANES)

    p_grp = jnp.stack([grp(a_re), grp(a_im), grp(ldt)], axis=1)
    p_bt = jnp.stack([b_re, b_im], axis=1).transpose(0, 3, 1, 2).reshape(nb, LANES, LANES)
    p_lanes = jnp.stack([lanes(a_re), lanes(a_im), lanes(ldt), ct(c_re), ct(c_im)], axis=1)
    dsk = jnp.tile(d_skip.reshape(nb, 1, LANES), (1, 1, SSM_CHUNK))
    return p_grp, p_bt, p_lanes, dsk


def kernel(x, c, w_ada, b_ada, g_mix, w_in, w_ck1, w_ck2, pe_ck, w_cv1, w_cv2, pe_cv,
           a_re, a_im, log_dt, b_re, b_im, c_re, c_im, d_skip, w_glu, b_glu,
           w_out, g_mlp, w_up, w_down, g_final):
    bsz, seq, _ = x.shape
    assert w_ada.shape[0] == 1, "single-layer block"
    assert seq % 1024 == 0 and (seq & (seq - 1)) == 0
    m = bsz * seq
    x2 = x.reshape(m, D_MODEL)

    c_pad = jnp.pad(c, ((0, 8 - bsz), (0, 0)))
    ada = _ada(c_pad, w_ada[0], b_ada)
    ada3 = ada[:bsz].reshape(bsz * 6, 1, D_MODEL)

    w_in_t = w_in[0].T
    proj, vt = _inproj(x2, g_mix, ada3, w_in_t, _gate_rows(w_in_t), seq)

    w1s = jnp.stack([w_ck1[0], w_cv1[0]]).reshape(2, CMP_LEN * HEAD_DIM, HEAD_DIM).astype(BF16)
    pes = jnp.stack([pe_ck[0], pe_cv[0]]).reshape(2, 1, CMP_LEN * HEAD_DIM)
    pes = jnp.pad(pes, ((0, 0), (0, 7), (0, 0))).astype(BF16)
    w2s = jnp.stack([w_ck2[0], w_cv2[0]]).astype(BF16)
    kvc, kvct = _compress(proj, w1s, pes, w2s, bsz, seq)

    e_tab = (jnp.arange(seq)[:, None] // SEL_BLOCK == jnp.arange(LANES)[None, :]).astype(BF16)
    y_a = _nsa(proj, kvc, kvct, vt, e_tab, bsz, seq)

    p_grp, p_bt, p_lanes, dsk = _s5_params(a_re[0], a_im[0], log_dt[0], b_re[0], b_im[0],
                                           c_re[0], c_im[0], d_skip[0])
    ys = _s5(proj, p_grp, p_bt, p_lanes, dsk, seq // SSM_CHUNK)

    h1, u2 = _mix(ys, y_a, proj, x2, w_glu[0].astype(BF16), b_glu, w_out[0].astype(BF16), g_mlp, ada3, seq)
    out = _mlp(h1, u2, ada3, w_up[0].astype(BF16), w_down[0].astype(BF16), g_final.reshape(1, D_MODEL), seq)
    return out.reshape(bsz, seq, D_MODEL)
```

```python
import functools

import jax
import jax.numpy as jnp
from jax import lax
from jax.experimental import pallas as pl
from jax.experimental.pallas import tpu as pltpu

F32 = jnp.float32
BF16 = jnp.bfloat16

D_MODEL = 2048
N_HEADS = 16
HEAD_DIM = 128
N_KV_GROUPS = 4
GROUP_SIZE = 4
KV_DIM = N_KV_GROUPS * HEAD_DIM
CMP_LEN = 32
CMP_STRIDE = 16
SEL_BLOCK = 64
SEL_TOPN = 16
WINDOW = 512
SSM_GROUP = 16
SSM_NGROUPS = D_MODEL // SSM_GROUP
SSM_STATE = 64
EPS = 1e-6
NEG = -1e30
BIG = 1e6
LOG2E = 1.4426950408889634
ONES_ROWS = 16

LANES = 128
SSM_CHUNK = 16
GROUPS_PER_BLOCK = LANES // SSM_GROUP
N_SSM_BLOCKS = SSM_NGROUPS // GROUPS_PER_BLOCK

OFF_Q = 0
OFF_SSM = OFF_Q + D_MODEL
OFF_GA = OFF_SSM + D_MODEL
OFF_GB = OFF_GA + D_MODEL
OFF_KC = OFF_GB + D_MODEL
OFF_VC = OFF_KC + KV_DIM
OFF_KS = OFF_VC + KV_DIM
OFF_KW = OFF_KS + KV_DIM
OFF_GATE = OFF_KW + KV_DIM
PROJ_PAD = OFF_GATE + N_KV_GROUPS * LANES
RAW_VS = D_MODEL + 3 * KV_DIM
RAW_KW = RAW_VS + KV_DIM
RAW_VW = RAW_KW + KV_DIM
RAW_GATE = RAW_VW + KV_DIM
RAW_SSM = RAW_GATE + 3 * N_HEADS

VMEM_LIMIT = 56 * 1024 * 1024


def _cparams(sem):
    return pltpu.CompilerParams(dimension_semantics=sem, vmem_limit_bytes=VMEM_LIMIT)


def _t(x):
    r, c = x.shape
    rows = []
    for j in range(c // LANES):
        rows.append(jnp.concatenate([x[i * LANES:(i + 1) * LANES, j * LANES:(j + 1) * LANES].T
                                     for i in range(r // LANES)], axis=1))
    return jnp.concatenate(rows, axis=0)


def _nt_dot(a, b):
    return lax.dot_general(a, b, (((1,), (1,)), ((), ())), preferred_element_type=F32)


def _ada_kernel(c_ref, w_ref, b_ref, o_ref):
    c = c_ref[...]
    cond = c * jax.nn.sigmoid(c)
    o_ref[...] = jnp.dot(cond.astype(BF16), w_ref[...].astype(BF16),
                         preferred_element_type=F32) + b_ref[...]


def _ada(c_pad, w_ada, b_ada):
    n = w_ada.shape[1]
    tn = 1024
    return pl.pallas_call(
        _ada_kernel,
        grid=(n // tn,),
        in_specs=[pl.BlockSpec((8, D_MODEL), lambda j: (0, 0)),
                  pl.BlockSpec((D_MODEL, tn), lambda j: (0, j)),
                  pl.BlockSpec((1, tn), lambda j: (0, j))],
        out_specs=pl.BlockSpec((8, tn), lambda j: (0, j)),
        out_shape=jax.ShapeDtypeStruct((8, n), F32),
        compiler_params=_cparams(("parallel",)),
        name="ada",
    )(c_pad, w_ada, b_ada)


INPROJ_TN = 512
_Q_TILES = D_MODEL // INPROJ_TN
_TAIL_TILES = 3 * D_MODEL // INPROJ_TN
_GATE_TILE = _Q_TILES + _TAIL_TILES + 4


_ROW_UNIT = 16


def _w_row(j):
    k = j - (_Q_TILES + _TAIL_TILES)
    kv_block = jnp.where(k < 3, 4 + k, 8)
    head = jnp.where(j < _Q_TILES, j, kv_block) * (INPROJ_TN // _ROW_UNIT)
    tail = RAW_SSM // _ROW_UNIT + (j - _Q_TILES) * (INPROJ_TN // _ROW_UNIT)
    return jnp.where((j >= _Q_TILES) & (k < 0), tail, head) * _ROW_UNIT


def _inproj_kernel(x_ref, g_ref, sc_ref, sh_ref, w_ref, wg_ref, wvs_ref, wvw_ref, o_ref, vt_ref, u_scr):
    j = pl.program_id(1)

    @pl.when(j == 0)
    def _():
        x = x_ref[...]
        ms = jnp.mean(x * x, axis=-1, keepdims=True)
        y = x * lax.rsqrt(ms + EPS) * g_ref[...]
        u = (y * (1.0 + sc_ref[...]) + sh_ref[...]).astype(BF16)
        u_scr[...] = u
        vt_ref[:KV_DIM, :] = _nt_dot(wvs_ref[...].astype(BF16), u).astype(vt_ref.dtype)
        vt_ref[KV_DIM:, :] = _nt_dot(wvw_ref[...].astype(BF16), u).astype(vt_ref.dtype)

    def emit(w):
        o_ref[...] = _nt_dot(u_scr[...], w).astype(o_ref.dtype)

    pl.when(j == _GATE_TILE)(lambda: emit(wg_ref[...]))
    pl.when(j != _GATE_TILE)(lambda: emit(w_ref[...].astype(BF16)))


def _inproj(x2, g_mix, ada3, w_in_t, w_gate_t, seq):
    m = x2.shape[0]
    tm, tn = 1024, INPROJ_TN
    per_b = seq // tm
    const = dict(pipeline_mode=pl.Buffered(1))
    wblk = (pl.Element(tn), pl.Element(D_MODEL))
    return pl.pallas_call(
        _inproj_kernel,
        grid=(m // tm, PROJ_PAD // tn),
        in_specs=[pl.BlockSpec((tm, D_MODEL), lambda i, j: (i, 0)),
                  pl.BlockSpec((1, D_MODEL), lambda i, j: (0, 0)),
                  pl.BlockSpec((None, 1, D_MODEL), lambda i, j: ((i // per_b) * 6 + 1, 0, 0)),
                  pl.BlockSpec((None, 1, D_MODEL), lambda i, j: ((i // per_b) * 6 + 0, 0, 0)),
                  pl.BlockSpec(wblk, lambda i, j: (_w_row(j), 0)),
                  pl.BlockSpec((tn, D_MODEL), lambda i, j: (0, 0), **const),
                  pl.BlockSpec((tn, D_MODEL), lambda i, j: (RAW_VS // tn, 0), **const),
                  pl.BlockSpec((tn, D_MODEL), lambda i, j: (RAW_VW // tn, 0), **const)],
        out_specs=[pl.BlockSpec((tm, tn), lambda i, j: (i, j)),
                   pl.BlockSpec((2 * KV_DIM, tm), lambda i, j: (0, i))],
        out_shape=[jax.ShapeDtypeStruct((m, PROJ_PAD), BF16),
                   jax.ShapeDtypeStruct((2 * KV_DIM, m), BF16)],
        scratch_shapes=[pltpu.VMEM((tm, D_MODEL), BF16)],
        compiler_params=_cparams(("parallel", "arbitrary")),
        name="inproj",
    )(x2, g_mix, ada3, ada3, w_in_t, w_gate_t, w_in_t, w_in_t)


def _compress_kernel(x_ref, w1_ref, pe_ref, w2_ref, o_ref, ot_ref, xf_scr):
    xf_scr[...] = x_ref[...].astype(F32)
    nc = x_ref.shape[0] // CMP_STRIDE
    x = jnp.concatenate([xf_scr[pl.ds(l, nc, stride=CMP_STRIDE), :].astype(BF16)
                         for l in range(CMP_STRIDE)], axis=1)
    half = CMP_STRIDE * HEAD_DIM
    w1 = w1_ref[...]
    ha = jnp.dot(x, w1[:half], preferred_element_type=F32)
    hb = jnp.dot(x, w1[half:], preferred_element_type=F32)
    hb = pltpu.roll(hb, nc - 1, axis=0)
    pe_t = jnp.dot(pe_ref[...], w1, preferred_element_type=F32)[0:1]
    hid = jax.nn.gelu(ha + hb + pe_t)
    out = jnp.dot(hid.astype(BF16), w2_ref[...], preferred_element_type=F32)
    row = lax.broadcasted_iota(jnp.int32, out.shape, 0)
    out = jnp.where(row < nc - 1, out, 0.0)
    o_ref[...] = out.astype(o_ref.dtype)
    ot_ref[...] = _t(out).astype(ot_ref.dtype)


def _compress(proj, w1s, pes, w2s, bsz, seq):
    nc = seq // CMP_STRIDE
    g = N_KV_GROUPS
    return pl.pallas_call(
        _compress_kernel,
        grid=(2, bsz, g),
        in_specs=[pl.BlockSpec((seq, HEAD_DIM), lambda a, i, j: (i, OFF_KC // HEAD_DIM + a * g + j)),
                  pl.BlockSpec((None, CMP_LEN * HEAD_DIM, HEAD_DIM), lambda a, i, j: (a, 0, 0)),
                  pl.BlockSpec((None, 8, CMP_LEN * HEAD_DIM), lambda a, i, j: (a, 0, 0)),
                  pl.BlockSpec((None, HEAD_DIM, HEAD_DIM), lambda a, i, j: (a, 0, 0))],
        out_specs=[pl.BlockSpec((None, None, None, nc, HEAD_DIM), lambda a, i, j: (a, i, j, 0, 0)),
                   pl.BlockSpec((None, None, None, HEAD_DIM, nc), lambda a, i, j: (a, i, j, 0, 0))],
        out_shape=[jax.ShapeDtypeStruct((2, bsz, g, nc, HEAD_DIM), BF16),
                   jax.ShapeDtypeStruct((2, bsz, g, HEAD_DIM, nc), BF16)],
        scratch_shapes=[pltpu.VMEM((seq, HEAD_DIM), F32)],
        compiler_params=_cparams(("parallel", "parallel", "parallel")),
        name="compress",
    )(proj, w1s, pes, w2s)


NSA_STREAMS = 4


def _nsa_kernel(q_ref, gate_ref, kc_ref, vct_ref, ks_ref, e_ref, vst_ref, kw_ref, vwt_ref, o_ref,
                sa_scr, sb_scr, m_scr, acc_scr, *, tq, tk, seq):
    s0 = pl.program_id(2) * tq
    rows = GROUP_SIZE * tq
    gw = GROUP_SIZE * HEAD_DIM
    nsel = seq // SEL_BLOCK
    nc = kc_ref.shape[1]
    topn = min(SEL_TOPN, nsel)
    qscale = HEAD_DIM ** -0.5 * LOG2E
    streams = range(NSA_STREAMS)
    hd = lambda k: slice(k * HEAD_DIM, (k + 1) * HEAD_DIM)
    t_lane = s0 + (lax.broadcasted_iota(jnp.int32, (1, rows), 1) & (tq - 1))

    def reset_stats(k):
        m_scr[k] = jnp.full((1, rows), NEG, F32)
        acc_scr[k] = jnp.zeros((HEAD_DIM + ONES_ROWS, rows), F32)

    def update(k, s, off, vt_ref, mask):
        n = s.shape[0]
        if mask is not None:
            kpos = off + lax.broadcasted_iota(jnp.int32, (n, 1), 0)
            s = jnp.where(mask(kpos), s, NEG)
        m = m_scr[k]
        m_new = jnp.maximum(m, jnp.max(s, axis=0, keepdims=True))
        p = jnp.exp2(s - m_new).astype(BF16)
        vt1 = jnp.concatenate([vt_ref[hd(k), pl.ds(off, n)], jnp.ones((ONES_ROWS, n), BF16)], axis=0)
        acc_scr[k] = jnp.exp2(m - m_new) * acc_scr[k] + jnp.dot(vt1, p, preferred_element_type=F32)
        m_scr[k] = m_new

    def result(k):
        acc = acc_scr[k]
        return acc[:HEAD_DIM] * (1.0 / jnp.maximum(acc[HEAD_DIM:HEAD_DIM + 1], 1e-30))

    q4 = []
    for k in streams:
        q = q_ref[:, k * gw:(k + 1) * gw]
        qk = jnp.concatenate([q[:, hd(r)] for r in range(GROUP_SIZE)], axis=0)
        q4.append((qk.astype(F32) * qscale).astype(BF16))

    causal = lambda kpos: kpos <= t_lane

    for k in streams:
        reset_stats(k)
    n_back = WINDOW // tq
    win = []
    for c in range(n_back, 0, -1):
        off = s0 - c * tq
        if c == n_back:
            lo_edge = jnp.where(off >= 0, t_lane - (WINDOW - 1), seq)
        else:
            lo_edge = jnp.where(off >= 0, 0, seq)
        win.append((pl.multiple_of(jnp.maximum(off, 0), tq), lambda kpos, e=lo_edge: kpos >= e))
    win.append((pl.multiple_of(s0, tq), causal))

    def win_scores_into(k, dst, off):
        dst[k, :tq, :] = _nt_dot(kw_ref[pl.ds(off, tq), hd(k)], q4[k])

    bufs = (sa_scr, sb_scr)
    for k in streams:
        win_scores_into(k, bufs[0], win[0][0])
    for c, (off, mask) in enumerate(win):
        if c + 1 < len(win):
            for k in streams:
                win_scores_into(k, bufs[(c + 1) % 2], win[c + 1][0])
        for k in streams:
            update(k, bufs[c % 2][k, :tq, :], off, vwt_ref, mask)
    o_w = [result(k) for k in streams]

    n_col = lax.broadcasted_iota(jnp.int32, (nc, 1), 0)
    valid_c = (n_col * CMP_STRIDE + (CMP_LEN - 1)) <= t_lane
    jj = lax.broadcasted_iota(jnp.int32, (nsel, nc), 0)
    nn = lax.broadcasted_iota(jnp.int32, (nsel, nc), 1)
    ov = ((nn * CMP_STRIDE < jj * SEL_BLOCK + SEL_BLOCK)
          & (nn * CMP_STRIDE + CMP_LEN > jj * SEL_BLOCK)
          & (nn < nc - 1))
    ov = jnp.where(ov, 1.0, 0.0).astype(BF16)
    jcol = lax.broadcasted_iota(jnp.int32, (nsel, tq), 0)
    cur = lax.shift_right_logical(s0 + lax.broadcasted_iota(jnp.int32, (nsel, tq), 1), 6)
    valid = jcol <= cur
    forced = (jcol == 0) | (jcol == cur) | (jcol == cur - 1)
    o_c, score = [], []
    for k in streams:
        sc = jnp.where(valid_c, _nt_dot(kc_ref[k], q4[k]), NEG)
        mc = jnp.max(sc, axis=0, keepdims=True)
        pc = jnp.exp2(sc - mc)
        inv = 1.0 / jnp.maximum(jnp.sum(pc, axis=0, keepdims=True), 1e-30)
        pc = pc * jnp.where(t_lane >= CMP_LEN - 1, inv, 0.0)
        o_c.append(jnp.dot(vct_ref[k], pc.astype(BF16), preferred_element_type=F32))
        psum = pc[:, 0:tq] + pc[:, tq:2 * tq] + pc[:, 2 * tq:3 * tq] + pc[:, 3 * tq:4 * tq]
        p_hi = psum.astype(BF16)
        p_lo = (psum - p_hi.astype(F32)).astype(BF16)
        imp_t = (jnp.dot(ov, p_hi, preferred_element_type=F32)
                 + jnp.dot(ov, p_lo, preferred_element_type=F32))
        score.append(jnp.where(forced, BIG, jnp.where(valid, imp_t, -BIG)))

    jcol_f = jcol.astype(F32)
    picked = [jnp.zeros((nsel, tq), F32) for _ in streams]
    for _ in range(topn):
        for k in streams:
            best = jnp.max(score[k], axis=0, keepdims=True)
            first = jnp.min(jnp.where(score[k] == best, jcol_f, float(nsel)), axis=0, keepdims=True)
            hit = jcol_f == first
            picked[k] = jnp.where(hit, 1.0, picked[k])
            score[k] = jnp.where(hit, -jnp.inf, score[k])

    q_aug = []
    for k in streams:
        bias_t = jnp.where((picked[k] > 0.5) & valid, 0.0, NEG)
        bias_t = jnp.concatenate([bias_t, jnp.zeros((LANES - nsel, tq), F32)], axis=0)
        bias_q = _t(bias_t).astype(BF16)
        q_aug.append(jnp.concatenate([q4[k], jnp.concatenate([bias_q] * GROUP_SIZE, axis=0)], axis=1))

    def chunk_off(kb):
        return pl.multiple_of(kb * tk, tk)

    def sel_scores_into(k, dst, kb):
        off = chunk_off(kb)
        keys = jnp.concatenate([ks_ref[pl.ds(off, tk), hd(k)], e_ref[pl.ds(off, tk), :]], axis=1)
        dst[k] = _nt_dot(keys, q_aug[k])

    hi = (s0 + tq - 1) // tk + 1
    for k in streams:
        reset_stats(k)
    for k in streams:
        sel_scores_into(k, sa_scr, 0)
    n_pairs = lax.shift_right_logical(hi - 1, 1)

    def pair(i, c):
        k0 = 2 * i
        for k in streams:
            sel_scores_into(k, sb_scr, k0 + 1)
        for k in streams:
            update(k, sa_scr[k], chunk_off(k0), vst_ref, None)
        for k in streams:
            sel_scores_into(k, sa_scr, k0 + 2)
        for k in streams:
            update(k, sb_scr[k], chunk_off(k0 + 1), vst_ref, None)
        return c

    lax.fori_loop(0, n_pairs, pair, 0)

    @pl.when(((hi - 1) & 1) == 1)
    def _():
        for k in streams:
            sel_scores_into(k, sb_scr, hi - 1)
        for k in streams:
            update(k, sa_scr[k], chunk_off(hi - 2), vst_ref, None)
        for k in streams:
            update(k, sb_scr[k], chunk_off(hi - 1), vst_ref, causal)

    @pl.when(((hi - 1) & 1) == 0)
    def _():
        for k in streams:
            update(k, sa_scr[k], chunk_off(hi - 1), vst_ref, causal)

    sg = _t(jax.nn.sigmoid(gate_ref[...].astype(F32)))
    for k in streams:
        o_s = result(k)
        outs = []
        for r in range(GROUP_SIZE):
            sl = slice(r * tq, (r + 1) * tq)
            g0 = k * LANES + 3 * r
            o_r = (sg[g0:g0 + 1] * o_c[k][:, sl] + sg[g0 + 1:g0 + 2] * o_s[:, sl]
                   + sg[g0 + 2:g0 + 3] * o_w[k][:, sl])
            outs.append(_t(o_r))
        o_ref[:, k * gw:(k + 1) * gw] = jnp.concatenate(outs, axis=1).astype(o_ref.dtype)


def _nsa(proj, kvc, kvct, vt, e_tab, bsz, seq):
    tq, tk = 256, 512
    assert WINDOW % tq == 0 and tk % tq == 0
    ns = NSA_STREAMS
    nq = seq // tq
    nc = kvc.shape[3]
    gw = ns * GROUP_SIZE * HEAD_DIM
    kw_ = ns * HEAD_DIM
    rows = GROUP_SIZE * tq
    n_vs_blocks = KV_DIM // kw_

    once = dict(pipeline_mode=pl.Buffered(1))

    def k_spec(off):
        cb = off // kw_
        return pl.BlockSpec((seq, kw_), lambda b, g, i: (b, cb + g), **once)

    return pl.pallas_call(
        functools.partial(_nsa_kernel, tq=tq, tk=tk, seq=seq),
        grid=(bsz, N_KV_GROUPS // ns, nq),
        in_specs=[pl.BlockSpec((tq, gw), lambda b, g, i: (b * nq + i, g)),
                  pl.BlockSpec((tq, ns * LANES), lambda b, g, i: (b * nq + i, OFF_GATE // (ns * LANES) + g)),
                  pl.BlockSpec((None, None, ns, nc, HEAD_DIM), lambda b, g, i: (0, b, g, 0, 0)),
                  pl.BlockSpec((None, None, ns, HEAD_DIM, nc), lambda b, g, i: (1, b, g, 0, 0)),
                  k_spec(OFF_KS),
                  pl.BlockSpec((seq, LANES), lambda b, g, i: (0, 0), **once),
                  pl.BlockSpec((kw_, seq), lambda b, g, i: (g, b), **once),
                  k_spec(OFF_KW),
                  pl.BlockSpec((kw_, seq), lambda b, g, i: (n_vs_blocks + g, b), **once)],
        out_specs=pl.BlockSpec((tq, gw), lambda b, g, i: (b * nq + i, g)),
        out_shape=jax.ShapeDtypeStruct((bsz * seq, D_MODEL), BF16),
        scratch_shapes=[pltpu.VMEM((ns, tk, rows), F32), pltpu.VMEM((ns, tk, rows), F32),
                        pltpu.VMEM((ns, 1, rows), F32),
                        pltpu.VMEM((ns, HEAD_DIM + ONES_ROWS, rows), F32)],
        compiler_params=_cparams(("parallel", "parallel", "arbitrary")),
        name="nsa",
    )(proj, proj, kvc, kvct, proj, e_tab, vt, proj, vt)


def _s5_kernel(u_ref, pg_ref, bt_ref, pl_ref, dsk_ref, o_ref, wt_scr, ws_scr, wo_scr, xp_scr, tok_scr, y_scr,
               *, chunks_per_seq):
    L = SSM_CHUNK
    hp = lax.Precision.HIGHEST
    are_g, aim_g, ldt_g = pg_ref[0], pg_ref[1], pg_ref[2]
    bt = bt_ref[...]
    are_l, aim_l, ldt_l, cr_l, ci_l = pl_ref[0], pl_ref[1], pl_ref[2], pl_ref[3], pl_ref[4]

    lane = lax.broadcasted_iota(jnp.int32, (LANES, LANES), 1)
    rowi = lax.broadcasted_iota(jnp.int32, (LANES, LANES), 0)
    is_re = lane < SSM_STATE
    row_grp = rowi // SSM_GROUP
    lane_grp = lane // SSM_GROUP

    def rows16(z):
        return jnp.concatenate([jnp.broadcast_to(z[g:g + 1], (SSM_GROUP, LANES))
                                for g in range(GROUPS_PER_BLOCK)], axis=0)

    def cmul(x, zr, zi):
        return x * zr + pltpu.roll(x, SSM_STATE, axis=1) * jnp.where(is_re, -zi, zi)

    dt_g = jnp.exp(ldt_g)
    l1r = jnp.exp(are_g * dt_g) * jnp.cos(aim_g * dt_g)
    l1i = jnp.exp(are_g * dt_g) * jnp.sin(aim_g * dt_g)
    den = are_g * are_g + aim_g * aim_g
    nr = l1r - 1.0
    coef_r = (nr * are_g + l1i * aim_g) / den
    coef_i = (l1i * are_g - nr * aim_g) / den
    bbar = cmul(bt, rows16(coef_r), rows16(coef_i))

    pw_g = [(jnp.ones_like(l1r), jnp.zeros_like(l1r))]
    for _ in range(L):
        pr, pi = pw_g[-1]
        pw_g.append((pr * l1r - pi * l1i, pr * l1i + pi * l1r))

    dt_l = jnp.exp(ldt_l)
    m1r = jnp.exp(are_l * dt_l) * jnp.cos(aim_l * dt_l)
    m1i = jnp.exp(are_l * dt_l) * jnp.sin(aim_l * dt_l)
    gs = []
    pr, pi = jnp.ones_like(m1r), jnp.zeros_like(m1r)
    for _ in range(L + 1):
        gs.append(jnp.concatenate([cr_l * pr - ci_l * pi, -(cr_l * pi + ci_l * pr)], axis=0))
        pr, pi = pr * m1r - pi * m1i, pr * m1i + pi * m1r

    same_grp = row_grp == lane_grp
    zeros_tile = jnp.zeros((LANES, LANES), BF16)

    for t in range(L):
        d_t = jnp.dot(bbar, gs[t], precision=hp, preferred_element_type=F32)
        d_t = jnp.where(same_grp, d_t, 0.0).astype(BF16)
        for j in range(L - t):
            i = j + t
            wt_scr[j * LANES:(j + 1) * LANES, i * LANES:(i + 1) * LANES] = d_t
    for p2 in range(L // 2):
        wt_scr[(2 * p2 + 1) * LANES:(2 * p2 + 2) * LANES, 2 * p2 * LANES:(2 * p2 + 1) * LANES] = zeros_tile

    ws_scr[...] = jnp.zeros(ws_scr.shape, BF16)
    for j in range(L):
        zr, zi = pw_g[L - 1 - j]
        s_j = cmul(bbar, rows16(zr), rows16(zi)).astype(BF16)
        for g in range(GROUPS_PER_BLOCK):
            r0 = j * LANES + g * SSM_GROUP
            ws_scr[r0:r0 + SSM_GROUP, g * LANES:(g + 1) * LANES] = s_j[g * SSM_GROUP:(g + 1) * SSM_GROUP]

    for g in range(GROUPS_PER_BLOCK):
        for i in range(L):
            wo_scr[g * LANES:(g + 1) * LANES, i * LANES:(i + 1) * LANES] = (
                jnp.where(lane_grp == g, gs[i + 1], 0.0).astype(BF16))

    tok_scr[...] = u_ref[...].astype(F32)
    nrow = u_ref.shape[0] // L
    u = jnp.concatenate([tok_scr[pl.ds(j, nrow, stride=L), :].astype(BF16) for j in range(L)], axis=1)
    st = jnp.dot(u, ws_scr[...], preferred_element_type=F32)

    kk = lax.broadcasted_iota(jnp.int32, (nrow, 1), 0) & (chunks_per_seq - 1)
    lane1 = lax.broadcasted_iota(jnp.int32, (1, LANES), 1)
    assert L // 2 == GROUPS_PER_BLOCK
    for g in range(GROUPS_PER_BLOCK):
        cols = slice(2 * g * LANES, (2 * g + 2) * LANES)
        kr = (2 * g + 2) * LANES
        y_scr[:, cols] = jnp.dot(u[:, :kr], wt_scr[:kr, cols], preferred_element_type=F32)

        xs = st[:, g * LANES:(g + 1) * LANES]
        zr = pw_g[L][0][g:g + 1]
        zi = pw_g[L][1][g:g + 1]
        d = 1
        while d < chunks_per_seq:
            sh = jnp.where(kk >= d, pltpu.roll(xs, d, axis=0), 0.0)
            zmix = jnp.where(lane1 < SSM_STATE, -zi, zi)
            xs = xs + sh * zr + pltpu.roll(sh, SSM_STATE, axis=1) * zmix
            zr, zi = zr * zr - zi * zi, 2.0 * zr * zi
            d *= 2
        prev = jnp.where(kk >= 1, pltpu.roll(xs, 1, axis=0), 0.0)
        xp_scr[:, g * LANES:(g + 1) * LANES] = prev.astype(BF16)

    xp = xp_scr[...]
    for p2 in range(L // 2):
        cols = slice(2 * p2 * LANES, (2 * p2 + 2) * LANES)
        y = y_scr[:, cols] + jnp.dot(xp, wo_scr[:, cols], preferred_element_type=F32)
        y = jax.nn.gelu(y + dsk_ref[:, cols] * u[:, cols].astype(F32))
        for i in range(2):
            tok_scr[pl.ds(2 * p2 + i, nrow, stride=L), :] = y[:, i * LANES:(i + 1) * LANES]
    o_ref[...] = tok_scr[...].astype(o_ref.dtype)


def _s5(proj, p_grp, p_bt, p_lanes, dsk, chunks_per_seq):
    m = proj.shape[0]
    nrow, kk = m // SSM_CHUNK, SSM_CHUNK * LANES
    nst = GROUPS_PER_BLOCK * LANES
    return pl.pallas_call(
        functools.partial(_s5_kernel, chunks_per_seq=chunks_per_seq),
        grid=(N_SSM_BLOCKS,),
        in_specs=[pl.BlockSpec((m, LANES), lambda i: (0, OFF_SSM // LANES + i)),
                  pl.BlockSpec((None, 3, GROUPS_PER_BLOCK, LANES), lambda i: (i, 0, 0, 0)),
                  pl.BlockSpec((None, LANES, LANES), lambda i: (i, 0, 0)),
                  pl.BlockSpec((None, 5, SSM_STATE, LANES), lambda i: (i, 0, 0, 0)),
                  pl.BlockSpec((None, 1, kk), lambda i: (i, 0, 0))],
        out_specs=pl.BlockSpec((m, LANES), lambda i: (0, i)),
        out_shape=jax.ShapeDtypeStruct((m, D_MODEL), BF16),
        scratch_shapes=[pltpu.VMEM((kk, kk), BF16), pltpu.VMEM((kk, nst), BF16),
                        pltpu.VMEM((nst, kk), BF16), pltpu.VMEM((nrow, nst), BF16),
                        pltpu.VMEM((m, LANES), F32), pltpu.VMEM((nrow, kk), F32)],
        compiler_params=_cparams(("parallel",)),
        name="s5",
    )(proj, p_grp, p_bt, p_lanes, dsk)


def _mix_kernel(ys_ref, ya_ref, ga_ref, gb_ref, x_ref, wg_ref, bg_ref, wo_ref, gt_ref, gm_ref, sc_ref, sh_ref,
                h_ref, u_ref):
    ys = ys_ref[...]
    z = jnp.dot(ys, wg_ref[...], preferred_element_type=F32) + bg_ref[...]
    yb = ys.astype(F32) * jax.nn.sigmoid(z)
    mix = (jax.nn.sigmoid(ga_ref[...].astype(F32)) * ya_ref[...].astype(F32)
           + jax.nn.sigmoid(gb_ref[...].astype(F32)) * yb)
    h = x_ref[...] + gt_ref[...] * jnp.dot(mix.astype(BF16), wo_ref[...], preferred_element_type=F32)
    h_ref[...] = h
    ms = jnp.mean(h * h, axis=-1, keepdims=True)
    y = h * lax.rsqrt(ms + EPS) * gm_ref[...]
    u_ref[...] = (y * (1.0 + sc_ref[...]) + sh_ref[...]).astype(u_ref.dtype)


def _mix(ys, ya, proj, x2, w_glu, b_glu, w_out, g_mlp, ada3, seq):
    m = ys.shape[0]
    tm = 256
    per_b = seq // tm
    const = dict(pipeline_mode=pl.Buffered(1))
    row = lambda cb: pl.BlockSpec((tm, D_MODEL), lambda i: (i, cb))

    def ada_spec(k):
        return pl.BlockSpec((None, 1, D_MODEL), lambda i: ((i // per_b) * 6 + k, 0, 0))

    return pl.pallas_call(
        _mix_kernel,
        grid=(m // tm,),
        in_specs=[row(0), row(0), row(OFF_GA // D_MODEL), row(OFF_GB // D_MODEL), row(0),
                  pl.BlockSpec((D_MODEL, D_MODEL), lambda i: (0, 0), **const),
                  pl.BlockSpec((1, D_MODEL), lambda i: (0, 0)),
                  pl.BlockSpec((D_MODEL, D_MODEL), lambda i: (0, 0), **const),
                  ada_spec(2),
                  pl.BlockSpec((1, D_MODEL), lambda i: (0, 0)),
                  ada_spec(4), ada_spec(3)],
        out_specs=[row(0), row(0)],
        out_shape=[jax.ShapeDtypeStruct((m, D_MODEL), F32), jax.ShapeDtypeStruct((m, D_MODEL), BF16)],
        compiler_params=_cparams(("parallel",)),
        name="mix",
    )(ys, ya, proj, proj, x2, w_glu, b_glu, w_out, ada3, g_mlp, ada3, ada3)


def _mlp_kernel(h_ref, u_ref, gt_ref, wu_ref, wd_ref, gf_ref, o_ref, acc_scr):
    f = pl.program_id(1)

    @pl.when(f == 0)
    def _():
        acc_scr[...] = jnp.zeros_like(acc_scr)

    a = jnp.dot(u_ref[...], wu_ref[...], preferred_element_type=F32)
    a = jnp.square(jnp.maximum(a, 0.0))
    acc_scr[...] += jnp.dot(a.astype(BF16), wd_ref[...], preferred_element_type=F32)

    @pl.when(f == pl.num_programs(1) - 1)
    def _():
        h2 = h_ref[...] + gt_ref[...] * acc_scr[...]
        ms = jnp.mean(h2 * h2, axis=-1, keepdims=True)
        o_ref[...] = h2 * lax.rsqrt(ms + EPS) * gf_ref[...]


def _mlp(h1, u2, ada3, w_up, w_down, g_final, seq):
    m = h1.shape[0]
    tm, tf = 512, 1024
    per_b = seq // tm
    ff = w_up.shape[1]
    return pl.pallas_call(
        _mlp_kernel,
        grid=(m // tm, ff // tf),
        in_specs=[pl.BlockSpec((tm, D_MODEL), lambda i, f: (i, 0)),
                  pl.BlockSpec((tm, D_MODEL), lambda i, f: (i, 0)),
                  pl.BlockSpec((None, 1, D_MODEL), lambda i, f: ((i // per_b) * 6 + 5, 0, 0)),
                  pl.BlockSpec((D_MODEL, tf), lambda i, f: (0, f)),
                  pl.BlockSpec((tf, D_MODEL), lambda i, f: (f, 0)),
                  pl.BlockSpec((1, D_MODEL), lambda i, f: (0, 0))],
        out_specs=pl.BlockSpec((tm, D_MODEL), lambda i, f: (i, 0)),
        out_shape=jax.ShapeDtypeStruct((m, D_MODEL), F32),
        scratch_shapes=[pltpu.VMEM((tm, D_MODEL), F32)],
        compiler_params=_cparams(("parallel", "arbitrary")),
        name="mlp",
    )(h1, u2, ada3, w_up, w_down, g_final)


def _gate_rows(wt):
    wg = wt[RAW_GATE:RAW_SSM].reshape(N_KV_GROUPS, 3 * GROUP_SIZE, D_MODEL)
    wg = jnp.pad(wg, ((0, 0), (0, LANES - 3 * GROUP_SIZE), (0, 0))).reshape(N_KV_GROUPS * LANES, D_MODEL)
    return wg.astype(BF16)


def _s5_params(a_re, a_im, log_dt, b_re, b_im, c_re, c_im, d_skip):
    nb, gp, p, cg = N_SSM_BLOCKS, GROUPS_PER_BLOCK, SSM_STATE, SSM_GROUP
    ldt = jnp.broadcast_to(log_dt[:, None], (SSM_NGROUPS, p))

    def grp(a):
        return jnp.concatenate([a, a], axis=-1).reshape(nb, gp, LANES)

    def lanes(a):
        a = jnp.broadcast_to(a.reshape(nb, gp, p, 1).transpose(0, 2, 1, 3), (nb, p, gp, cg))
        return a.reshape(nb, p, LANES)

    def ct(cm):
        return cm.reshape(nb, gp, cg, p).transpose(0, 3, 1, 2).reshape(nb, p, LANES)

    p_grp = jnp.stack([grp(a_re), grp(a_im), grp(ldt)], axis=1)
    p_bt = jnp.stack([b_re, b_im], axis=1).transpose(0, 3, 1, 2).reshape(nb, LANES, LANES)
    p_lanes = jnp.stack([lanes(a_re), lanes(a_im), lanes(ldt), ct(c_re), ct(c_im)], axis=1)
    dsk = jnp.tile(d_skip.reshape(nb, 1, LANES), (1, 1, SSM_CHUNK))
    return p_grp, p_bt, p_lanes, dsk


def kernel(x, c, w_ada, b_ada, g_mix, w_in, w_ck1, w_ck2, pe_ck, w_cv1, w_cv2, pe_cv,
           a_re, a_im, log_dt, b_re, b_im, c_re, c_im, d_skip, w_glu, b_glu,
           w_out, g_mlp, w_up, w_down, g_final):
    bsz, seq, _ = x.shape
    assert w_ada.shape[0] == 1, "single-layer block"
    assert seq % 1024 == 0 and (seq & (seq - 1)) == 0
    m = bsz * seq
    x2 = x.reshape(m, D_MODEL)

    c_pad = jnp.pad(c, ((0, 8 - bsz), (0, 0)))
    ada = _ada(c_pad, w_ada[0], b_ada)
    ada3 = ada[:bsz].reshape(bsz * 6, 1, D_MODEL)

    w_in_t = w_in[0].T
    proj, vt = _inproj(x2, g_mix, ada3, w_in_t, _gate_rows(w_in_t), seq)

    w1s = jnp.stack([w_ck1[0], w_cv1[0]]).reshape(2, CMP_LEN * HEAD_DIM, HEAD_DIM).astype(BF16)
    pes = jnp.stack([pe_ck[0], pe_cv[0]]).reshape(2, 1, CMP_LEN * HEAD_DIM)
    pes = jnp.pad(pes, ((0, 0), (0, 7), (0, 0))).astype(BF16)
    w2s = jnp.stack([w_ck2[0], w_cv2[0]]).astype(BF16)
    kvc, kvct = _compress(proj, w1s, pes, w2s, bsz, seq)

    e_tab = (jnp.arange(seq)[:, None] // SEL_BLOCK == jnp.arange(LANES)[None, :]).astype(BF16)
    y_a = _nsa(proj, kvc, kvct, vt, e_tab, bsz, seq)

    p_grp, p_bt, p_lanes, dsk = _s5_params(a_re[0], a_im[0], log_dt[0], b_re[0], b_im[0],
                                           c_re[0], c_im[0], d_skip[0])
    ys = _s5(proj, p_grp, p_bt, p_lanes, dsk, seq // SSM_CHUNK)

    h1, u2 = _mix(ys, y_a, proj, x2, w_glu[0].astype(BF16), b_glu, w_out[0].astype(BF16), g_mlp, ada3, seq)
    out = _mlp(h1, u2, ada3, w_up[0].astype(BF16), w_down[0].astype(BF16), g_final.reshape(1, D_MODEL), seq)
    return out.reshape(bsz, seq, D_MODEL)
```

```python
import functools

import jax
import jax.numpy as jnp
from jax import lax
from jax.experimental import pallas as pl
from jax.experimental.pallas import tpu as pltpu

F32 = jnp.float32
BF16 = jnp.bfloat16

D_MODEL = 2048
N_HEADS = 16
HEAD_DIM = 128
N_KV_GROUPS = 4
GROUP_SIZE = 4
KV_DIM = N_KV_GROUPS * HEAD_DIM
CMP_LEN = 32
CMP_STRIDE = 16
SEL_BLOCK = 64
SEL_TOPN = 16
WINDOW = 512
SSM_GROUP = 16
SSM_NGROUPS = D_MODEL // SSM_GROUP
SSM_STATE = 64
EPS = 1e-6
NEG = -1e30
BIG = 1e6
LOG2E = 1.4426950408889634
ONES_ROWS = 16

LANES = 128
SSM_CHUNK = 16
GROUPS_PER_BLOCK = LANES // SSM_GROUP
N_SSM_BLOCKS = SSM_NGROUPS // GROUPS_PER_BLOCK

OFF_Q = 0
OFF_SSM = OFF_Q + D_MODEL
OFF_GA = OFF_SSM + D_MODEL
OFF_GB = OFF_GA + D_MODEL
OFF_KC = OFF_GB + D_MODEL
OFF_VC = OFF_KC + KV_DIM
OFF_KS = OFF_VC + KV_DIM
OFF_KW = OFF_KS + KV_DIM
OFF_GATE = OFF_KW + KV_DIM
PROJ_PAD = OFF_GATE + N_KV_GROUPS * LANES
RAW_VS = D_MODEL + 3 * KV_DIM
RAW_KW = RAW_VS + KV_DIM
RAW_VW = RAW_KW + KV_DIM
RAW_GATE = RAW_VW + KV_DIM
RAW_SSM = RAW_GATE + 3 * N_HEADS

VMEM_LIMIT = 56 * 1024 * 1024


def _cparams(sem):
    return pltpu.CompilerParams(dimension_semantics=sem, vmem_limit_bytes=VMEM_LIMIT)


def _t(x):
    r, c = x.shape
    rows = []
    for j in range(c // LANES):
        rows.append(jnp.concatenate([x[i * LANES:(i + 1) * LANES, j * LANES:(j + 1) * LANES].T
                                     for i in range(r // LANES)], axis=1))
    return jnp.concatenate(rows, axis=0)


def _nt_dot(a, b):
    return lax.dot_general(a, b, (((1,), (1,)), ((), ())), preferred_element_type=F32)


def _ada_kernel(c_ref, w_ref, b_ref, o_ref):
    c = c_ref[...]
    cond = c * jax.nn.sigmoid(c)
    o_ref[...] = jnp.dot(cond.astype(BF16), w_ref[...].astype(BF16),
                         preferred_element_type=F32) + b_ref[...]


def _ada(c_pad, w_ada, b_ada):
    n = w_ada.shape[1]
    tn = 1024
    return pl.pallas_call(
        _ada_kernel,
        grid=(n // tn,),
        in_specs=[pl.BlockSpec((8, D_MODEL), lambda j: (0, 0)),
                  pl.BlockSpec((D_MODEL, tn), lambda j: (0, j)),
                  pl.BlockSpec((1, tn), lambda j: (0, j))],
        out_specs=pl.BlockSpec((8, tn), lambda j: (0, j)),
        out_shape=jax.ShapeDtypeStruct((8, n), F32),
        compiler_params=_cparams(("parallel",)),
        name="ada",
    )(c_pad, w_ada, b_ada)


INPROJ_TN = 512
_Q_TILES = D_MODEL // INPROJ_TN
_TAIL_TILES = 3 * D_MODEL // INPROJ_TN
_GATE_TILE = _Q_TILES + _TAIL_TILES + 4


_ROW_UNIT = 16


def _w_row(j):
    k = j - (_Q_TILES + _TAIL_TILES)
    kv_block = jnp.where(k < 3, 4 + k, 8)
    head = jnp.where(j < _Q_TILES, j, kv_block) * (INPROJ_TN // _ROW_UNIT)
    tail = RAW_SSM // _ROW_UNIT + (j - _Q_TILES) * (INPROJ_TN // _ROW_UNIT)
    return jnp.where((j >= _Q_TILES) & (k < 0), tail, head) * _ROW_UNIT


def _inproj_kernel(x_ref, g_ref, sc_ref, sh_ref, w_ref, wg_ref, wvs_ref, wvw_ref, o_ref, vt_ref, u_scr):
    j = pl.program_id(1)

    @pl.when(j == 0)
    def _():
        x = x_ref[...]
        ms = jnp.mean(x * x, axis=-1, keepdims=True)
        y = x * lax.rsqrt(ms + EPS) * g_ref[...]
        u = (y * (1.0 + sc_ref[...]) + sh_ref[...]).astype(BF16)
        u_scr[...] = u
        vt_ref[:KV_DIM, :] = _nt_dot(wvs_ref[...].astype(BF16), u).astype(vt_ref.dtype)
        vt_ref[KV_DIM:, :] = _nt_dot(wvw_ref[...].astype(BF16), u).astype(vt_ref.dtype)

    def emit(w):
        o_ref[...] = _nt_dot(u_scr[...], w).astype(o_ref.dtype)

    pl.when(j == _GATE_TILE)(lambda: emit(wg_ref[...]))
    pl.when(j != _GATE_TILE)(lambda: emit(w_ref[...].astype(BF16)))


def _inproj(x2, g_mix, ada3, w_in_t, w_gate_t, seq):
    m = x2.shape[0]
    tm, tn = 1024, INPROJ_TN
    per_b = seq // tm
    const = dict(pipeline_mode=pl.Buffered(1))
    wblk = (pl.Element(tn), pl.Element(D_MODEL))
    return pl.pallas_call(
        _inproj_kernel,
        grid=(m // tm, PROJ_PAD // tn),
        in_specs=[pl.BlockSpec((tm, D_MODEL), lambda i, j: (i, 0)),
                  pl.BlockSpec((1, D_MODEL), lambda i, j: (0, 0)),
                  pl.BlockSpec((None, 1, D_MODEL), lambda i, j: ((i // per_b) * 6 + 1, 0, 0)),
                  pl.BlockSpec((None, 1, D_MODEL), lambda i, j: ((i // per_b) * 6 + 0, 0, 0)),
                  pl.BlockSpec(wblk, lambda i, j: (_w_row(j), 0)),
                  pl.BlockSpec((tn, D_MODEL), lambda i, j: (0, 0), **const),
                  pl.BlockSpec((tn, D_MODEL), lambda i, j: (RAW_VS // tn, 0), **const),
                  pl.BlockSpec((tn, D_MODEL), lambda i, j: (RAW_VW // tn, 0), **const)],
        out_specs=[pl.BlockSpec((tm, tn), lambda i, j: (i, j)),
                   pl.BlockSpec((2 * KV_DIM, tm), lambda i, j: (0, i))],
        out_shape=[jax.ShapeDtypeStruct((m, PROJ_PAD), BF16),
                   jax.ShapeDtypeStruct((2 * KV_DIM, m), BF16)],
        scratch_shapes=[pltpu.VMEM((tm, D_MODEL), BF16)],
        compiler_params=_cparams(("parallel", "arbitrary")),
        name="inproj",
    )(x2, g_mix, ada3, ada3, w_in_t, w_gate_t, w_in_t, w_in_t)


def _compress_kernel(x_ref, w1_ref, pe_ref, w2_ref, o_ref, ot_ref, xf_scr):
    xf_scr[...] = x_ref[...].astype(F32)
    nc = x_ref.shape[0] // CMP_STRIDE
    x = jnp.concatenate([xf_scr[pl.ds(l, nc, stride=CMP_STRIDE), :].astype(BF16)
                         for l in range(CMP_STRIDE)], axis=1)
    half = CMP_STRIDE * HEAD_DIM
    w1 = w1_ref[...]
    ha = jnp.dot(x, w1[:half], preferred_element_type=F32)
    hb = jnp.dot(x, w1[half:], preferred_element_type=F32)
    hb = pltpu.roll(hb, nc - 1, axis=0)
    pe_t = jnp.dot(pe_ref[...], w1, preferred_element_type=F32)[0:1]
    hid = jax.nn.gelu(ha + hb + pe_t)
    out = jnp.dot(hid.astype(BF16), w2_ref[...], preferred_element_type=F32)
    row = lax.broadcasted_iota(jnp.int32, out.shape, 0)
    out = jnp.where(row < nc - 1, out, 0.0)
    o_ref[...] = out.astype(o_ref.dtype)
    ot_ref[...] = _t(out).astype(ot_ref.dtype)


def _compress(proj, w1s, pes, w2s, bsz, seq):
    nc = seq // CMP_STRIDE
    g = N_KV_GROUPS
    return pl.pallas_call(
        _compress_kernel,
        grid=(2, bsz, g),
        in_specs=[pl.BlockSpec((seq, HEAD_DIM), lambda a, i, j: (i, OFF_KC // HEAD_DIM + a * g + j)),
                  pl.BlockSpec((None, CMP_LEN * HEAD_DIM, HEAD_DIM), lambda a, i, j: (a, 0, 0)),
                  pl.BlockSpec((None, 8, CMP_LEN * HEAD_DIM), lambda a, i, j: (a, 0, 0)),
                  pl.BlockSpec((None, HEAD_DIM, HEAD_DIM), lambda a, i, j: (a, 0, 0))],
        out_specs=[pl.BlockSpec((None, None, None, nc, HEAD_DIM), lambda a, i, j: (a, i, j, 0, 0)),
                   pl.BlockSpec((None, None, None, HEAD_DIM, nc), lambda a, i, j: (a, i, j, 0, 0))],
        out_shape=[jax.ShapeDtypeStruct((2, bsz, g, nc, HEAD_DIM), BF16),
                   jax.ShapeDtypeStruct((2, bsz, g, HEAD_DIM, nc), BF16)],
        scratch_shapes=[pltpu.VMEM((seq, HEAD_DIM), F32)],
        compiler_params=_cparams(("parallel", "parallel", "parallel")),
        name="compress",
    )(proj, w1s, pes, w2s)


NSA_STREAMS = 4


def _nsa_kernel(q_ref, gate_ref, kc_ref, vct_ref, ks_ref, e_ref, vst_ref, kw_ref, vwt_ref, o_ref,
                sa_scr, sb_scr, m_scr, acc_scr, *, tq, tk, seq):
    s0 = pl.program_id(2) * tq
    rows = GROUP_SIZE * tq
    gw = GROUP_SIZE * HEAD_DIM
    nsel = seq // SEL_BLOCK
    nc = kc_ref.shape[1]
    topn = min(SEL_TOPN, nsel)
    qscale = HEAD_DIM ** -0.5 * LOG2E
    streams = range(NSA_STREAMS)
    hd = lambda k: slice(k * HEAD_DIM, (k + 1) * HEAD_DIM)
    t_lane = s0 + (lax.broadcasted_iota(jnp.int32, (1, rows), 1) & (tq - 1))

    def reset_stats(k):
        m_scr[k] = jnp.full((1, rows), NEG, F32)
        acc_scr[k] = jnp.zeros((HEAD_DIM + ONES_ROWS, rows), F32)

    def update(k, s, off, vt_ref, mask):
        n = s.shape[0]
        if mask is not None:
            kpos = off + lax.broadcasted_iota(jnp.int32, (n, 1), 0)
            s = jnp.where(mask(kpos), s, NEG)
        m = m_scr[k]
        m_new = jnp.maximum(m, jnp.max(s, axis=0, keepdims=True))
        p = jnp.exp2(s - m_new).astype(BF16)
        vt1 = jnp.concatenate([vt_ref[hd(k), pl.ds(off, n)], jnp.ones((ONES_ROWS, n), BF16)], axis=0)
        acc_scr[k] = jnp.exp2(m - m_new) * acc_scr[k] + jnp.dot(vt1, p, preferred_element_type=F32)
        m_scr[k] = m_new

    def result(k):
        acc = acc_scr[k]
        return acc[:HEAD_DIM] * (1.0 / jnp.maximum(acc[HEAD_DIM:HEAD_DIM + 1], 1e-30))

    q4 = []
    for k in streams:
        q = q_ref[:, k * gw:(k + 1) * gw]
        qk = jnp.concatenate([q[:, hd(r)] for r in range(GROUP_SIZE)], axis=0)
        q4.append((qk.astype(F32) * qscale).astype(BF16))

    causal = lambda kpos: kpos <= t_lane

    for k in streams:
        reset_stats(k)
    n_back = WINDOW // tq
    win = []
    for c in range(n_back, 0, -1):
        off = s0 - c * tq
        if c == n_back:
            lo_edge = jnp.where(off >= 0, t_lane - (WINDOW - 1), seq)
        else:
            lo_edge = jnp.where(off >= 0, 0, seq)
        win.append((pl.multiple_of(jnp.maximum(off, 0), tq), lambda kpos, e=lo_edge: kpos >= e))
    win.append((pl.multiple_of(s0, tq), causal))

    def win_scores_into(k, dst, off):
        dst[k, :tq, :] = _nt_dot(kw_ref[pl.ds(off, tq), hd(k)], q4[k])

    bufs = (sa_scr, sb_scr)
    for k in streams:
        win_scores_into(k, bufs[0], win[0][0])
    for c, (off, mask) in enumerate(win):
        if c + 1 < len(win):
            for k in streams:
                win_scores_into(k, bufs[(c + 1) % 2], win[c + 1][0])
        for k in streams:
            update(k, bufs[c % 2][k, :tq, :], off, vwt_ref, mask)
    o_w = [result(k) for k in streams]

    n_col = lax.broadcasted_iota(jnp.int32, (nc, 1), 0)
    valid_c = (n_col * CMP_STRIDE + (CMP_LEN - 1)) <= t_lane
    jj = lax.broadcasted_iota(jnp.int32, (nsel, nc), 0)
    nn = lax.broadcasted_iota(jnp.int32, (nsel, nc), 1)
    ov = ((nn * CMP_STRIDE < jj * SEL_BLOCK + SEL_BLOCK)
          & (nn * CMP_STRIDE + CMP_LEN > jj * SEL_BLOCK)
          & (nn < nc - 1))
    ov = jnp.where(ov, 1.0, 0.0).astype(BF16)
    jcol = lax.broadcasted_iota(jnp.int32, (nsel, tq), 0)
    cur = lax.shift_right_logical(s0 + lax.broadcasted_iota(jnp.int32, (nsel, tq), 1), 6)
    valid = jcol <= cur
    forced = (jcol == 0) | (jcol == cur) | (jcol == cur - 1)
    o_c, score = [], []
    for k in streams:
        sc = jnp.where(valid_c, _nt_dot(kc_ref[k], q4[k]), NEG)
        mc = jnp.max(sc, axis=0, keepdims=True)
        pc = jnp.exp2(sc - mc)
        inv = 1.0 / jnp.maximum(jnp.sum(pc, axis=0, keepdims=True), 1e-30)
        pc = pc * jnp.where(t_lane >= CMP_LEN - 1, inv, 0.0)
        o_c.append(jnp.dot(vct_ref[k], pc.astype(BF16), preferred_element_type=F32))
        psum = pc[:, 0:tq] + pc[:, tq:2 * tq] + pc[:, 2 * tq:3 * tq] + pc[:, 3 * tq:4 * tq]
        p_hi = psum.astype(BF16)
        p_lo = (psum - p_hi.astype(F32)).astype(BF16)
        imp_t = (jnp.dot(ov, p_hi, preferred_element_type=F32)
                 + jnp.dot(ov, p_lo, preferred_element_type=F32))
        score.append(jnp.where(forced, BIG, jnp.where(valid, imp_t, -BIG)))

    jcol_f = jcol.astype(F32)
    picked = [jnp.zeros((nsel, tq), F32) for _ in streams]
    for _ in range(topn):
        for k in streams:
            best = jnp.max(score[k], axis=0, keepdims=True)
            first = jnp.min(jnp.where(score[k] == best, jcol_f, float(nsel)), axis=0, keepdims=True)
            hit = jcol_f == first
            picked[k] = jnp.where(hit, 1.0, picked[k])
            score[k] = jnp.where(hit, -jnp.inf, score[k])

    q_aug = []
    for k in streams:
        bias_t = jnp.where((picked[k] > 0.5) & valid, 0.0, NEG)
        bias_t = jnp.concatenate([bias_t, jnp.zeros((LANES - nsel, tq), F32)], axis=0)
        bias_q = _t(bias_t).astype(BF16)
        q_aug.append(jnp.concatenate([q4[k], jnp.concatenate([bias_q] * GROUP_SIZE, axis=0)], axis=1))

    def chunk_off(kb):
        return pl.multiple_of(kb * tk, tk)

    def sel_scores_into(k, dst, kb):
        off = chunk_off(kb)
        keys = jnp.concatenate([ks_ref[pl.ds(off, tk), hd(k)], e_ref[pl.ds(off, tk), :]], axis=1)
        dst[k] = _nt_dot(keys, q_aug[k])

    hi = (s0 + tq - 1) // tk + 1
    for k in streams:
        reset_stats(k)
    for k in streams:
        sel_scores_into(k, sa_scr, 0)
    n_pairs = lax.shift_right_logical(hi - 1, 1)

    def pair(i, c):
        k0 = 2 * i
        for k in streams:
            sel_scores_into(k, sb_scr, k0 + 1)
        for k in streams:
            update(k, sa_scr[k], chunk_off(k0), vst_ref, None)
        for k in streams:
            sel_scores_into(k, sa_scr, k0 + 2)
        for k in streams:
            update(k, sb_scr[k], chunk_off(k0 + 1), vst_ref, None)
        return c

    lax.fori_loop(0, n_pairs, pair, 0)

    @pl.when(((hi - 1) & 1) == 1)
    def _():
        for k in streams:
            sel_scores_into(k, sb_scr, hi - 1)
        for k in streams:
            update(k, sa_scr[k], chunk_off(hi - 2), vst_ref, None)
        for k in streams:
            update(k, sb_scr[k], chunk_off(hi - 1), vst_ref, causal)

    @pl.when(((hi - 1) & 1) == 0)
    def _():
        for k in streams:
            update(k, sa_scr[k], chunk_off(hi - 1), vst_ref, causal)

    sg = _t(jax.nn.sigmoid(gate_ref[...].astype(F32)))
    for k in streams:
        o_s = result(k)
        outs = []
        for r in range(GROUP_SIZE):
            sl = slice(r * tq, (r + 1) * tq)
            g0 = k * LANES + 3 * r
            o_r = (sg[g0:g0 + 1] * o_c[k][:, sl] + sg[g0 + 1:g0 + 2] * o_s[:, sl]
                   + sg[g0 + 2:g0 + 3] * o_w[k][:, sl])
            outs.append(_t(o_r))
        o_ref[:, k * gw:(k + 1) * gw] = jnp.concatenate(outs, axis=1).astype(o_ref.dtype)


def _nsa(proj, kvc, kvct, vt, e_tab, bsz, seq):
    tq, tk = 256, 512
    assert WINDOW % tq == 0 and tk % tq == 0
    ns = NSA_STREAMS
    nq = seq // tq
    nc = kvc.shape[3]
    gw = ns * GROUP_SIZE * HEAD_DIM
    kw_ = ns * HEAD_DIM
    rows = GROUP_SIZE * tq
    n_vs_blocks = KV_DIM // kw_

    once = dict(pipeline_mode=pl.Buffered(1))

    def k_spec(off):
        cb = off // kw_
        return pl.BlockSpec((seq, kw_), lambda b, g, i: (b, cb + g), **once)

    return pl.pallas_call(
        functools.partial(_nsa_kernel, tq=tq, tk=tk, seq=seq),
        grid=(bsz, N_KV_GROUPS // ns, nq),
        in_specs=[pl.BlockSpec((tq, gw), lambda b, g, i: (b * nq + i, g)),
                  pl.BlockSpec((tq, ns * LANES), lambda b, g, i: (b * nq + i, OFF_GATE // (ns * LANES) + g)),
                  pl.BlockSpec((None, None, ns, nc, HEAD_DIM), lambda b, g, i: (0, b, g, 0, 0)),
                  pl.BlockSpec((None, None, ns, HEAD_DIM, nc), lambda b, g, i: (1, b, g, 0, 0)),
                  k_spec(OFF_KS),
                  pl.BlockSpec((seq, LANES), lambda b, g, i: (0, 0), **once),
                  pl.BlockSpec((kw_, seq), lambda b, g, i: (g, b), **once),
                  k_spec(OFF_KW),
                  pl.BlockSpec((kw_, seq), lambda b, g, i: (n_vs_blocks + g, b), **once)],
        out_specs=pl.BlockSpec((tq, gw), lambda b, g, i: (b * nq + i, g)),
        out_shape=jax.ShapeDtypeStruct((bsz * seq, D_MODEL), BF16),
        scratch_shapes=[pltpu.VMEM((ns, tk, rows), F32), pltpu.VMEM((ns, tk, rows), F32),
                        pltpu.VMEM((ns, 1, rows), F32),
                        pltpu.VMEM((ns, HEAD_DIM + ONES_ROWS, rows), F32)],
        compiler_params=_cparams(("parallel", "parallel", "arbitrary")),
        name="nsa",
    )(proj, proj, kvc, kvct, proj, e_tab, vt, proj, vt)


def _s5_kernel(u_ref, pg_ref, bt_ref, pl_ref, dsk_ref, wgl_ref, wou_ref, o_ref, wgl_o, wou_o,
               wt_scr, ws_scr, wo_scr, xp_scr, tok_scr, y_scr, *, chunks_per_seq):
    L = SSM_CHUNK
    hp = lax.Precision.HIGHEST
    wgl_o[...] = wgl_ref[...].astype(wgl_o.dtype)
    wou_o[...] = wou_ref[...].astype(wou_o.dtype)
    are_g, aim_g, ldt_g = pg_ref[0], pg_ref[1], pg_ref[2]
    bt = bt_ref[...]
    are_l, aim_l, ldt_l, cr_l, ci_l = pl_ref[0], pl_ref[1], pl_ref[2], pl_ref[3], pl_ref[4]

    lane = lax.broadcasted_iota(jnp.int32, (LANES, LANES), 1)
    rowi = lax.broadcasted_iota(jnp.int32, (LANES, LANES), 0)
    is_re = lane < SSM_STATE
    row_grp = rowi // SSM_GROUP
    lane_grp = lane // SSM_GROUP

    def rows16(z):
        return jnp.concatenate([jnp.broadcast_to(z[g:g + 1], (SSM_GROUP, LANES))
                                for g in range(GROUPS_PER_BLOCK)], axis=0)

    def cmul(x, zr, zi):
        return x * zr + pltpu.roll(x, SSM_STATE, axis=1) * jnp.where(is_re, -zi, zi)

    dt_g = jnp.exp(ldt_g)
    l1r = jnp.exp(are_g * dt_g) * jnp.cos(aim_g * dt_g)
    l1i = jnp.exp(are_g * dt_g) * jnp.sin(aim_g * dt_g)
    den = are_g * are_g + aim_g * aim_g
    nr = l1r - 1.0
    coef_r = (nr * are_g + l1i * aim_g) / den
    coef_i = (l1i * are_g - nr * aim_g) / den
    bbar = cmul(bt, rows16(coef_r), rows16(coef_i))

    pw_g = [(jnp.ones_like(l1r), jnp.zeros_like(l1r))]
    for _ in range(L):
        pr, pi = pw_g[-1]
        pw_g.append((pr * l1r - pi * l1i, pr * l1i + pi * l1r))

    dt_l = jnp.exp(ldt_l)
    m1r = jnp.exp(are_l * dt_l) * jnp.cos(aim_l * dt_l)
    m1i = jnp.exp(are_l * dt_l) * jnp.sin(aim_l * dt_l)
    gs = []
    pr, pi = jnp.ones_like(m1r), jnp.zeros_like(m1r)
    for _ in range(L + 1):
        gs.append(jnp.concatenate([cr_l * pr - ci_l * pi, -(cr_l * pi + ci_l * pr)], axis=0))
        pr, pi = pr * m1r - pi * m1i, pr * m1i + pi * m1r

    same_grp = row_grp == lane_grp
    zeros_tile = jnp.zeros((LANES, LANES), BF16)

    for t in range(L):
        d_t = jnp.dot(bbar, gs[t], precision=hp, preferred_element_type=F32)
        d_t = jnp.where(same_grp, d_t, 0.0).astype(BF16)
        for j in range(L - t):
            i = j + t
            wt_scr[j * LANES:(j + 1) * LANES, i * LANES:(i + 1) * LANES] = d_t
    for p2 in range(L // 2):
        wt_scr[(2 * p2 + 1) * LANES:(2 * p2 + 2) * LANES, 2 * p2 * LANES:(2 * p2 + 1) * LANES] = zeros_tile

    ws_scr[...] = jnp.zeros(ws_scr.shape, BF16)
    for j in range(L):
        zr, zi = pw_g[L - 1 - j]
        s_j = cmul(bbar, rows16(zr), rows16(zi)).astype(BF16)
        for g in range(GROUPS_PER_BLOCK):
            r0 = j * LANES + g * SSM_GROUP
            ws_scr[r0:r0 + SSM_GROUP, g * LANES:(g + 1) * LANES] = s_j[g * SSM_GROUP:(g + 1) * SSM_GROUP]

    for g in range(GROUPS_PER_BLOCK):
        for i in range(L):
            wo_scr[g * LANES:(g + 1) * LANES, i * LANES:(i + 1) * LANES] = (
                jnp.where(lane_grp == g, gs[i + 1], 0.0).astype(BF16))

    tok_scr[...] = u_ref[...].astype(F32)
    nrow = u_ref.shape[0] // L
    u = jnp.concatenate([tok_scr[pl.ds(j, nrow, stride=L), :].astype(BF16) for j in range(L)], axis=1)
    st = jnp.dot(u, ws_scr[...], preferred_element_type=F32)

    kk = lax.broadcasted_iota(jnp.int32, (nrow, 1), 0) & (chunks_per_seq - 1)
    lane1 = lax.broadcasted_iota(jnp.int32, (1, LANES), 1)
    assert L // 2 == GROUPS_PER_BLOCK
    for g in range(GROUPS_PER_BLOCK):
        cols = slice(2 * g * LANES, (2 * g + 2) * LANES)
        kr = (2 * g + 2) * LANES
        y_scr[:, cols] = jnp.dot(u[:, :kr], wt_scr[:kr, cols], preferred_element_type=F32)

        xs = st[:, g * LANES:(g + 1) * LANES]
        zr = pw_g[L][0][g:g + 1]
        zi = pw_g[L][1][g:g + 1]
        d = 1
        while d < chunks_per_seq:
            sh = jnp.where(kk >= d, pltpu.roll(xs, d, axis=0), 0.0)
            zmix = jnp.where(lane1 < SSM_STATE, -zi, zi)
            xs = xs + sh * zr + pltpu.roll(sh, SSM_STATE, axis=1) * zmix
            zr, zi = zr * zr - zi * zi, 2.0 * zr * zi
            d *= 2
        prev = jnp.where(kk >= 1, pltpu.roll(xs, 1, axis=0), 0.0)
        xp_scr[:, g * LANES:(g + 1) * LANES] = prev.astype(BF16)

    xp = xp_scr[...]
    for p2 in range(L // 2):
        cols = slice(2 * p2 * LANES, (2 * p2 + 2) * LANES)
        y = y_scr[:, cols] + jnp.dot(xp, wo_scr[:, cols], preferred_element_type=F32)
        y = jax.nn.gelu(y + dsk_ref[:, cols] * u[:, cols].astype(F32))
        for i in range(2):
            tok_scr[pl.ds(2 * p2 + i, nrow, stride=L), :] = y[:, i * LANES:(i + 1) * LANES]
    o_ref[...] = tok_scr[...].astype(o_ref.dtype)


def _s5(proj, p_grp, p_bt, p_lanes, dsk, w_glu, w_out, chunks_per_seq):
    m = proj.shape[0]
    nrow, kk = m // SSM_CHUNK, SSM_CHUNK * LANES
    nst = GROUPS_PER_BLOCK * LANES
    ws_ = D_MODEL // N_SSM_BLOCKS
    wspec = pl.BlockSpec((ws_, D_MODEL), lambda i: (i, 0))
    return pl.pallas_call(
        functools.partial(_s5_kernel, chunks_per_seq=chunks_per_seq),
        grid=(N_SSM_BLOCKS,),
        in_specs=[pl.BlockSpec((m, LANES), lambda i: (0, OFF_SSM // LANES + i)),
                  pl.BlockSpec((None, 3, GROUPS_PER_BLOCK, LANES), lambda i: (i, 0, 0, 0)),
                  pl.BlockSpec((None, LANES, LANES), lambda i: (i, 0, 0)),
                  pl.BlockSpec((None, 5, SSM_STATE, LANES), lambda i: (i, 0, 0, 0)),
                  pl.BlockSpec((None, 1, kk), lambda i: (i, 0, 0)),
                  wspec, wspec],
        out_specs=[pl.BlockSpec((m, LANES), lambda i: (0, i)), wspec, wspec],
        out_shape=[jax.ShapeDtypeStruct((m, D_MODEL), BF16),
                   jax.ShapeDtypeStruct((D_MODEL, D_MODEL), BF16), jax.ShapeDtypeStruct((D_MODEL, D_MODEL), BF16)],
        scratch_shapes=[pltpu.VMEM((kk, kk), BF16), pltpu.VMEM((kk, nst), BF16),
                        pltpu.VMEM((nst, kk), BF16), pltpu.VMEM((nrow, nst), BF16),
                        pltpu.VMEM((m, LANES), F32), pltpu.VMEM((nrow, kk), F32)],
        compiler_params=_cparams(("parallel",)),
        name="s5",
    )(proj, p_grp, p_bt, p_lanes, dsk, w_glu, w_out)


def _mix_kernel(ys_ref, ya_ref, ga_ref, gb_ref, x_ref, wg_ref, bg_ref, wo_ref, gt_ref, gm_ref, sc_ref, sh_ref,
                wu_ref, wd_ref, h_ref, u_ref, wu_o, wd_o):
    wu_o[...] = wu_ref[...].astype(wu_o.dtype)
    wd_o[...] = wd_ref[...].astype(wd_o.dtype)
    ys = ys_ref[...]
    z = jnp.dot(ys, wg_ref[...], preferred_element_type=F32) + bg_ref[...]
    yb = ys.astype(F32) * jax.nn.sigmoid(z)
    mix = (jax.nn.sigmoid(ga_ref[...].astype(F32)) * ya_ref[...].astype(F32)
           + jax.nn.sigmoid(gb_ref[...].astype(F32)) * yb)
    h = x_ref[...] + gt_ref[...] * jnp.dot(mix.astype(BF16), wo_ref[...], preferred_element_type=F32)
    h_ref[...] = h
    ms = jnp.mean(h * h, axis=-1, keepdims=True)
    y = h * lax.rsqrt(ms + EPS) * gm_ref[...]
    u_ref[...] = (y * (1.0 + sc_ref[...]) + sh_ref[...]).astype(u_ref.dtype)


def _mix(ys, ya, proj, x2, w_glu, b_glu, w_out, g_mlp, ada3, w_up, w_down, seq):
    m = ys.shape[0]
    tm = 256
    per_b = seq // tm
    ff = w_up.shape[1]
    fs = ff // (m // tm)
    const = dict(pipeline_mode=pl.Buffered(1))
    row = lambda cb: pl.BlockSpec((tm, D_MODEL), lambda i: (i, cb))

    def ada_spec(k):
        return pl.BlockSpec((None, 1, D_MODEL), lambda i: ((i // per_b) * 6 + k, 0, 0))

    return pl.pallas_call(
        _mix_kernel,
        grid=(m // tm,),
        in_specs=[row(0), row(0), row(OFF_GA // D_MODEL), row(OFF_GB // D_MODEL), row(0),
                  pl.BlockSpec((D_MODEL, D_MODEL), lambda i: (0, 0), **const),
                  pl.BlockSpec((1, D_MODEL), lambda i: (0, 0)),
                  pl.BlockSpec((D_MODEL, D_MODEL), lambda i: (0, 0), **const),
                  ada_spec(2),
                  pl.BlockSpec((1, D_MODEL), lambda i: (0, 0)),
                  ada_spec(4), ada_spec(3),
                  pl.BlockSpec((D_MODEL, fs), lambda i: (0, i)),
                  pl.BlockSpec((fs, D_MODEL), lambda i: (i, 0))],
        out_specs=[row(0), row(0),
                   pl.BlockSpec((D_MODEL, fs), lambda i: (0, i)),
                   pl.BlockSpec((fs, D_MODEL), lambda i: (i, 0))],
        out_shape=[jax.ShapeDtypeStruct((m, D_MODEL), F32), jax.ShapeDtypeStruct((m, D_MODEL), BF16),
                   jax.ShapeDtypeStruct((D_MODEL, ff), BF16), jax.ShapeDtypeStruct((ff, D_MODEL), BF16)],
        compiler_params=_cparams(("parallel",)),
        name="mix",
    )(ys, ya, proj, proj, x2, w_glu, b_glu, w_out, ada3, g_mlp, ada3, ada3, w_up, w_down)


def _mlp_kernel(h_ref, u_ref, gt_ref, wu_ref, wd_ref, gf_ref, o_ref, acc_scr):
    f = pl.program_id(1)

    @pl.when(f == 0)
    def _():
        acc_scr[...] = jnp.zeros_like(acc_scr)

    a = jnp.dot(u_ref[...], wu_ref[...], preferred_element_type=F32)
    a = jnp.square(jnp.maximum(a, 0.0))
    acc_scr[...] += jnp.dot(a.astype(BF16), wd_ref[...], preferred_element_type=F32)

    @pl.when(f == pl.num_programs(1) - 1)
    def _():
        h2 = h_ref[...] + gt_ref[...] * acc_scr[...]
        ms = jnp.mean(h2 * h2, axis=-1, keepdims=True)
        o_ref[...] = h2 * lax.rsqrt(ms + EPS) * gf_ref[...]


def _mlp(h1, u2, ada3, w_up, w_down, g_final, seq):
    m = h1.shape[0]
    tm, tf = 512, 1024
    per_b = seq // tm
    ff = w_up.shape[1]
    return pl.pallas_call(
        _mlp_kernel,
        grid=(m // tm, ff // tf),
        in_specs=[pl.BlockSpec((tm, D_MODEL), lambda i, f: (i, 0)),
                  pl.BlockSpec((tm, D_MODEL), lambda i, f: (i, 0)),
                  pl.BlockSpec((None, 1, D_MODEL), lambda i, f: ((i // per_b) * 6 + 5, 0, 0)),
                  pl.BlockSpec((D_MODEL, tf), lambda i, f: (0, f)),
                  pl.BlockSpec((tf, D_MODEL), lambda i, f: (f, 0)),
                  pl.BlockSpec((1, D_MODEL), lambda i, f: (0, 0))],
        out_specs=pl.BlockSpec((tm, D_MODEL), lambda i, f: (i, 0)),
        out_shape=jax.ShapeDtypeStruct((m, D_MODEL), F32),
        scratch_shapes=[pltpu.VMEM((tm, D_MODEL), F32)],
        compiler_params=_cparams(("parallel", "arbitrary")),
        name="mlp",
    )(h1, u2, ada3, w_up, w_down, g_final)


def _gate_rows(wt):
    wg = wt[RAW_GATE:RAW_SSM].reshape(N_KV_GROUPS, 3 * GROUP_SIZE, D_MODEL)
    wg = jnp.pad(wg, ((0, 0), (0, LANES - 3 * GROUP_SIZE), (0, 0))).reshape(N_KV_GROUPS * LANES, D_MODEL)
    return wg.astype(BF16)


def _s5_params(a_re, a_im, log_dt, b_re, b_im, c_re, c_im, d_skip):
    nb, gp, p, cg = N_SSM_BLOCKS, GROUPS_PER_BLOCK, SSM_STATE, SSM_GROUP
    ldt = jnp.broadcast_to(log_dt[:, None], (SSM_NGROUPS, p))

    def grp(a):
        return jnp.concatenate([a, a], axis=-1).reshape(nb, gp, LANES)

    def lanes(a):
        a = jnp.broadcast_to(a.reshape(nb, gp, p, 1).transpose(0, 2, 1, 3), (nb, p, gp, cg))
        return a.reshape(nb, p, LANES)

    def ct(cm):
        return cm.reshape(nb, gp, cg, p).transpose(0, 3, 1, 2).reshape(nb, p, LANES)

    p_grp = jnp.stack([grp(a_re), grp(a_im), grp(ldt)], axis=1)
    p_bt = jnp.stack([b_re, b_im], axis=1).transpose(0, 3, 1, 2).reshape(nb, LANES, LANES)
    p_lanes = jnp.stack([lanes(a_re), lanes(a_im), lanes(ldt), ct(c_re), ct(c_im)], axis=1)
    dsk = jnp.tile(d_skip.reshape(nb, 1, LANES), (1, 1, SSM_CHUNK))
    return p_grp, p_bt, p_lanes, dsk


def kernel(x, c, w_ada, b_ada, g_mix, w_in, w_ck1, w_ck2, pe_ck, w_cv1, w_cv2, pe_cv,
           a_re, a_im, log_dt, b_re, b_im, c_re, c_im, d_skip, w_glu, b_glu,
           w_out, g_mlp, w_up, w_down, g_final):
    bsz, seq, _ = x.shape
    assert w_ada.shape[0] == 1, "single-layer block"
    assert seq % 1024 == 0 and (seq & (seq - 1)) == 0
    m = bsz * seq
    x2 = x.reshape(m, D_MODEL)

    c_pad = jnp.pad(c, ((0, 8 - bsz), (0, 0)))
    ada = _ada(c_pad, w_ada[0], b_ada)
    ada3 = ada[:bsz].reshape(bsz * 6, 1, D_MODEL)

    w_in_t = w_in[0].T
    proj, vt = _inproj(x2, g_mix, ada3, w_in_t, _gate_rows(w_in_t), seq)

    w1s = jnp.stack([w_ck1[0], w_cv1[0]]).reshape(2, CMP_LEN * HEAD_DIM, HEAD_DIM).astype(BF16)
    pes = jnp.stack([pe_ck[0], pe_cv[0]]).reshape(2, 1, CMP_LEN * HEAD_DIM)
    pes = jnp.pad(pes, ((0, 0), (0, 7), (0, 0))).astype(BF16)
    w2s = jnp.stack([w_ck2[0], w_cv2[0]]).astype(BF16)
    kvc, kvct = _compress(proj, w1s, pes, w2s, bsz, seq)

    e_tab = (jnp.arange(seq)[:, None] // SEL_BLOCK == jnp.arange(LANES)[None, :]).astype(BF16)
    y_a = _nsa(proj, kvc, kvct, vt, e_tab, bsz, seq)

    p_grp, p_bt, p_lanes, dsk = _s5_params(a_re[0], a_im[0], log_dt[0], b_re[0], b_im[0],
                                           c_re[0], c_im[0], d_skip[0])
    ys, w_glu_b, w_out_b = _s5(proj, p_grp, p_bt, p_lanes, dsk, w_glu[0], w_out[0], seq // SSM_CHUNK)

    h1, u2, w_up_b, w_down_b = _mix(ys, y_a, proj, x2, w_glu_b, b_glu, w_out_b, g_mlp, ada3, w_up[0], w_down[0], seq)
    out = _mlp(h1, u2, ada3, w_up_b, w_down_b, g_final.reshape(1, D_MODEL), seq)
    return out.reshape(bsz, seq, D_MODEL)
```

```python
import functools

import jax
import jax.numpy as jnp
from jax import lax
from jax.experimental import pallas as pl
from jax.experimental.pallas import tpu as pltpu

F32 = jnp.float32
BF16 = jnp.bfloat16

D_MODEL = 2048
N_HEADS = 16
HEAD_DIM = 128
N_KV_GROUPS = 4
GROUP_SIZE = 4
KV_DIM = N_KV_GROUPS * HEAD_DIM
CMP_LEN = 32
CMP_STRIDE = 16
SEL_BLOCK = 64
SEL_TOPN = 16
WINDOW = 512
SSM_GROUP = 16
SSM_NGROUPS = D_MODEL // SSM_GROUP
SSM_STATE = 64
EPS = 1e-6
NEG = -1e30
BIG = 1e6
LOG2E = 1.4426950408889634
ONES_ROWS = 16

LANES = 128
SUBLANES = 8
SEL_SHIFT = SEL_BLOCK.bit_length() - 1
assert 1 << SEL_SHIFT == SEL_BLOCK
SSM_CHUNK = 16
GROUPS_PER_BLOCK = LANES // SSM_GROUP
N_SSM_BLOCKS = SSM_NGROUPS // GROUPS_PER_BLOCK

OFF_Q = 0
OFF_SSM = OFF_Q + D_MODEL
OFF_GA = OFF_SSM + D_MODEL
OFF_GB = OFF_GA + D_MODEL
OFF_KC = OFF_GB + D_MODEL
OFF_VC = OFF_KC + KV_DIM
OFF_KS = OFF_VC + KV_DIM
OFF_KW = OFF_KS + KV_DIM
PROJ_PAD = OFF_KW + KV_DIM
N_GATES = 3 * N_HEADS
VT_ROWS = 2 * KV_DIM + LANES
RAW_VS = D_MODEL + 3 * KV_DIM
RAW_KW = RAW_VS + KV_DIM
RAW_VW = RAW_KW + KV_DIM
RAW_GATE = RAW_VW + KV_DIM
RAW_SSM = RAW_GATE + 3 * N_HEADS

VMEM_LIMIT = 56 * 1024 * 1024

ADA_TN = 1024
INPROJ_TM, INPROJ_TN = 1024, 512
NSA_TQ, NSA_TK = 256, 512
MIX_TM = 256
MLP_TM, MLP_TF = 512, 1024


def _cparams(sem):
    return pltpu.CompilerParams(dimension_semantics=sem, vmem_limit_bytes=VMEM_LIMIT)


def _t(x):
    r, c = x.shape
    rows = []
    for j in range(c // LANES):
        rows.append(jnp.concatenate([x[i * LANES:(i + 1) * LANES, j * LANES:(j + 1) * LANES].T
                                     for i in range(r // LANES)], axis=1))
    return jnp.concatenate(rows, axis=0)


def _nt_dot(a, b):
    return lax.dot_general(a, b, (((1,), (1,)), ((), ())), preferred_element_type=F32)


def _ada_kernel(c_ref, w_ref, b_ref, o_ref):
    c = c_ref[...]
    cond = c * jax.nn.sigmoid(c)
    o_ref[...] = jnp.dot(cond.astype(BF16), w_ref[...].astype(BF16),
                         preferred_element_type=F32) + b_ref[...]


def _ada(c_pad, w_ada, b_ada):
    n = w_ada.shape[1]
    tn = ADA_TN
    return pl.pallas_call(
        _ada_kernel,
        grid=(n // tn,),
        in_specs=[pl.BlockSpec((SUBLANES, D_MODEL), lambda j: (0, 0)),
                  pl.BlockSpec((D_MODEL, tn), lambda j: (0, j)),
                  pl.BlockSpec((1, tn), lambda j: (0, j))],
        out_specs=pl.BlockSpec((SUBLANES, tn), lambda j: (0, j)),
        out_shape=jax.ShapeDtypeStruct((SUBLANES, n), F32),
        compiler_params=_cparams(("parallel",)),
        name="ada",
    )(c_pad, w_ada, b_ada)


_Q_TILES = D_MODEL // INPROJ_TN
_TAIL_TILES = 3 * D_MODEL // INPROJ_TN
_ROW_UNIT = 16


def _w_row(j):
    k = j - (_Q_TILES + _TAIL_TILES)
    kv_block = jnp.where(k < 3, 4 + k, 8)
    head = jnp.where(j < _Q_TILES, j, kv_block) * (INPROJ_TN // _ROW_UNIT)
    tail = RAW_SSM // _ROW_UNIT + (j - _Q_TILES) * (INPROJ_TN // _ROW_UNIT)
    return jnp.where((j >= _Q_TILES) & (k < 0), tail, head) * _ROW_UNIT


def _inproj_kernel(x_ref, g_ref, sc_ref, sh_ref, w_ref, wg_ref, wvs_ref, wvw_ref, o_ref, vt_ref, u_scr):
    @pl.when(pl.program_id(1) == 0)
    def _():
        x = x_ref[...]
        ms = jnp.mean(x * x, axis=-1, keepdims=True)
        y = x * lax.rsqrt(ms + EPS) * g_ref[...]
        u = (y * (1.0 + sc_ref[...]) + sh_ref[...]).astype(BF16)
        u_scr[...] = u
        vt_ref[:KV_DIM, :] = _nt_dot(wvs_ref[...].astype(BF16), u).astype(vt_ref.dtype)
        vt_ref[KV_DIM:2 * KV_DIM, :] = _nt_dot(wvw_ref[...].astype(BF16), u).astype(vt_ref.dtype)
        vt_ref[2 * KV_DIM:, :] = _nt_dot(wg_ref[...], u).astype(vt_ref.dtype)

    o_ref[...] = _nt_dot(u_scr[...], w_ref[...].astype(BF16)).astype(o_ref.dtype)


def _inproj(x2, g_mix, ada3, w_in_t, w_gate_t, seq):
    m = x2.shape[0]
    tm, tn = INPROJ_TM, INPROJ_TN
    per_b = seq // tm
    const = dict(pipeline_mode=pl.Buffered(1))
    wblk = (pl.Element(tn), pl.Element(D_MODEL))
    return pl.pallas_call(
        _inproj_kernel,
        grid=(m // tm, PROJ_PAD // tn),
        in_specs=[pl.BlockSpec((tm, D_MODEL), lambda i, j: (i, 0)),
                  pl.BlockSpec((1, D_MODEL), lambda i, j: (0, 0)),
                  pl.BlockSpec((None, 1, D_MODEL), lambda i, j: ((i // per_b) * 6 + 1, 0, 0)),
                  pl.BlockSpec((None, 1, D_MODEL), lambda i, j: ((i // per_b) * 6 + 0, 0, 0)),
                  pl.BlockSpec(wblk, lambda i, j: (_w_row(j), 0)),
                  pl.BlockSpec((LANES, D_MODEL), lambda i, j: (0, 0), **const),
                  pl.BlockSpec((tn, D_MODEL), lambda i, j: (RAW_VS // tn, 0), **const),
                  pl.BlockSpec((tn, D_MODEL), lambda i, j: (RAW_VW // tn, 0), **const)],
        out_specs=[pl.BlockSpec((tm, tn), lambda i, j: (i, j)),
                   pl.BlockSpec((VT_ROWS, tm), lambda i, j: (0, i))],
        out_shape=[jax.ShapeDtypeStruct((m, PROJ_PAD), BF16),
                   jax.ShapeDtypeStruct((VT_ROWS, m), BF16)],
        scratch_shapes=[pltpu.VMEM((tm, D_MODEL), BF16)],
        compiler_params=_cparams(("parallel", "arbitrary")),
        name="inproj",
    )(x2, g_mix, ada3, ada3, w_in_t, w_gate_t, w_in_t, w_in_t)


def _compress_kernel(x_ref, w1_ref, pe_ref, w2_ref, o_ref, ot_ref, xf_scr):
    xf_scr[...] = x_ref[...].astype(F32)
    nc = x_ref.shape[0] // CMP_STRIDE
    x = jnp.concatenate([xf_scr[pl.ds(l, nc, stride=CMP_STRIDE), :].astype(BF16)
                         for l in range(CMP_STRIDE)], axis=1)
    half = CMP_STRIDE * HEAD_DIM
    w1 = w1_ref[...]
    ha = jnp.dot(x, w1[:half], preferred_element_type=F32)
    hb = jnp.dot(x, w1[half:], preferred_element_type=F32)
    hb = pltpu.roll(hb, nc - 1, axis=0)
    pe_t = jnp.dot(pe_ref[...], w1, preferred_element_type=F32)[0:1]
    hid = jax.nn.gelu(ha + hb + pe_t)
    out = jnp.dot(hid.astype(BF16), w2_ref[...], preferred_element_type=F32)
    row = lax.broadcasted_iota(jnp.int32, out.shape, 0)
    out = jnp.where(row < nc - 1, out, 0.0)
    o_ref[...] = out.astype(o_ref.dtype)
    ot_ref[...] = _t(out).astype(ot_ref.dtype)


def _compress(proj, w1s, pes, w2s, bsz, seq):
    nc = seq // CMP_STRIDE
    g = N_KV_GROUPS
    return pl.pallas_call(
        _compress_kernel,
        grid=(2, bsz, g),
        in_specs=[pl.BlockSpec((seq, HEAD_DIM), lambda a, i, j: (i, OFF_KC // HEAD_DIM + a * g + j)),
                  pl.BlockSpec((None, CMP_LEN * HEAD_DIM, HEAD_DIM), lambda a, i, j: (a, 0, 0)),
                  pl.BlockSpec((None, SUBLANES, CMP_LEN * HEAD_DIM), lambda a, i, j: (a, 0, 0)),
                  pl.BlockSpec((None, HEAD_DIM, HEAD_DIM), lambda a, i, j: (a, 0, 0))],
        out_specs=[pl.BlockSpec((None, None, None, nc, HEAD_DIM), lambda a, i, j: (a, i, j, 0, 0)),
                   pl.BlockSpec((None, None, None, HEAD_DIM, nc), lambda a, i, j: (a, i, j, 0, 0))],
        out_shape=[jax.ShapeDtypeStruct((2, bsz, g, nc, HEAD_DIM), BF16),
                   jax.ShapeDtypeStruct((2, bsz, g, HEAD_DIM, nc), BF16)],
        scratch_shapes=[pltpu.VMEM((seq, HEAD_DIM), F32)],
        compiler_params=_cparams(("parallel", "parallel", "parallel")),
        name="compress",
    )(proj, w1s, pes, w2s)


NSA_STREAMS = 4


def _nsa_kernel(q_ref, gate_ref, kc_ref, vct_ref, ks_ref, e_ref, vst_ref, kw_ref, vwt_ref, o_ref,
                sa_scr, sb_scr, m_scr, acc_scr, *, tq, tk, seq):
    s0 = pl.program_id(2) * tq
    rows = GROUP_SIZE * tq
    gw = GROUP_SIZE * HEAD_DIM
    nsel = seq // SEL_BLOCK
    nc = kc_ref.shape[1]
    topn = min(SEL_TOPN, nsel)
    qscale = HEAD_DIM ** -0.5 * LOG2E
    streams = range(NSA_STREAMS)
    hd = lambda k: slice(k * HEAD_DIM, (k + 1) * HEAD_DIM)
    t_lane = s0 + (lax.broadcasted_iota(jnp.int32, (1, rows), 1) & (tq - 1))

    def reset_stats(k):
        m_scr[k] = jnp.full((1, rows), NEG, F32)
        acc_scr[k] = jnp.zeros((HEAD_DIM + ONES_ROWS, rows), F32)

    def update(k, s, off, vt_ref, mask):
        n = s.shape[0]
        if mask is not None:
            kpos = off + lax.broadcasted_iota(jnp.int32, (n, 1), 0)
            s = jnp.where(mask(kpos), s, NEG)
        m = m_scr[k]
        m_new = jnp.maximum(m, jnp.max(s, axis=0, keepdims=True))
        p = jnp.exp2(s - m_new).astype(BF16)
        vt1 = jnp.concatenate([vt_ref[hd(k), pl.ds(off, n)], jnp.ones((ONES_ROWS, n), BF16)], axis=0)
        acc_scr[k] = jnp.exp2(m - m_new) * acc_scr[k] + jnp.dot(vt1, p, preferred_element_type=F32)
        m_scr[k] = m_new

    def result(k):
        acc = acc_scr[k]
        return acc[:HEAD_DIM] * (1.0 / jnp.maximum(acc[HEAD_DIM:HEAD_DIM + 1], 1e-30))

    q4 = []
    for k in streams:
        q = q_ref[:, k * gw:(k + 1) * gw]
        qk = jnp.concatenate([q[:, hd(r)] for r in range(GROUP_SIZE)], axis=0)
        q4.append((qk.astype(F32) * qscale).astype(BF16))

    causal = lambda kpos: kpos <= t_lane

    for k in streams:
        reset_stats(k)
    n_back = WINDOW // tq
    win = []
    for c in range(n_back, 0, -1):
        off = s0 - c * tq
        if c == n_back:
            lo_edge = jnp.where(off >= 0, t_lane - (WINDOW - 1), seq)
        else:
            lo_edge = jnp.where(off >= 0, 0, seq)
        win.append((pl.multiple_of(jnp.maximum(off, 0), tq), lambda kpos, e=lo_edge: kpos >= e))
    win.append((pl.multiple_of(s0, tq), causal))

    def win_scores_into(k, dst, off):
        dst[k, :tq, :] = _nt_dot(kw_ref[pl.ds(off, tq), hd(k)], q4[k])

    bufs = (sa_scr, sb_scr)
    for k in streams:
        win_scores_into(k, bufs[0], win[0][0])
    for c, (off, mask) in enumerate(win):
        if c + 1 < len(win):
            for k in streams:
                win_scores_into(k, bufs[(c + 1) % 2], win[c + 1][0])
        for k in streams:
            update(k, bufs[c % 2][k, :tq, :], off, vwt_ref, mask)
    o_w = [result(k) for k in streams]

    n_col = lax.broadcasted_iota(jnp.int32, (nc, 1), 0)
    valid_c = (n_col * CMP_STRIDE + (CMP_LEN - 1)) <= t_lane
    jj = lax.broadcasted_iota(jnp.int32, (nsel, nc), 0)
    nn = lax.broadcasted_iota(jnp.int32, (nsel, nc), 1)
    ov = ((nn * CMP_STRIDE < jj * SEL_BLOCK + SEL_BLOCK)
          & (nn * CMP_STRIDE + CMP_LEN > jj * SEL_BLOCK)
          & (nn < nc - 1))
    ov = jnp.where(ov, 1.0, 0.0).astype(BF16)
    jcol = lax.broadcasted_iota(jnp.int32, (nsel, tq), 0)
    cur = lax.shift_right_logical(s0 + lax.broadcasted_iota(jnp.int32, (nsel, tq), 1), SEL_SHIFT)
    valid = jcol <= cur
    forced = (jcol == 0) | (jcol == cur) | (jcol == cur - 1)
    o_c, score = [], []
    for k in streams:
        sc = jnp.where(valid_c, _nt_dot(kc_ref[k], q4[k]), NEG)
        mc = jnp.max(sc, axis=0, keepdims=True)
        pc = jnp.exp2(sc - mc)
        inv = 1.0 / jnp.maximum(jnp.sum(pc, axis=0, keepdims=True), 1e-30)
        pc = pc * jnp.where(t_lane >= CMP_LEN - 1, inv, 0.0)
        o_c.append(jnp.dot(vct_ref[k], pc.astype(BF16), preferred_element_type=F32))
        psum = pc[:, 0:tq] + pc[:, tq:2 * tq] + pc[:, 2 * tq:3 * tq] + pc[:, 3 * tq:4 * tq]
        p_hi = psum.astype(BF16)
        p_lo = (psum - p_hi.astype(F32)).astype(BF16)
        imp_t = (jnp.dot(ov, p_hi, preferred_element_type=F32)
                 + jnp.dot(ov, p_lo, preferred_element_type=F32))
        score.append(jnp.where(forced, BIG, jnp.where(valid, imp_t, -BIG)))

    jcol_f = jcol.astype(F32)
    picked = [jnp.zeros((nsel, tq), F32) for _ in streams]
    for _ in range(topn):
        for k in streams:
            best = jnp.max(score[k], axis=0, keepdims=True)
            first = jnp.min(jnp.where(score[k] == best, jcol_f, float(nsel)), axis=0, keepdims=True)
            hit = jcol_f == first
            picked[k] = jnp.where(hit, 1.0, picked[k])
            score[k] = jnp.where(hit, -jnp.inf, score[k])

    q_aug = []
    for k in streams:
        bias_t = jnp.where((picked[k] > 0.5) & valid, 0.0, NEG)
        bias_t = jnp.concatenate([bias_t, jnp.zeros((LANES - nsel, tq), F32)], axis=0)
        bias_q = _t(bias_t).astype(BF16)
        q_aug.append(jnp.concatenate([q4[k], jnp.concatenate([bias_q] * GROUP_SIZE, axis=0)], axis=1))

    def chunk_off(kb):
        return pl.multiple_of(kb * tk, tk)

    def sel_scores_into(k, dst, kb):
        off = chunk_off(kb)
        keys = jnp.concatenate([ks_ref[pl.ds(off, tk), hd(k)], e_ref[pl.ds(off, tk), :]], axis=1)
        dst[k] = _nt_dot(keys, q_aug[k])

    hi = (s0 + tq - 1) // tk + 1
    for k in streams:
        reset_stats(k)
    for k in streams:
        sel_scores_into(k, sa_scr, 0)
    n_pairs = lax.shift_right_logical(hi - 1, 1)

    def pair(i, c):
        k0 = 2 * i
        for k in streams:
            sel_scores_into(k, sb_scr, k0 + 1)
        for k in streams:
            update(k, sa_scr[k], chunk_off(k0), vst_ref, None)
        for k in streams:
            sel_scores_into(k, sa_scr, k0 + 2)
        for k in streams:
            update(k, sb_scr[k], chunk_off(k0 + 1), vst_ref, None)
        return c

    lax.fori_loop(0, n_pairs, pair, 0)

    @pl.when(((hi - 1) & 1) == 1)
    def _():
        for k in streams:
            sel_scores_into(k, sb_scr, hi - 1)
        for k in streams:
            update(k, sa_scr[k], chunk_off(hi - 2), vst_ref, None)
        for k in streams:
            update(k, sb_scr[k], chunk_off(hi - 1), vst_ref, causal)

    @pl.when(((hi - 1) & 1) == 0)
    def _():
        for k in streams:
            update(k, sa_scr[k], chunk_off(hi - 1), vst_ref, causal)

    sg = jax.nn.sigmoid(gate_ref[...].astype(F32))
    for k in streams:
        o_s = result(k)
        outs = []
        for r in range(GROUP_SIZE):
            sl = slice(r * tq, (r + 1) * tq)
            g0 = 3 * (k * GROUP_SIZE + r)
            o_r = (sg[g0:g0 + 1] * o_c[k][:, sl] + sg[g0 + 1:g0 + 2] * o_s[:, sl]
                   + sg[g0 + 2:g0 + 3] * o_w[k][:, sl])
            outs.append(_t(o_r))
        o_ref[:, k * gw:(k + 1) * gw] = jnp.concatenate(outs, axis=1).astype(o_ref.dtype)


def _nsa(proj, kvc, kvct, vt, e_tab, bsz, seq):
    tq, tk = NSA_TQ, NSA_TK
    assert WINDOW % tq == 0 and tk % tq == 0
    ns = NSA_STREAMS
    assert ns == N_KV_GROUPS
    nq = seq // tq
    nc = kvc.shape[3]
    gw = ns * GROUP_SIZE * HEAD_DIM
    kw_ = ns * HEAD_DIM
    rows = GROUP_SIZE * tq
    n_vs_blocks = KV_DIM // kw_

    once = dict(pipeline_mode=pl.Buffered(1))

    def k_spec(off):
        cb = off // kw_
        return pl.BlockSpec((seq, kw_), lambda b, g, i: (b, cb + g), **once)

    return pl.pallas_call(
        functools.partial(_nsa_kernel, tq=tq, tk=tk, seq=seq),
        grid=(bsz, N_KV_GROUPS // ns, nq),
        in_specs=[pl.BlockSpec((tq, gw), lambda b, g, i: (b * nq + i, g)),
                  pl.BlockSpec((LANES, tq), lambda b, g, i: (2 * KV_DIM // LANES, b * nq + i)),
                  pl.BlockSpec((None, None, ns, nc, HEAD_DIM), lambda b, g, i: (0, b, g, 0, 0)),
                  pl.BlockSpec((None, None, ns, HEAD_DIM, nc), lambda b, g, i: (1, b, g, 0, 0)),
                  k_spec(OFF_KS),
                  pl.BlockSpec((seq, LANES), lambda b, g, i: (0, 0), **once),
                  pl.BlockSpec((kw_, seq), lambda b, g, i: (g, b), **once),
                  k_spec(OFF_KW),
                  pl.BlockSpec((kw_, seq), lambda b, g, i: (n_vs_blocks + g, b), **once)],
        out_specs=pl.BlockSpec((tq, gw), lambda b, g, i: (b * nq + i, g)),
        out_shape=jax.ShapeDtypeStruct((bsz * seq, D_MODEL), BF16),
        scratch_shapes=[pltpu.VMEM((ns, tk, rows), F32), pltpu.VMEM((ns, tk, rows), F32),
                        pltpu.VMEM((ns, 1, rows), F32),
                        pltpu.VMEM((ns, HEAD_DIM + ONES_ROWS, rows), F32)],
        compiler_params=_cparams(("parallel", "parallel", "arbitrary")),
        name="nsa",
    )(proj, vt, kvc, kvct, proj, e_tab, vt, proj, vt)


def _s5_kernel(u_ref, pg_ref, bt_ref, pl_ref, dsk_ref, wgl_ref, wou_ref, o_ref, wgl_o, wou_o,
               wt_scr, ws_scr, wo_scr, xp_scr, tok_scr, y_scr, *, chunks_per_seq):
    L = SSM_CHUNK
    hp = lax.Precision.HIGHEST
    wgl_o[...] = wgl_ref[...].astype(wgl_o.dtype)
    wou_o[...] = wou_ref[...].astype(wou_o.dtype)
    are_g, aim_g, ldt_g = pg_ref[0], pg_ref[1], pg_ref[2]
    bt = bt_ref[...]
    are_l, aim_l, ldt_l, cr_l, ci_l = pl_ref[0], pl_ref[1], pl_ref[2], pl_ref[3], pl_ref[4]

    lane = lax.broadcasted_iota(jnp.int32, (LANES, LANES), 1)
    rowi = lax.broadcasted_iota(jnp.int32, (LANES, LANES), 0)
    is_re = lane < SSM_STATE
    row_grp = rowi // SSM_GROUP
    lane_grp = lane // SSM_GROUP

    def rows16(z):
        return jnp.concatenate([jnp.broadcast_to(z[g:g + 1], (SSM_GROUP, LANES))
                                for g in range(GROUPS_PER_BLOCK)], axis=0)

    def cmul(x, zr, zi):
        return x * zr + pltpu.roll(x, SSM_STATE, axis=1) * jnp.where(is_re, -zi, zi)

    dt_g = jnp.exp(ldt_g)
    l1r = jnp.exp(are_g * dt_g) * jnp.cos(aim_g * dt_g)
    l1i = jnp.exp(are_g * dt_g) * jnp.sin(aim_g * dt_g)
    den = are_g * are_g + aim_g * aim_g
    nr = l1r - 1.0
    coef_r = (nr * are_g + l1i * aim_g) / den
    coef_i = (l1i * are_g - nr * aim_g) / den
    bbar = cmul(bt, rows16(coef_r), rows16(coef_i))

    pw_g = [(jnp.ones_like(l1r), jnp.zeros_like(l1r))]
    for _ in range(L):
        pr, pi = pw_g[-1]
        pw_g.append((pr * l1r - pi * l1i, pr * l1i + pi * l1r))

    dt_l = jnp.exp(ldt_l)
    m1r = jnp.exp(are_l * dt_l) * jnp.cos(aim_l * dt_l)
    m1i = jnp.exp(are_l * dt_l) * jnp.sin(aim_l * dt_l)
    gs = []
    pr, pi = jnp.ones_like(m1r), jnp.zeros_like(m1r)
    for _ in range(L + 1):
        gs.append(jnp.concatenate([cr_l * pr - ci_l * pi, -(cr_l * pi + ci_l * pr)], axis=0))
        pr, pi = pr * m1r - pi * m1i, pr * m1i + pi * m1r

    same_grp = row_grp == lane_grp
    zeros_tile = jnp.zeros((LANES, LANES), BF16)

    for t in range(L):
        d_t = jnp.dot(bbar, gs[t], precision=hp, preferred_element_type=F32)
        d_t = jnp.where(same_grp, d_t, 0.0).astype(BF16)
        for j in range(L - t):
            i = j + t
            wt_scr[j * LANES:(j + 1) * LANES, i * LANES:(i + 1) * LANES] = d_t
    for p2 in range(L // 2):
        wt_scr[(2 * p2 + 1) * LANES:(2 * p2 + 2) * LANES, 2 * p2 * LANES:(2 * p2 + 1) * LANES] = zeros_tile

    ws_scr[...] = jnp.zeros(ws_scr.shape, BF16)
    for j in range(L):
        zr, zi = pw_g[L - 1 - j]
        s_j = cmul(bbar, rows16(zr), rows16(zi)).astype(BF16)
        for g in range(GROUPS_PER_BLOCK):
            r0 = j * LANES + g * SSM_GROUP
            ws_scr[r0:r0 + SSM_GROUP, g * LANES:(g + 1) * LANES] = s_j[g * SSM_GROUP:(g + 1) * SSM_GROUP]

    for g in range(GROUPS_PER_BLOCK):
        for i in range(L):
            wo_scr[g * LANES:(g + 1) * LANES, i * LANES:(i + 1) * LANES] = (
                jnp.where(lane_grp == g, gs[i + 1], 0.0).astype(BF16))

    tok_scr[...] = u_ref[...].astype(F32)
    nrow = u_ref.shape[0] // L
    u = jnp.concatenate([tok_scr[pl.ds(j, nrow, stride=L), :].astype(BF16) for j in range(L)], axis=1)
    st = jnp.dot(u, ws_scr[...], preferred_element_type=F32)

    kk = lax.broadcasted_iota(jnp.int32, (nrow, 1), 0) & (chunks_per_seq - 1)
    lane1 = lax.broadcasted_iota(jnp.int32, (1, LANES), 1)
    assert L // 2 == GROUPS_PER_BLOCK
    for g in range(GROUPS_PER_BLOCK):
        cols = slice(2 * g * LANES, (2 * g + 2) * LANES)
        kr = (2 * g + 2) * LANES
        y_scr[:, cols] = jnp.dot(u[:, :kr], wt_scr[:kr, cols], preferred_element_type=F32)

        xs = st[:, g * LANES:(g + 1) * LANES]
        zr = pw_g[L][0][g:g + 1]
        zi = pw_g[L][1][g:g + 1]
        d = 1
        while d < chunks_per_seq:
            sh = jnp.where(kk >= d, pltpu.roll(xs, d, axis=0), 0.0)
            zmix = jnp.where(lane1 < SSM_STATE, -zi, zi)
            xs = xs + sh * zr + pltpu.roll(sh, SSM_STATE, axis=1) * zmix
            zr, zi = zr * zr - zi * zi, 2.0 * zr * zi
            d *= 2
        prev = jnp.where(kk >= 1, pltpu.roll(xs, 1, axis=0), 0.0)
        xp_scr[:, g * LANES:(g + 1) * LANES] = prev.astype(BF16)

    xp = xp_scr[...]
    for p2 in range(L // 2):
        cols = slice(2 * p2 * LANES, (2 * p2 + 2) * LANES)
        y = y_scr[:, cols] + jnp.dot(xp, wo_scr[:, cols], preferred_element_type=F32)
        y = jax.nn.gelu(y + dsk_ref[:, cols] * u[:, cols].astype(F32))
        for i in range(2):
            tok_scr[pl.ds(2 * p2 + i, nrow, stride=L), :] = y[:, i * LANES:(i + 1) * LANES]
    o_ref[...] = tok_scr[...].astype(o_ref.dtype)


def _s5(proj, p_grp, p_bt, p_lanes, dsk, w_glu, w_out, chunks_per_seq):
    m = proj.shape[0]
    nrow, kk = m // SSM_CHUNK, SSM_CHUNK * LANES
    nst = GROUPS_PER_BLOCK * LANES
    ws_ = D_MODEL // N_SSM_BLOCKS
    wspec = pl.BlockSpec((ws_, D_MODEL), lambda i: (i, 0))
    return pl.pallas_call(
        functools.partial(_s5_kernel, chunks_per_seq=chunks_per_seq),
        grid=(N_SSM_BLOCKS,),
        in_specs=[pl.BlockSpec((m, LANES), lambda i: (0, OFF_SSM // LANES + i)),
                  pl.BlockSpec((None, 3, GROUPS_PER_BLOCK, LANES), lambda i: (i, 0, 0, 0)),
                  pl.BlockSpec((None, LANES, LANES), lambda i: (i, 0, 0)),
                  pl.BlockSpec((None, 5, SSM_STATE, LANES), lambda i: (i, 0, 0, 0)),
                  pl.BlockSpec((None, 1, kk), lambda i: (i, 0, 0)),
                  wspec, wspec],
        out_specs=[pl.BlockSpec((m, LANES), lambda i: (0, i)), wspec, wspec],
        out_shape=[jax.ShapeDtypeStruct((m, D_MODEL), BF16),
                   jax.ShapeDtypeStruct((D_MODEL, D_MODEL), BF16), jax.ShapeDtypeStruct((D_MODEL, D_MODEL), BF16)],
        scratch_shapes=[pltpu.VMEM((kk, kk), BF16), pltpu.VMEM((kk, nst), BF16),
                        pltpu.VMEM((nst, kk), BF16), pltpu.VMEM((nrow, nst), BF16),
                        pltpu.VMEM((m, LANES), F32), pltpu.VMEM((nrow, kk), F32)],
        compiler_params=_cparams(("parallel",)),
        name="s5",
    )(proj, p_grp, p_bt, p_lanes, dsk, w_glu, w_out)


def _mix_kernel(ys_ref, ya_ref, ga_ref, gb_ref, x_ref, wg_ref, bg_ref, wo_ref, gt_ref, gm_ref, sc_ref, sh_ref,
                wu_ref, wd_ref, h_ref, u_ref, wu_o, wd_o):
    wu_o[...] = wu_ref[...].astype(wu_o.dtype)
    wd_o[...] = wd_ref[...].astype(wd_o.dtype)
    ys = ys_ref[...]
    z = jnp.dot(ys, wg_ref[...], preferred_element_type=F32) + bg_ref[...]
    yb = ys.astype(F32) * jax.nn.sigmoid(z)
    mix = (jax.nn.sigmoid(ga_ref[...].astype(F32)) * ya_ref[...].astype(F32)
           + jax.nn.sigmoid(gb_ref[...].astype(F32)) * yb)
    h = x_ref[...] + gt_ref[...] * jnp.dot(mix.astype(BF16), wo_ref[...], preferred_element_type=F32)
    h_ref[...] = h
    ms = jnp.mean(h * h, axis=-1, keepdims=True)
    y = h * lax.rsqrt(ms + EPS) * gm_ref[...]
    u_ref[...] = (y * (1.0 + sc_ref[...]) + sh_ref[...]).astype(u_ref.dtype)


def _mix(ys, ya, proj, x2, w_glu, b_glu, w_out, g_mlp, ada3, w_up, w_down, seq):
    m = ys.shape[0]
    tm = MIX_TM
    per_b = seq // tm
    ff = w_up.shape[1]
    fs = ff // (m // tm)
    const = dict(pipeline_mode=pl.Buffered(1))
    row = lambda cb: pl.BlockSpec((tm, D_MODEL), lambda i: (i, cb))

    def ada_spec(k):
        return pl.BlockSpec((None, 1, D_MODEL), lambda i: ((i // per_b) * 6 + k, 0, 0))

    return pl.pallas_call(
        _mix_kernel,
        grid=(m // tm,),
        in_specs=[row(0), row(0), row(OFF_GA // D_MODEL), row(OFF_GB // D_MODEL), row(0),
                  pl.BlockSpec((D_MODEL, D_MODEL), lambda i: (0, 0), **const),
                  pl.BlockSpec((1, D_MODEL), lambda i: (0, 0)),
                  pl.BlockSpec((D_MODEL, D_MODEL), lambda i: (0, 0), **const),
                  ada_spec(2),
                  pl.BlockSpec((1, D_MODEL), lambda i: (0, 0)),
                  ada_spec(4), ada_spec(3),
                  pl.BlockSpec((D_MODEL, fs), lambda i: (0, i)),
                  pl.BlockSpec((fs, D_MODEL), lambda i: (i, 0))],
        out_specs=[row(0), row(0),
                   pl.BlockSpec((D_MODEL, fs), lambda i: (0, i)),
                   pl.BlockSpec((fs, D_MODEL), lambda i: (i, 0))],
        out_shape=[jax.ShapeDtypeStruct((m, D_MODEL), F32), jax.ShapeDtypeStruct((m, D_MODEL), BF16),
                   jax.ShapeDtypeStruct((D_MODEL, ff), BF16), jax.ShapeDtypeStruct((ff, D_MODEL), BF16)],
        compiler_params=_cparams(("parallel",)),
        name="mix",
    )(ys, ya, proj, proj, x2, w_glu, b_glu, w_out, ada3, g_mlp, ada3, ada3, w_up, w_down)


def _mlp_kernel(h_ref, u_ref, gt_ref, wu_ref, wd_ref, gf_ref, o_ref, acc_scr):
    f = pl.program_id(1)

    @pl.when(f == 0)
    def _():
        acc_scr[...] = jnp.zeros_like(acc_scr)

    a = jnp.dot(u_ref[...], wu_ref[...], preferred_element_type=F32)
    a = jnp.square(jnp.maximum(a, 0.0))
    acc_scr[...] += jnp.dot(a.astype(BF16), wd_ref[...], preferred_element_type=F32)

    @pl.when(f == pl.num_programs(1) - 1)
    def _():
        h2 = h_ref[...] + gt_ref[...] * acc_scr[...]
        ms = jnp.mean(h2 * h2, axis=-1, keepdims=True)
        o_ref[...] = h2 * lax.rsqrt(ms + EPS) * gf_ref[...]


def _mlp(h1, u2, ada3, w_up, w_down, g_final, seq):
    m = h1.shape[0]
    tm, tf = MLP_TM, MLP_TF
    per_b = seq // tm
    ff = w_up.shape[1]
    return pl.pallas_call(
        _mlp_kernel,
        grid=(m // tm, ff // tf),
        in_specs=[pl.BlockSpec((tm, D_MODEL), lambda i, f: (i, 0)),
                  pl.BlockSpec((tm, D_MODEL), lambda i, f: (i, 0)),
                  pl.BlockSpec((None, 1, D_MODEL), lambda i, f: ((i // per_b) * 6 + 5, 0, 0)),
                  pl.BlockSpec((D_MODEL, tf), lambda i, f: (0, f)),
                  pl.BlockSpec((tf, D_MODEL), lambda i, f: (f, 0)),
                  pl.BlockSpec((1, D_MODEL), lambda i, f: (0, 0))],
        out_specs=pl.BlockSpec((tm, D_MODEL), lambda i, f: (i, 0)),
        out_shape=jax.ShapeDtypeStruct((m, D_MODEL), F32),
        scratch_shapes=[pltpu.VMEM((tm, D_MODEL), F32)],
        compiler_params=_cparams(("parallel", "arbitrary")),
        name="mlp",
    )(h1, u2, ada3, w_up, w_down, g_final)


def _gate_rows(wt):
    return jnp.pad(wt[RAW_GATE:RAW_SSM], ((0, LANES - N_GATES), (0, 0))).astype(BF16)


def _s5_params(a_re, a_im, log_dt, b_re, b_im, c_re, c_im, d_skip):
    nb, gp, p, cg = N_SSM_BLOCKS, GROUPS_PER_BLOCK, SSM_STATE, SSM_GROUP
    ldt = jnp.broadcast_to(log_dt[:, None], (SSM_NGROUPS, p))

    def grp(a):
        return jnp.concatenate([a, a], axis=-1).reshape(nb, gp, LANES)

    def lanes(a):
        a = jnp.broadcast_to(a.reshape(nb, gp, p, 1).transpose(0, 2, 1, 3), (nb, p, gp, cg))
        return a.reshape(nb, p, LANES)

    def ct(cm):
        return cm.reshape(nb, gp, cg, p).transpose(0, 3, 1, 2).reshape(nb, p, LANES)

    p_grp = jnp.stack([grp(a_re), grp(a_im), grp(ldt)], axis=1)
    p_bt = jnp.stack([b_re, b_im], axis=1).transpose(0, 3, 1, 2).reshape(nb, LANES, LANES)
    p_lanes = jnp.stack([lanes(a_re), lanes(a_im), lanes(ldt), ct(c_re), ct(c_im)], axis=1)
    dsk = jnp.tile(d_skip.reshape(nb, 1, LANES), (1, 1, SSM_CHUNK))
    return p_grp, p_bt, p_lanes, dsk


def kernel(x, c, w_ada, b_ada, g_mix, w_in, w_ck1, w_ck2, pe_ck, w_cv1, w_cv2, pe_cv,
           a_re, a_im, log_dt, b_re, b_im, c_re, c_im, d_skip, w_glu, b_glu,
           w_out, g_mlp, w_up, w_down, g_final):
    bsz, seq, _ = x.shape
    assert w_ada.shape[0] == 1, "single-layer block"
    assert seq % 1024 == 0 and (seq & (seq - 1)) == 0
    m = bsz * seq
    x2 = x.reshape(m, D_MODEL)

    c_pad = jnp.pad(c, ((0, SUBLANES - bsz), (0, 0)))
    ada = _ada(c_pad, w_ada[0], b_ada)
    ada3 = ada[:bsz].reshape(bsz * 6, 1, D_MODEL)

    w_in_t = w_in[0].T
    proj, vt = _inproj(x2, g_mix, ada3, w_in_t, _gate_rows(w_in_t), seq)

    w1s = jnp.stack([w_ck1[0], w_cv1[0]]).reshape(2, CMP_LEN * HEAD_DIM, HEAD_DIM).astype(BF16)
    pes = jnp.stack([pe_ck[0], pe_cv[0]]).reshape(2, 1, CMP_LEN * HEAD_DIM)
    pes = jnp.pad(pes, ((0, 0), (0, SUBLANES - 1), (0, 0))).astype(BF16)
    w2s = jnp.stack([w_ck2[0], w_cv2[0]]).astype(BF16)
    kvc, kvct = _compress(proj, w1s, pes, w2s, bsz, seq)

    e_tab = (jnp.arange(seq)[:, None] // SEL_BLOCK == jnp.arange(LANES)[None, :]).astype(BF16)
    y_a = _nsa(proj, kvc, kvct, vt, e_tab, bsz, seq)

    p_grp, p_bt, p_lanes, dsk = _s5_params(a_re[0], a_im[0], log_dt[0], b_re[0], b_im[0],
                                           c_re[0], c_im[0], d_skip[0])
    ys, w_glu_b, w_out_b = _s5(proj, p_grp, p_bt, p_lanes, dsk, w_glu[0], w_out[0], seq // SSM_CHUNK)

    h1, u2, w_up_b, w_down_b = _mix(ys, y_a, proj, x2, w_glu_b, b_glu, w_out_b, g_mlp, ada3, w_up[0], w_down[0], seq)
    out = _mlp(h1, u2, ada3, w_up_b, w_down_b, g_final.reshape(1, D_MODEL), seq)
    return out.reshape(bsz, seq, D_MODEL)
```

```python
import functools

import jax
import jax.numpy as jnp
from jax import lax
from jax.experimental import pallas as pl
from jax.experimental.pallas import tpu as pltpu

F32 = jnp.float32
BF16 = jnp.bfloat16

D_MODEL = 2048
N_HEADS = 16
HEAD_DIM = 128
N_KV_GROUPS = 4
GROUP_SIZE = 4
KV_DIM = N_KV_GROUPS * HEAD_DIM
CMP_LEN = 32
CMP_STRIDE = 16
SEL_BLOCK = 64
SEL_TOPN = 16
WINDOW = 512
SSM_GROUP = 16
SSM_NGROUPS = D_MODEL // SSM_GROUP
SSM_STATE = 64
EPS = 1e-6
NEG = -1e30
BIG = 1e6
LOG2E = 1.4426950408889634
ONES_ROWS = 16

LANES = 128
SUBLANES = 8
SEL_SHIFT = SEL_BLOCK.bit_length() - 1
assert 1 << SEL_SHIFT == SEL_BLOCK
SSM_CHUNK = 16
GROUPS_PER_BLOCK = LANES // SSM_GROUP
N_SSM_BLOCKS = SSM_NGROUPS // GROUPS_PER_BLOCK

OFF_Q = 0
OFF_SSM = OFF_Q + D_MODEL
OFF_GA = OFF_SSM + D_MODEL
OFF_GB = OFF_GA + D_MODEL
OFF_KC = OFF_GB + D_MODEL
OFF_VC = OFF_KC + KV_DIM
OFF_KS = OFF_VC + KV_DIM
OFF_KW = OFF_KS + KV_DIM
PROJ_PAD = OFF_KW + KV_DIM
N_GATES = 3 * N_HEADS
VT_ROWS = 2 * KV_DIM + LANES
RAW_VS = D_MODEL + 3 * KV_DIM
RAW_KW = RAW_VS + KV_DIM
RAW_VW = RAW_KW + KV_DIM
RAW_GATE = RAW_VW + KV_DIM
RAW_SSM = RAW_GATE + 3 * N_HEADS

VMEM_LIMIT = 56 * 1024 * 1024

ADA_TN = 1024
INPROJ_TM, INPROJ_TN = 1024, 512
NSA_TQ, NSA_TK = 256, 512
MIX_TM = 256
MLP_TM, MLP_TF = 512, 1024


def _cparams(sem):
    return pltpu.CompilerParams(dimension_semantics=sem, vmem_limit_bytes=VMEM_LIMIT)


def _t(x):
    r, c = x.shape
    rows = []
    for j in range(c // LANES):
        rows.append(jnp.concatenate([x[i * LANES:(i + 1) * LANES, j * LANES:(j + 1) * LANES].T
                                     for i in range(r // LANES)], axis=1))
    return jnp.concatenate(rows, axis=0)


def _nt_dot(a, b):
    return lax.dot_general(a, b, (((1,), (1,)), ((), ())), preferred_element_type=F32)


def _ada_kernel(c_ref, w_ref, b_ref, o_ref):
    c = c_ref[...]
    cond = c * jax.nn.sigmoid(c)
    o_ref[...] = jnp.dot(cond.astype(BF16), w_ref[...].astype(BF16),
                         preferred_element_type=F32) + b_ref[...]


def _ada(c_pad, w_ada, b_ada):
    n = w_ada.shape[1]
    tn = ADA_TN
    return pl.pallas_call(
        _ada_kernel,
        grid=(n // tn,),
        in_specs=[pl.BlockSpec((SUBLANES, D_MODEL), lambda j: (0, 0)),
                  pl.BlockSpec((D_MODEL, tn), lambda j: (0, j)),
                  pl.BlockSpec((1, tn), lambda j: (0, j))],
        out_specs=pl.BlockSpec((SUBLANES, tn), lambda j: (0, j)),
        out_shape=jax.ShapeDtypeStruct((SUBLANES, n), F32),
        compiler_params=_cparams(("parallel",)),
        name="ada",
    )(c_pad, w_ada, b_ada)


_Q_TILES = D_MODEL // INPROJ_TN
_TAIL_TILES = 3 * D_MODEL // INPROJ_TN
_ROW_UNIT = 16


def _w_row(j):
    k = j - (_Q_TILES + _TAIL_TILES)
    kv_block = jnp.where(k < 3, 4 + k, 8)
    head = jnp.where(j < _Q_TILES, j, kv_block) * (INPROJ_TN // _ROW_UNIT)
    tail = RAW_SSM // _ROW_UNIT + (j - _Q_TILES) * (INPROJ_TN // _ROW_UNIT)
    return jnp.where((j >= _Q_TILES) & (k < 0), tail, head) * _ROW_UNIT


def _inproj_kernel(x_ref, g_ref, sc_ref, sh_ref, w_ref, wg_ref, wvs_ref, wvw_ref, o_ref, vt_ref, u_scr):
    @pl.when(pl.program_id(1) == 0)
    def _():
        x = x_ref[...]
        ms = jnp.mean(x * x, axis=-1, keepdims=True)
        y = x * lax.rsqrt(ms + EPS) * g_ref[...]
        u = (y * (1.0 + sc_ref[...]) + sh_ref[...]).astype(BF16)
        u_scr[...] = u
        vt_ref[:KV_DIM, :] = _nt_dot(wvs_ref[...].astype(BF16), u).astype(vt_ref.dtype)
        vt_ref[KV_DIM:2 * KV_DIM, :] = _nt_dot(wvw_ref[...].astype(BF16), u).astype(vt_ref.dtype)
        vt_ref[2 * KV_DIM:, :] = _nt_dot(wg_ref[...], u).astype(vt_ref.dtype)

    o_ref[...] = _nt_dot(u_scr[...], w_ref[...].astype(BF16)).astype(o_ref.dtype)


def _inproj(x2, g_mix, ada3, w_in_t, w_gate_t, seq):
    m = x2.shape[0]
    tm, tn = INPROJ_TM, INPROJ_TN
    per_b = seq // tm
    const = dict(pipeline_mode=pl.Buffered(1))
    wblk = (pl.Element(tn), pl.Element(D_MODEL))
    return pl.pallas_call(
        _inproj_kernel,
        grid=(m // tm, PROJ_PAD // tn),
        in_specs=[pl.BlockSpec((tm, D_MODEL), lambda i, j: (i, 0)),
                  pl.BlockSpec((1, D_MODEL), lambda i, j: (0, 0)),
                  pl.BlockSpec((None, 1, D_MODEL), lambda i, j: ((i // per_b) * 6 + 1, 0, 0)),
                  pl.BlockSpec((None, 1, D_MODEL), lambda i, j: ((i // per_b) * 6 + 0, 0, 0)),
                  pl.BlockSpec(wblk, lambda i, j: (_w_row(j), 0)),
                  pl.BlockSpec((LANES, D_MODEL), lambda i, j: (0, 0), **const),
                  pl.BlockSpec((tn, D_MODEL), lambda i, j: (RAW_VS // tn, 0), **const),
                  pl.BlockSpec((tn, D_MODEL), lambda i, j: (RAW_VW // tn, 0), **const)],
        out_specs=[pl.BlockSpec((tm, tn), lambda i, j: (i, j)),
                   pl.BlockSpec((VT_ROWS, tm), lambda i, j: (0, i))],
        out_shape=[jax.ShapeDtypeStruct((m, PROJ_PAD), BF16),
                   jax.ShapeDtypeStruct((VT_ROWS, m), BF16)],
        scratch_shapes=[pltpu.VMEM((tm, D_MODEL), BF16)],
        compiler_params=_cparams(("parallel", "arbitrary")),
        name="inproj",
    )(x2, g_mix, ada3, ada3, w_in_t, w_gate_t, w_in_t, w_in_t)


def _compress_kernel(x_ref, w1_ref, pe_ref, w2_ref, o_ref, ot_ref, xf_scr):
    nc = x_ref.shape[0] // CMP_STRIDE
    half = CMP_STRIDE * HEAD_DIM
    w1 = w1_ref[...]
    pe_t = jnp.dot(pe_ref[...], w1, preferred_element_type=F32)[0:1]
    row = lax.broadcasted_iota(jnp.int32, (nc, HEAD_DIM), 0)
    for g in range(N_KV_GROUPS):
        xf_scr[g] = x_ref[:, g * HEAD_DIM:(g + 1) * HEAD_DIM].astype(F32)
        x = jnp.concatenate([xf_scr[g, pl.ds(l, nc, stride=CMP_STRIDE), :].astype(BF16)
                             for l in range(CMP_STRIDE)], axis=1)
        ha = jnp.dot(x, w1[:half], preferred_element_type=F32)
        hb = jnp.dot(x, w1[half:], preferred_element_type=F32)
        hb = pltpu.roll(hb, nc - 1, axis=0)
        hid = jax.nn.gelu(ha + hb + pe_t)
        out = jnp.dot(hid.astype(BF16), w2_ref[...], preferred_element_type=F32)
        out = jnp.where(row < nc - 1, out, 0.0)
        o_ref[g] = out.astype(o_ref.dtype)
        ot_ref[g] = _t(out).astype(ot_ref.dtype)


def _compress(proj, w1s, pes, w2s, bsz, seq):
    nc = seq // CMP_STRIDE
    g = N_KV_GROUPS
    return pl.pallas_call(
        _compress_kernel,
        grid=(2, bsz),
        in_specs=[pl.BlockSpec((seq, KV_DIM), lambda a, i: (i, OFF_KC // KV_DIM + a)),
                  pl.BlockSpec((None, CMP_LEN * HEAD_DIM, HEAD_DIM), lambda a, i: (a, 0, 0)),
                  pl.BlockSpec((None, SUBLANES, CMP_LEN * HEAD_DIM), lambda a, i: (a, 0, 0)),
                  pl.BlockSpec((None, HEAD_DIM, HEAD_DIM), lambda a, i: (a, 0, 0))],
        out_specs=[pl.BlockSpec((None, None, g, nc, HEAD_DIM), lambda a, i: (a, i, 0, 0, 0)),
                   pl.BlockSpec((None, None, g, HEAD_DIM, nc), lambda a, i: (a, i, 0, 0, 0))],
        out_shape=[jax.ShapeDtypeStruct((2, bsz, g, nc, HEAD_DIM), BF16),
                   jax.ShapeDtypeStruct((2, bsz, g, HEAD_DIM, nc), BF16)],
        scratch_shapes=[pltpu.VMEM((g, seq, HEAD_DIM), F32)],
        compiler_params=_cparams(("parallel", "parallel")),
        name="compress",
    )(proj, w1s, pes, w2s)


NSA_STREAMS = 4


def _nsa_kernel(q_ref, gate_ref, kc_ref, vct_ref, ks_ref, e_ref, vst_ref, kw_ref, vwt_ref, o_ref,
                sa_scr, sb_scr, m_scr, acc_scr, *, tq, tk, seq):
    s0 = pl.program_id(2) * tq
    rows = GROUP_SIZE * tq
    gw = GROUP_SIZE * HEAD_DIM
    nsel = seq // SEL_BLOCK
    nc = kc_ref.shape[1]
    topn = min(SEL_TOPN, nsel)
    qscale = HEAD_DIM ** -0.5 * LOG2E
    streams = range(NSA_STREAMS)
    hd = lambda k: slice(k * HEAD_DIM, (k + 1) * HEAD_DIM)
    t_lane = s0 + (lax.broadcasted_iota(jnp.int32, (1, rows), 1) & (tq - 1))

    def reset_stats(k):
        m_scr[k] = jnp.full((1, rows), NEG, F32)
        acc_scr[k] = jnp.zeros((HEAD_DIM + ONES_ROWS, rows), F32)

    def update(k, s, off, vt_ref, mask):
        n = s.shape[0]
        if mask is not None:
            kpos = off + lax.broadcasted_iota(jnp.int32, (n, 1), 0)
            s = jnp.where(mask(kpos), s, NEG)
        m = m_scr[k]
        m_new = jnp.maximum(m, jnp.max(s, axis=0, keepdims=True))
        p = jnp.exp2(s - m_new).astype(BF16)
        vt1 = jnp.concatenate([vt_ref[hd(k), pl.ds(off, n)], jnp.ones((ONES_ROWS, n), BF16)], axis=0)
        acc_scr[k] = jnp.exp2(m - m_new) * acc_scr[k] + jnp.dot(vt1, p, preferred_element_type=F32)
        m_scr[k] = m_new

    def result(k):
        acc = acc_scr[k]
        return acc[:HEAD_DIM] * (1.0 / jnp.maximum(acc[HEAD_DIM:HEAD_DIM + 1], 1e-30))

    q4 = []
    for k in streams:
        q = q_ref[:, k * gw:(k + 1) * gw]
        qk = jnp.concatenate([q[:, hd(r)] for r in range(GROUP_SIZE)], axis=0)
        q4.append((qk.astype(F32) * qscale).astype(BF16))

    causal = lambda kpos: kpos <= t_lane

    for k in streams:
        reset_stats(k)
    n_back = WINDOW // tq
    win = []
    for c in range(n_back, 0, -1):
        off = s0 - c * tq
        if c == n_back:
            lo_edge = jnp.where(off >= 0, t_lane - (WINDOW - 1), seq)
        else:
            lo_edge = jnp.where(off >= 0, 0, seq)
        win.append((pl.multiple_of(jnp.maximum(off, 0), tq), lambda kpos, e=lo_edge: kpos >= e))
    win.append((pl.multiple_of(s0, tq), causal))

    def win_scores_into(k, dst, off):
        dst[k, :tq, :] = _nt_dot(kw_ref[pl.ds(off, tq), hd(k)], q4[k])

    bufs = (sa_scr, sb_scr)
    for k in streams:
        win_scores_into(k, bufs[0], win[0][0])
    for c, (off, mask) in enumerate(win):
        if c + 1 < len(win):
            for k in streams:
                win_scores_into(k, bufs[(c + 1) % 2], win[c + 1][0])
        for k in streams:
            update(k, bufs[c % 2][k, :tq, :], off, vwt_ref, mask)
    o_w = [result(k) for k in streams]

    n_col = lax.broadcasted_iota(jnp.int32, (nc, 1), 0)
    valid_c = (n_col * CMP_STRIDE + (CMP_LEN - 1)) <= t_lane
    jj = lax.broadcasted_iota(jnp.int32, (nsel, nc), 0)
    nn = lax.broadcasted_iota(jnp.int32, (nsel, nc), 1)
    ov = ((nn * CMP_STRIDE < jj * SEL_BLOCK + SEL_BLOCK)
          & (nn * CMP_STRIDE + CMP_LEN > jj * SEL_BLOCK)
          & (nn < nc - 1))
    ov = jnp.where(ov, 1.0, 0.0).astype(BF16)
    jcol = lax.broadcasted_iota(jnp.int32, (nsel, tq), 0)
    cur = lax.shift_right_logical(s0 + lax.broadcasted_iota(jnp.int32, (nsel, tq), 1), SEL_SHIFT)
    valid = jcol <= cur
    forced = (jcol == 0) | (jcol == cur) | (jcol == cur - 1)
    o_c, score = [], []
    for k in streams:
        sc = jnp.where(valid_c, _nt_dot(kc_ref[k], q4[k]), NEG)
        mc = jnp.max(sc, axis=0, keepdims=True)
        pc = jnp.exp2(sc - mc)
        inv = 1.0 / jnp.maximum(jnp.sum(pc, axis=0, keepdims=True), 1e-30)
        pc = pc * jnp.where(t_lane >= CMP_LEN - 1, inv, 0.0)
        o_c.append(jnp.dot(vct_ref[k], pc.astype(BF16), preferred_element_type=F32))
        psum = pc[:, 0:tq] + pc[:, tq:2 * tq] + pc[:, 2 * tq:3 * tq] + pc[:, 3 * tq:4 * tq]
        p_hi = psum.astype(BF16)
        p_lo = (psum - p_hi.astype(F32)).astype(BF16)
        imp_t = (jnp.dot(ov, p_hi, preferred_element_type=F32)
                 + jnp.dot(ov, p_lo, preferred_element_type=F32))
        score.append(jnp.where(forced, BIG, jnp.where(valid, imp_t, -BIG)))

    jcol_f = jcol.astype(F32)
    picked = [jnp.zeros((nsel, tq), F32) for _ in streams]
    for _ in range(topn):
        for k in streams:
            best = jnp.max(score[k], axis=0, keepdims=True)
            first = jnp.min(jnp.where(score[k] == best, jcol_f, float(nsel)), axis=0, keepdims=True)
            hit = jcol_f == first
            picked[k] = jnp.where(hit, 1.0, picked[k])
            score[k] = jnp.where(hit, -jnp.inf, score[k])

    q_aug = []
    for k in streams:
        bias_t = jnp.where((picked[k] > 0.5) & valid, 0.0, NEG)
        bias_t = jnp.concatenate([bias_t, jnp.zeros((LANES - nsel, tq), F32)], axis=0)
        bias_q = _t(bias_t).astype(BF16)
        q_aug.append(jnp.concatenate([q4[k], jnp.concatenate([bias_q] * GROUP_SIZE, axis=0)], axis=1))

    def chunk_off(kb):
        return pl.multiple_of(kb * tk, tk)

    def sel_scores_into(k, dst, kb):
        off = chunk_off(kb)
        keys = jnp.concatenate([ks_ref[pl.ds(off, tk), hd(k)], e_ref[pl.ds(off, tk), :]], axis=1)
        dst[k] = _nt_dot(keys, q_aug[k])

    hi = (s0 + tq - 1) // tk + 1
    for k in streams:
        reset_stats(k)
    for k in streams:
        sel_scores_into(k, sa_scr, 0)
    n_pairs = lax.shift_right_logical(hi - 1, 1)

    def pair(i, c):
        k0 = 2 * i
        for k in streams:
            sel_scores_into(k, sb_scr, k0 + 1)
        for k in streams:
            update(k, sa_scr[k], chunk_off(k0), vst_ref, None)
        for k in streams:
            sel_scores_into(k, sa_scr, k0 + 2)
        for k in streams:
            update(k, sb_scr[k], chunk_off(k0 + 1), vst_ref, None)
        return c

    lax.fori_loop(0, n_pairs, pair, 0)

    @pl.when(((hi - 1) & 1) == 1)
    def _():
        for k in streams:
            sel_scores_into(k, sb_scr, hi - 1)
        for k in streams:
            update(k, sa_scr[k], chunk_off(hi - 2), vst_ref, None)
        for k in streams:
            update(k, sb_scr[k], chunk_off(hi - 1), vst_ref, causal)

    @pl.when(((hi - 1) & 1) == 0)
    def _():
        for k in streams:
            update(k, sa_scr[k], chunk_off(hi - 1), vst_ref, causal)

    sg = jax.nn.sigmoid(gate_ref[...].astype(F32))
    for k in streams:
        o_s = result(k)
        outs = []
        for r in range(GROUP_SIZE):
            sl = slice(r * tq, (r + 1) * tq)
            g0 = 3 * (k * GROUP_SIZE + r)
            o_r = (sg[g0:g0 + 1] * o_c[k][:, sl] + sg[g0 + 1:g0 + 2] * o_s[:, sl]
                   + sg[g0 + 2:g0 + 3] * o_w[k][:, sl])
            outs.append(_t(o_r))
        o_ref[:, k * gw:(k + 1) * gw] = jnp.concatenate(outs, axis=1).astype(o_ref.dtype)


def _nsa(proj, kvc, kvct, vt, e_tab, bsz, seq):
    tq, tk = NSA_TQ, NSA_TK
    assert WINDOW % tq == 0 and tk % tq == 0
    ns = NSA_STREAMS
    assert ns == N_KV_GROUPS
    nq = seq // tq
    nc = kvc.shape[3]
    gw = ns * GROUP_SIZE * HEAD_DIM
    kw_ = ns * HEAD_DIM
    rows = GROUP_SIZE * tq
    n_vs_blocks = KV_DIM // kw_

    once = dict(pipeline_mode=pl.Buffered(1))

    def k_spec(off):
        cb = off // kw_
        return pl.BlockSpec((seq, kw_), lambda b, g, i: (b, cb + g), **once)

    return pl.pallas_call(
        functools.partial(_nsa_kernel, tq=tq, tk=tk, seq=seq),
        grid=(bsz, N_KV_GROUPS // ns, nq),
        in_specs=[pl.BlockSpec((tq, gw), lambda b, g, i: (b * nq + i, g)),
                  pl.BlockSpec((LANES, tq), lambda b, g, i: (2 * KV_DIM // LANES, b * nq + i)),
                  pl.BlockSpec((None, None, ns, nc, HEAD_DIM), lambda b, g, i: (0, b, g, 0, 0)),
                  pl.BlockSpec((None, None, ns, HEAD_DIM, nc), lambda b, g, i: (1, b, g, 0, 0)),
                  k_spec(OFF_KS),
                  pl.BlockSpec((seq, LANES), lambda b, g, i: (0, 0), **once),
                  pl.BlockSpec((kw_, seq), lambda b, g, i: (g, b), **once),
                  k_spec(OFF_KW),
                  pl.BlockSpec((kw_, seq), lambda b, g, i: (n_vs_blocks + g, b), **once)],
        out_specs=pl.BlockSpec((tq, gw), lambda b, g, i: (b * nq + i, g)),
        out_shape=jax.ShapeDtypeStruct((bsz * seq, D_MODEL), BF16),
        scratch_shapes=[pltpu.VMEM((ns, tk, rows), F32), pltpu.VMEM((ns, tk, rows), F32),
                        pltpu.VMEM((ns, 1, rows), F32),
                        pltpu.VMEM((ns, HEAD_DIM + ONES_ROWS, rows), F32)],
        compiler_params=_cparams(("parallel", "parallel", "arbitrary")),
        name="nsa",
    )(proj, vt, kvc, kvct, proj, e_tab, vt, proj, vt)


def _s5_kernel(u_ref, pg_ref, bt_ref, pl_ref, dsk_ref, wgl_ref, wou_ref, o_ref, wgl_o, wou_o,
               wt_scr, ws_scr, wo_scr, xp_scr, tok_scr, y_scr, *, chunks_per_seq):
    L = SSM_CHUNK
    hp = lax.Precision.HIGHEST
    wgl_o[...] = wgl_ref[...].astype(wgl_o.dtype)
    wou_o[...] = wou_ref[...].astype(wou_o.dtype)
    are_g, aim_g, ldt_g = pg_ref[0], pg_ref[1], pg_ref[2]
    bt = bt_ref[...]
    are_l, aim_l, ldt_l, cr_l, ci_l = pl_ref[0], pl_ref[1], pl_ref[2], pl_ref[3], pl_ref[4]

    lane = lax.broadcasted_iota(jnp.int32, (LANES, LANES), 1)
    rowi = lax.broadcasted_iota(jnp.int32, (LANES, LANES), 0)
    is_re = lane < SSM_STATE
    row_grp = rowi // SSM_GROUP
    lane_grp = lane // SSM_GROUP

    def rows16(z):
        return jnp.concatenate([jnp.broadcast_to(z[g:g + 1], (SSM_GROUP, LANES))
                                for g in range(GROUPS_PER_BLOCK)], axis=0)

    def cmul(x, zr, zi):
        return x * zr + pltpu.roll(x, SSM_STATE, axis=1) * jnp.where(is_re, -zi, zi)

    dt_g = jnp.exp(ldt_g)
    l1r = jnp.exp(are_g * dt_g) * jnp.cos(aim_g * dt_g)
    l1i = jnp.exp(are_g * dt_g) * jnp.sin(aim_g * dt_g)
    den = are_g * are_g + aim_g * aim_g
    nr = l1r - 1.0
    coef_r = (nr * are_g + l1i * aim_g) / den
    coef_i = (l1i * are_g - nr * aim_g) / den
    bbar = cmul(bt, rows16(coef_r), rows16(coef_i))

    pw_g = [(jnp.ones_like(l1r), jnp.zeros_like(l1r))]
    for _ in range(L):
        pr, pi = pw_g[-1]
        pw_g.append((pr * l1r - pi * l1i, pr * l1i + pi * l1r))

    dt_l = jnp.exp(ldt_l)
    m1r = jnp.exp(are_l * dt_l) * jnp.cos(aim_l * dt_l)
    m1i = jnp.exp(are_l * dt_l) * jnp.sin(aim_l * dt_l)
    gs = []
    pr, pi = jnp.ones_like(m1r), jnp.zeros_like(m1r)
    for _ in range(L + 1):
        gs.append(jnp.concatenate([cr_l * pr - ci_l * pi, -(cr_l * pi + ci_l * pr)], axis=0))
        pr, pi = pr * m1r - pi * m1i, pr * m1i + pi * m1r

    same_grp = row_grp == lane_grp
    zeros_tile = jnp.zeros((LANES, LANES), BF16)

    for t in range(L):
        d_t = jnp.dot(bbar, gs[t], precision=hp, preferred_element_type=F32)
        d_t = jnp.where(same_grp, d_t, 0.0).astype(BF16)
        for j in range(L - t):
            i = j + t
            wt_scr[j * LANES:(j + 1) * LANES, i * LANES:(i + 1) * LANES] = d_t
    for p2 in range(L // 2):
        wt_scr[(2 * p2 + 1) * LANES:(2 * p2 + 2) * LANES, 2 * p2 * LANES:(2 * p2 + 1) * LANES] = zeros_tile

    ws_scr[...] = jnp.zeros(ws_scr.shape, BF16)
    for j in range(L):
        zr, zi = pw_g[L - 1 - j]
        s_j = cmul(bbar, rows16(zr), rows16(zi)).astype(BF16)
        for g in range(GROUPS_PER_BLOCK):
            r0 = j * LANES + g * SSM_GROUP
            ws_scr[r0:r0 + SSM_GROUP, g * LANES:(g + 1) * LANES] = s_j[g * SSM_GROUP:(g + 1) * SSM_GROUP]

    for g in range(GROUPS_PER_BLOCK):
        for i in range(L):
            wo_scr[g * LANES:(g + 1) * LANES, i * LANES:(i + 1) * LANES] = (
                jnp.where(lane_grp == g, gs[i + 1], 0.0).astype(BF16))

    tok_scr[...] = u_ref[...].astype(F32)
    nrow = u_ref.shape[0] // L
    u = jnp.concatenate([tok_scr[pl.ds(j, nrow, stride=L), :].astype(BF16) for j in range(L)], axis=1)
    st = jnp.dot(u, ws_scr[...], preferred_element_type=F32)

    kk = lax.broadcasted_iota(jnp.int32, (nrow, 1), 0) & (chunks_per_seq - 1)
    lane1 = lax.broadcasted_iota(jnp.int32, (1, LANES), 1)
    assert L // 2 == GROUPS_PER_BLOCK
    for g in range(GROUPS_PER_BLOCK):
        cols = slice(2 * g * LANES, (2 * g + 2) * LANES)
        kr = (2 * g + 2) * LANES
        y_scr[:, cols] = jnp.dot(u[:, :kr], wt_scr[:kr, cols], preferred_element_type=F32)

        xs = st[:, g * LANES:(g + 1) * LANES]
        zr = pw_g[L][0][g:g + 1]
        zi = pw_g[L][1][g:g + 1]
        d = 1
        while d < chunks_per_seq:
            sh = jnp.where(kk >= d, pltpu.roll(xs, d, axis=0), 0.0)
            zmix = jnp.where(lane1 < SSM_STATE, -zi, zi)
            xs = xs + sh * zr + pltpu.roll(sh, SSM_STATE, axis=1) * zmix
            zr, zi = zr * zr - zi * zi, 2.0 * zr * zi
            d *= 2
        prev = jnp.where(kk >= 1, pltpu.roll(xs, 1, axis=0), 0.0)
        xp_scr[:, g * LANES:(g + 1) * LANES] = prev.astype(BF16)

    xp = xp_scr[...]
    for p2 in range(L // 2):
        cols = slice(2 * p2 * LANES, (2 * p2 + 2) * LANES)
        y = y_scr[:, cols] + jnp.dot(xp, wo_scr[:, cols], preferred_element_type=F32)
        y = jax.nn.gelu(y + dsk_ref[:, cols] * u[:, cols].astype(F32))
        for i in range(2):
            tok_scr[pl.ds(2 * p2 + i, nrow, stride=L), :] = y[:, i * LANES:(i + 1) * LANES]
    o_ref[...] = tok_scr[...].astype(o_ref.dtype)


def _s5(proj, p_grp, p_bt, p_lanes, dsk, w_glu, w_out, chunks_per_seq):
    m = proj.shape[0]
    nrow, kk = m // SSM_CHUNK, SSM_CHUNK * LANES
    nst = GROUPS_PER_BLOCK * LANES
    ws_ = D_MODEL // N_SSM_BLOCKS
    wspec = pl.BlockSpec((ws_, D_MODEL), lambda i: (i, 0))
    return pl.pallas_call(
        functools.partial(_s5_kernel, chunks_per_seq=chunks_per_seq),
        grid=(N_SSM_BLOCKS,),
        in_specs=[pl.BlockSpec((m, LANES), lambda i: (0, OFF_SSM // LANES + i)),
                  pl.BlockSpec((None, 3, GROUPS_PER_BLOCK, LANES), lambda i: (i, 0, 0, 0)),
                  pl.BlockSpec((None, LANES, LANES), lambda i: (i, 0, 0)),
                  pl.BlockSpec((None, 5, SSM_STATE, LANES), lambda i: (i, 0, 0, 0)),
                  pl.BlockSpec((None, 1, kk), lambda i: (i, 0, 0)),
                  wspec, wspec],
        out_specs=[pl.BlockSpec((m, LANES), lambda i: (0, i)), wspec, wspec],
        out_shape=[jax.ShapeDtypeStruct((m, D_MODEL), BF16),
                   jax.ShapeDtypeStruct((D_MODEL, D_MODEL), BF16), jax.ShapeDtypeStruct((D_MODEL, D_MODEL), BF16)],
        scratch_shapes=[pltpu.VMEM((kk, kk), BF16), pltpu.VMEM((kk, nst), BF16),
                        pltpu.VMEM((nst, kk), BF16), pltpu.VMEM((nrow, nst), BF16),
                        pltpu.VMEM((m, LANES), F32), pltpu.VMEM((nrow, kk), F32)],
        compiler_params=_cparams(("parallel",)),
        name="s5",
    )(proj, p_grp, p_bt, p_lanes, dsk, w_glu, w_out)


def _mix_kernel(ys_ref, ya_ref, ga_ref, gb_ref, x_ref, wg_ref, bg_ref, wo_ref, gt_ref, gm_ref, sc_ref, sh_ref,
                wu_ref, wd_ref, h_ref, u_ref, wu_o, wd_o):
    wu_o[...] = wu_ref[...].astype(wu_o.dtype)
    wd_o[...] = wd_ref[...].astype(wd_o.dtype)
    ys = ys_ref[...]
    z = jnp.dot(ys, wg_ref[...], preferred_element_type=F32) + bg_ref[...]
    yb = ys.astype(F32) * jax.nn.sigmoid(z)
    mix = (jax.nn.sigmoid(ga_ref[...].astype(F32)) * ya_ref[...].astype(F32)
           + jax.nn.sigmoid(gb_ref[...].astype(F32)) * yb)
    h = x_ref[...] + gt_ref[...] * jnp.dot(mix.astype(BF16), wo_ref[...], preferred_element_type=F32)
    h_ref[...] = h
    ms = jnp.mean(h * h, axis=-1, keepdims=True)
    y = h * lax.rsqrt(ms + EPS) * gm_ref[...]
    u_ref[...] = (y * (1.0 + sc_ref[...]) + sh_ref[...]).astype(u_ref.dtype)


def _mix(ys, ya, proj, x2, w_glu, b_glu, w_out, g_mlp, ada3, w_up, w_down, seq):
    m = ys.shape[0]
    tm = MIX_TM
    per_b = seq // tm
    ff = w_up.shape[1]
    fs = ff // (m // tm)
    const = dict(pipeline_mode=pl.Buffered(1))
    row = lambda cb: pl.BlockSpec((tm, D_MODEL), lambda i: (i, cb))

    def ada_spec(k):
        return pl.BlockSpec((None, 1, D_MODEL), lambda i: ((i // per_b) * 6 + k, 0, 0))

    return pl.pallas_call(
        _mix_kernel,
        grid=(m // tm,),
        in_specs=[row(0), row(0), row(OFF_GA // D_MODEL), row(OFF_GB // D_MODEL), row(0),
                  pl.BlockSpec((D_MODEL, D_MODEL), lambda i: (0, 0), **const),
                  pl.BlockSpec((1, D_MODEL), lambda i: (0, 0)),
                  pl.BlockSpec((D_MODEL, D_MODEL), lambda i: (0, 0), **const),
                  ada_spec(2),
                  pl.BlockSpec((1, D_MODEL), lambda i: (0, 0)),
                  ada_spec(4), ada_spec(3),
                  pl.BlockSpec((D_MODEL, fs), lambda i: (0, i)),
                  pl.BlockSpec((fs, D_MODEL), lambda i: (i, 0))],
        out_specs=[row(0), row(0),
                   pl.BlockSpec((D_MODEL, fs), lambda i: (0, i)),
                   pl.BlockSpec((fs, D_MODEL), lambda i: (i, 0))],
        out_shape=[jax.ShapeDtypeStruct((m, D_MODEL), F32), jax.ShapeDtypeStruct((m, D_MODEL), BF16),
                   jax.ShapeDtypeStruct((D_MODEL, ff), BF16), jax.ShapeDtypeStruct((ff, D_MODEL), BF16)],
        compiler_params=_cparams(("parallel",)),
        name="mix",
    )(ys, ya, proj, proj, x2, w_glu, b_glu, w_out, ada3, g_mlp, ada3, ada3, w_up, w_down)


def _mlp_kernel(h_ref, u_ref, gt_ref, wu_ref, wd_ref, gf_ref, o_ref, acc_scr):
    f = pl.program_id(1)

    @pl.when(f == 0)
    def _():
        acc_scr[...] = jnp.zeros_like(acc_scr)

    a = jnp.dot(u_ref[...], wu_ref[...], preferred_element_type=F32)
    a = jnp.square(jnp.maximum(a, 0.0))
    acc_scr[...] += jnp.dot(a.astype(BF16), wd_ref[...], preferred_element_type=F32)

    @pl.when(f == pl.num_programs(1) - 1)
    def _():
        h2 = h_ref[...] + gt_ref[...] * acc_scr[...]
        ms = jnp.mean(h2 * h2, axis=-1, keepdims=True)
        o_ref[...] = h2 * lax.rsqrt(ms + EPS) * gf_ref[...]


def _mlp(h1, u2, ada3, w_up, w_down, g_final, seq):
    m = h1.shape[0]
    tm, tf = MLP_TM, MLP_TF
    per_b = seq // tm
    ff = w_up.shape[1]
    return pl.pallas_call(
        _mlp_kernel,
        grid=(m // tm, ff // tf),
        in_specs=[pl.BlockSpec((tm, D_MODEL), lambda i, f: (i, 0)),
                  pl.BlockSpec((tm, D_MODEL), lambda i, f: (i, 0)),
                  pl.BlockSpec((None, 1, D_MODEL), lambda i, f: ((i // per_b) * 6 + 5, 0, 0)),
                  pl.BlockSpec((D_MODEL, tf), lambda i, f: (0, f)),
                  pl.BlockSpec((tf, D_MODEL), lambda i, f: (f, 0)),
                  pl.BlockSpec((1, D_MODEL), lambda i, f: (0, 0))],
        out_specs=pl.BlockSpec((tm, D_MODEL), lambda i, f: (i, 0)),
        out_shape=jax.ShapeDtypeStruct((m, D_MODEL), F32),
        scratch_shapes=[pltpu.VMEM((tm, D_MODEL), F32)],
        compiler_params=_cparams(("parallel", "arbitrary")),
        name="mlp",
    )(h1, u2, ada3, w_up, w_down, g_final)


def _gate_rows(wt):
    return jnp.pad(wt[RAW_GATE:RAW_SSM], ((0, LANES - N_GATES), (0, 0))).astype(BF16)


def _s5_params(a_re, a_im, log_dt, b_re, b_im, c_re, c_im, d_skip):
    nb, gp, p, cg = N_SSM_BLOCKS, GROUPS_PER_BLOCK, SSM_STATE, SSM_GROUP
    ldt = jnp.broadcast_to(log_dt[:, None], (SSM_NGROUPS, p))

    def grp(a):
        return jnp.concatenate([a, a], axis=-1).reshape(nb, gp, LANES)

    def lanes(a):
        a = jnp.broadcast_to(a.reshape(nb, gp, p, 1).transpose(0, 2, 1, 3), (nb, p, gp, cg))
        return a.reshape(nb, p, LANES)

    def ct(cm):
        return cm.reshape(nb, gp, cg, p).transpose(0, 3, 1, 2).reshape(nb, p, LANES)

    p_grp = jnp.stack([grp(a_re), grp(a_im), grp(ldt)], axis=1)
    p_bt = jnp.stack([b_re, b_im], axis=1).transpose(0, 3, 1, 2).reshape(nb, LANES, LANES)
    p_lanes = jnp.stack([lanes(a_re), lanes(a_im), lanes(ldt), ct(c_re), ct(c_im)], axis=1)
    dsk = jnp.tile(d_skip.reshape(nb, 1, LANES), (1, 1, SSM_CHUNK))
    return p_grp, p_bt, p_lanes, dsk


def kernel(x, c, w_ada, b_ada, g_mix, w_in, w_ck1, w_ck2, pe_ck, w_cv1, w_cv2, pe_cv,
           a_re, a_im, log_dt, b_re, b_im, c_re, c_im, d_skip, w_glu, b_glu,
           w_out, g_mlp, w_up, w_down, g_final):
    bsz, seq, _ = x.shape
    assert w_ada.shape[0] == 1, "single-layer block"
    assert seq % 1024 == 0 and (seq & (seq - 1)) == 0
    m = bsz * seq
    x2 = x.reshape(m, D_MODEL)

    c_pad = jnp.pad(c, ((0, SUBLANES - bsz), (0, 0)))
    ada = _ada(c_pad, w_ada[0], b_ada)
    ada3 = ada[:bsz].reshape(bsz * 6, 1, D_MODEL)

    w_in_t = w_in[0].T
    proj, vt = _inproj(x2, g_mix, ada3, w_in_t, _gate_rows(w_in_t), seq)

    w1s = jnp.stack([w_ck1[0], w_cv1[0]]).reshape(2, CMP_LEN * HEAD_DIM, HEAD_DIM).astype(BF16)
    pes = jnp.stack([pe_ck[0], pe_cv[0]]).reshape(2, 1, CMP_LEN * HEAD_DIM)
    pes = jnp.pad(pes, ((0, 0), (0, SUBLANES - 1), (0, 0))).astype(BF16)
    w2s = jnp.stack([w_ck2[0], w_cv2[0]]).astype(BF16)
    kvc, kvct = _compress(proj, w1s, pes, w2s, bsz, seq)

    e_tab = (jnp.arange(seq)[:, None] // SEL_BLOCK == jnp.arange(LANES)[None, :]).astype(BF16)
    y_a = _nsa(proj, kvc, kvct, vt, e_tab, bsz, seq)

    p_grp, p_bt, p_lanes, dsk = _s5_params(a_re[0], a_im[0], log_dt[0], b_re[0], b_im[0],
                                           c_re[0], c_im[0], d_skip[0])
    ys, w_glu_b, w_out_b = _s5(proj, p_grp, p_bt, p_lanes, dsk, w_glu[0], w_out[0], seq // SSM_CHUNK)

    h1, u2, w_up_b, w_down_b = _mix(ys, y_a, proj, x2, w_glu_b, b_glu, w_out_b, g_mlp, ada3, w_up[0], w_down[0], seq)
    out = _mlp(h1, u2, ada3, w_up_b, w_down_b, g_final.reshape(1, D_MODEL), seq)
    return out.reshape(bsz, seq, D_MODEL)
```

```python
import functools

import jax
import jax.numpy as jnp
from jax import lax
from jax.experimental import pallas as pl
from jax.experimental.pallas import tpu as pltpu

F32 = jnp.float32
BF16 = jnp.bfloat16

D_MODEL = 2048
N_HEADS = 16
HEAD_DIM = 128
N_KV_GROUPS = 4
GROUP_SIZE = 4
KV_DIM = N_KV_GROUPS * HEAD_DIM
CMP_LEN = 32
CMP_STRIDE = 16
SEL_BLOCK = 64
SEL_TOPN = 16
WINDOW = 512
SSM_GROUP = 16
SSM_NGROUPS = D_MODEL // SSM_GROUP
SSM_STATE = 64
EPS = 1e-6
NEG = -1e30
BIG = 1e6
LOG2E = 1.4426950408889634
ONES_ROWS = 16

LANES = 128
SUBLANES = 8
SEL_SHIFT = SEL_BLOCK.bit_length() - 1
assert 1 << SEL_SHIFT == SEL_BLOCK
SSM_CHUNK = 16
GROUPS_PER_BLOCK = LANES // SSM_GROUP
N_SSM_BLOCKS = SSM_NGROUPS // GROUPS_PER_BLOCK

OFF_Q = 0
OFF_SSM = OFF_Q + D_MODEL
OFF_GA = OFF_SSM + D_MODEL
OFF_GB = OFF_GA + D_MODEL
OFF_KC = OFF_GB + D_MODEL
OFF_VC = OFF_KC + KV_DIM
OFF_KS = OFF_VC + KV_DIM
OFF_KW = OFF_KS + KV_DIM
PROJ_PAD = OFF_KW + KV_DIM
N_GATES = 3 * N_HEADS
VT_ROWS = 2 * KV_DIM + LANES
RAW_VS = D_MODEL + 3 * KV_DIM
RAW_KW = RAW_VS + KV_DIM
RAW_VW = RAW_KW + KV_DIM
RAW_GATE = RAW_VW + KV_DIM
RAW_SSM = RAW_GATE + 3 * N_HEADS

VMEM_LIMIT = 56 * 1024 * 1024

ADA_TN = 1024
INPROJ_TM, INPROJ_TN = 1024, 512
NSA_TQ, NSA_TK = 256, 512
MIX_TM = 256
MLP_TM, MLP_TF = 512, 1024


def _cparams(sem):
    return pltpu.CompilerParams(dimension_semantics=sem, vmem_limit_bytes=VMEM_LIMIT)


def _t(x):
    r, c = x.shape
    rows = []
    for j in range(c // LANES):
        rows.append(jnp.concatenate([x[i * LANES:(i + 1) * LANES, j * LANES:(j + 1) * LANES].T
                                     for i in range(r // LANES)], axis=1))
    return jnp.concatenate(rows, axis=0)


def _nt_dot(a, b):
    return lax.dot_general(a, b, (((1,), (1,)), ((), ())), preferred_element_type=F32)


def _ada_kernel(c_ref, w_ref, b_ref, o_ref):
    c = c_ref[...]
    cond = c * jax.nn.sigmoid(c)
    o_ref[...] = jnp.dot(cond.astype(BF16), w_ref[...].astype(BF16),
                         preferred_element_type=F32) + b_ref[...]


def _ada(c_pad, w_ada, b_ada):
    n = w_ada.shape[1]
    tn = ADA_TN
    return pl.pallas_call(
        _ada_kernel,
        grid=(n // tn,),
        in_specs=[pl.BlockSpec((SUBLANES, D_MODEL), lambda j: (0, 0)),
                  pl.BlockSpec((D_MODEL, tn), lambda j: (0, j)),
                  pl.BlockSpec((1, tn), lambda j: (0, j))],
        out_specs=pl.BlockSpec((SUBLANES, tn), lambda j: (0, j)),
        out_shape=jax.ShapeDtypeStruct((SUBLANES, n), F32),
        compiler_params=_cparams(("parallel",)),
        name="ada",
    )(c_pad, w_ada, b_ada)


_Q_TILES = D_MODEL // INPROJ_TN
_TAIL_TILES = 3 * D_MODEL // INPROJ_TN
_ROW_UNIT = 16


def _w_row(j):
    k = j - (_Q_TILES + _TAIL_TILES)
    kv_block = jnp.where(k < 3, 4 + k, 8)
    head = jnp.where(j < _Q_TILES, j, kv_block) * (INPROJ_TN // _ROW_UNIT)
    tail = RAW_SSM // _ROW_UNIT + (j - _Q_TILES) * (INPROJ_TN // _ROW_UNIT)
    return jnp.where((j >= _Q_TILES) & (k < 0), tail, head) * _ROW_UNIT


def _inproj_kernel(x_ref, g_ref, sc_ref, sh_ref, w_ref, wg_ref, wvs_ref, wvw_ref, o_ref, vt_ref, u_scr):
    @pl.when(pl.program_id(1) == 0)
    def _():
        x = x_ref[...]
        ms = jnp.mean(x * x, axis=-1, keepdims=True)
        y = x * lax.rsqrt(ms + EPS) * g_ref[...]
        u = (y * (1.0 + sc_ref[...]) + sh_ref[...]).astype(BF16)
        u_scr[...] = u
        vt_ref[:KV_DIM, :] = _nt_dot(wvs_ref[...].astype(BF16), u).astype(vt_ref.dtype)
        vt_ref[KV_DIM:2 * KV_DIM, :] = _nt_dot(wvw_ref[...].astype(BF16), u).astype(vt_ref.dtype)
        vt_ref[2 * KV_DIM:, :] = _nt_dot(wg_ref[...], u).astype(vt_ref.dtype)

    o_ref[...] = _nt_dot(u_scr[...], w_ref[...].astype(BF16)).astype(o_ref.dtype)


def _inproj(x2, g_mix, ada3, w_in_t, w_gate_t, seq):
    m = x2.shape[0]
    tm, tn = INPROJ_TM, INPROJ_TN
    per_b = seq // tm
    const = dict(pipeline_mode=pl.Buffered(1))
    wblk = (pl.Element(tn), pl.Element(D_MODEL))
    return pl.pallas_call(
        _inproj_kernel,
        grid=(m // tm, PROJ_PAD // tn),
        in_specs=[pl.BlockSpec((tm, D_MODEL), lambda i, j: (i, 0)),
                  pl.BlockSpec((1, D_MODEL), lambda i, j: (0, 0)),
                  pl.BlockSpec((None, 1, D_MODEL), lambda i, j: ((i // per_b) * 6 + 1, 0, 0)),
                  pl.BlockSpec((None, 1, D_MODEL), lambda i, j: ((i // per_b) * 6 + 0, 0, 0)),
                  pl.BlockSpec(wblk, lambda i, j: (_w_row(j), 0)),
                  pl.BlockSpec((LANES, D_MODEL), lambda i, j: (0, 0), **const),
                  pl.BlockSpec((tn, D_MODEL), lambda i, j: (RAW_VS // tn, 0), **const),
                  pl.BlockSpec((tn, D_MODEL), lambda i, j: (RAW_VW // tn, 0), **const)],
        out_specs=[pl.BlockSpec((tm, tn), lambda i, j: (i, j)),
                   pl.BlockSpec((VT_ROWS, tm), lambda i, j: (0, i))],
        out_shape=[jax.ShapeDtypeStruct((m, PROJ_PAD), BF16),
                   jax.ShapeDtypeStruct((VT_ROWS, m), BF16)],
        scratch_shapes=[pltpu.VMEM((tm, D_MODEL), BF16)],
        compiler_params=_cparams(("parallel", "arbitrary")),
        name="inproj",
    )(x2, g_mix, ada3, ada3, w_in_t, w_gate_t, w_in_t, w_in_t)


def _compress_kernel(x_ref, w1_ref, pe_ref, w2_ref, o_ref, ot_ref, xf_scr):
    xf_scr[...] = x_ref[...].astype(F32)
    nc = x_ref.shape[0] // CMP_STRIDE
    x = jnp.concatenate([xf_scr[pl.ds(l, nc, stride=CMP_STRIDE), :].astype(BF16)
                         for l in range(CMP_STRIDE)], axis=1)
    half = CMP_STRIDE * HEAD_DIM
    w1 = w1_ref[...]
    ha = jnp.dot(x, w1[:half], preferred_element_type=F32)
    hb = jnp.dot(x, w1[half:], preferred_element_type=F32)
    hb = pltpu.roll(hb, nc - 1, axis=0)
    pe_t = jnp.dot(pe_ref[...], w1, preferred_element_type=F32)[0:1]
    hid = jax.nn.gelu(ha + hb + pe_t)
    out = jnp.dot(hid.astype(BF16), w2_ref[...], preferred_element_type=F32)
    row = lax.broadcasted_iota(jnp.int32, out.shape, 0)
    out = jnp.where(row < nc - 1, out, 0.0)
    o_ref[...] = out.astype(o_ref.dtype)
    ot_ref[...] = _t(out).astype(ot_ref.dtype)


def _compress(proj, w1s, pes, w2s, bsz, seq):
    nc = seq // CMP_STRIDE
    g = N_KV_GROUPS
    return pl.pallas_call(
        _compress_kernel,
        grid=(2, bsz, g),
        in_specs=[pl.BlockSpec((seq, HEAD_DIM), lambda a, i, j: (i, OFF_KC // HEAD_DIM + a * g + j)),
                  pl.BlockSpec((None, CMP_LEN * HEAD_DIM, HEAD_DIM), lambda a, i, j: (a, 0, 0)),
                  pl.BlockSpec((None, SUBLANES, CMP_LEN * HEAD_DIM), lambda a, i, j: (a, 0, 0)),
                  pl.BlockSpec((None, HEAD_DIM, HEAD_DIM), lambda a, i, j: (a, 0, 0))],
        out_specs=[pl.BlockSpec((None, None, None, nc, HEAD_DIM), lambda a, i, j: (a, i, j, 0, 0)),
                   pl.BlockSpec((None, None, None, HEAD_DIM, nc), lambda a, i, j: (a, i, j, 0, 0))],
        out_shape=[jax.ShapeDtypeStruct((2, bsz, g, nc, HEAD_DIM), BF16),
                   jax.ShapeDtypeStruct((2, bsz, g, HEAD_DIM, nc), BF16)],
        scratch_shapes=[pltpu.VMEM((seq, HEAD_DIM), F32)],
        compiler_params=_cparams(("parallel", "parallel", "parallel")),
        name="compress",
    )(proj, w1s, pes, w2s)


NSA_STREAMS = 4


def _nsa_kernel(q_ref, gate_ref, kc_ref, vct_ref, ks_ref, e_ref, vst_ref, kw_ref, vwt_ref, o_ref,
                sa_scr, sb_scr, m_scr, acc_scr, *, tq, tk, seq):
    s0 = pl.program_id(2) * tq
    rows = GROUP_SIZE * tq
    gw = GROUP_SIZE * HEAD_DIM
    nsel = seq // SEL_BLOCK
    nc = kc_ref.shape[1]
    topn = min(SEL_TOPN, nsel)
    qscale = HEAD_DIM ** -0.5 * LOG2E
    streams = range(NSA_STREAMS)
    hd = lambda k: slice(k * HEAD_DIM, (k + 1) * HEAD_DIM)
    t_lane = s0 + (lax.broadcasted_iota(jnp.int32, (1, rows), 1) & (tq - 1))

    def reset_stats(k):
        m_scr[k] = jnp.full((1, rows), NEG, F32)
        acc_scr[k] = jnp.zeros((HEAD_DIM + ONES_ROWS, rows), F32)

    def update(k, s, off, vt_ref, mask):
        n = s.shape[0]
        if mask is not None:
            kpos = off + lax.broadcasted_iota(jnp.int32, (n, 1), 0)
            s = jnp.where(mask(kpos), s, NEG)
        m = m_scr[k]
        m_new = jnp.maximum(m, jnp.max(s, axis=0, keepdims=True))
        p = jnp.exp2(s - m_new).astype(BF16)
        vt1 = jnp.concatenate([vt_ref[hd(k), pl.ds(off, n)], jnp.ones((ONES_ROWS, n), BF16)], axis=0)
        acc_scr[k] = jnp.exp2(m - m_new) * acc_scr[k] + jnp.dot(vt1, p, preferred_element_type=F32)
        m_scr[k] = m_new

    def result(k):
        acc = acc_scr[k]
        return acc[:HEAD_DIM] * (1.0 / jnp.maximum(acc[HEAD_DIM:HEAD_DIM + 1], 1e-30))

    q4 = []
    for k in streams:
        q = q_ref[:, k * gw:(k + 1) * gw]
        qk = jnp.concatenate([q[:, hd(r)] for r in range(GROUP_SIZE)], axis=0)
        q4.append((qk.astype(F32) * qscale).astype(BF16))

    causal = lambda kpos: kpos <= t_lane

    for k in streams:
        reset_stats(k)
    n_back = WINDOW // tq
    win = []
    for c in range(n_back, 0, -1):
        off = s0 - c * tq
        if c == n_back:
            lo_edge = jnp.where(off >= 0, t_lane - (WINDOW - 1), seq)
        else:
            lo_edge = jnp.where(off >= 0, 0, seq)
        win.append((pl.multiple_of(jnp.maximum(off, 0), tq), lambda kpos, e=lo_edge: kpos >= e))
    win.append((pl.multiple_of(s0, tq), causal))

    def win_scores_into(k, dst, off):
        dst[k, :tq, :] = _nt_dot(kw_ref[pl.ds(off, tq), hd(k)], q4[k])

    bufs = (sa_scr, sb_scr)
    for k in streams:
        win_scores_into(k, bufs[0], win[0][0])
    for c, (off, mask) in enumerate(win):
        if c + 1 < len(win):
            for k in streams:
                win_scores_into(k, bufs[(c + 1) % 2], win[c + 1][0])
        for k in streams:
            update(k, bufs[c % 2][k, :tq, :], off, vwt_ref, mask)
    o_w = [result(k) for k in streams]

    n_col = lax.broadcasted_iota(jnp.int32, (nc, 1), 0)
    valid_c = (n_col * CMP_STRIDE + (CMP_LEN - 1)) <= t_lane
    jj = lax.broadcasted_iota(jnp.int32, (nsel, nc), 0)
    nn = lax.broadcasted_iota(jnp.int32, (nsel, nc), 1)
    ov = ((nn * CMP_STRIDE < jj * SEL_BLOCK + SEL_BLOCK)
          & (nn * CMP_STRIDE + CMP_LEN > jj * SEL_BLOCK)
          & (nn < nc - 1))
    ov = jnp.where(ov, 1.0, 0.0).astype(BF16)
    jcol = lax.broadcasted_iota(jnp.int32, (nsel, tq), 0)
    cur = lax.shift_right_logical(s0 + lax.broadcasted_iota(jnp.int32, (nsel, tq), 1), SEL_SHIFT)
    valid = jcol <= cur
    forced = (jcol == 0) | (jcol == cur) | (jcol == cur - 1)
    o_c, score = [], []
    for k in streams:
        sc = jnp.where(valid_c, _nt_dot(kc_ref[k], q4[k]), NEG)
        mc = jnp.max(sc, axis=0, keepdims=True)
        pc = jnp.exp2(sc - mc)
        inv = 1.0 / jnp.maximum(jnp.sum(pc, axis=0, keepdims=True), 1e-30)
        pc = pc * jnp.where(t_lane >= CMP_LEN - 1, inv, 0.0)
        o_c.append(jnp.dot(vct_ref[k], pc.astype(BF16), preferred_element_type=F32))
        psum = pc[:, 0:tq] + pc[:, tq:2 * tq] + pc[:, 2 * tq:3 * tq] + pc[:, 3 * tq:4 * tq]
        p_hi = psum.astype(BF16)
        p_lo = (psum - p_hi.astype(F32)).astype(BF16)
        imp_t = (jnp.dot(ov, p_hi, preferred_element_type=F32)
                 + jnp.dot(ov, p_lo, preferred_element_type=F32))
        score.append(jnp.where(forced, BIG, jnp.where(valid, imp_t, -BIG)))

    jcol_f = jcol.astype(F32)
    picked = [jnp.zeros((nsel, tq), F32) for _ in streams]
    for _ in range(topn):
        for k in streams:
            best = jnp.max(score[k], axis=0, keepdims=True)
            first = jnp.min(jnp.where(score[k] == best, jcol_f, float(nsel)), axis=0, keepdims=True)
            hit = jcol_f == first
            picked[k] = jnp.where(hit, 1.0, picked[k])
            score[k] = jnp.where(hit, -jnp.inf, score[k])

    q_aug = []
    for k in streams:
        bias_t = jnp.where((picked[k] > 0.5) & valid, 0.0, NEG)
        bias_t = jnp.concatenate([bias_t, jnp.zeros((LANES - nsel, tq), F32)], axis=0)
        bias_q = _t(bias_t).astype(BF16)
        q_aug.append(jnp.concatenate([q4[k], jnp.concatenate([bias_q] * GROUP_SIZE, axis=0)], axis=1))

    def chunk_off(kb):
        return pl.multiple_of(kb * tk, tk)

    def sel_scores_into(k, dst, kb):
        off = chunk_off(kb)
        keys = jnp.concatenate([ks_ref[pl.ds(off, tk), hd(k)], e_ref[pl.ds(off, tk), :]], axis=1)
        dst[k] = _nt_dot(keys, q_aug[k])

    hi = (s0 + tq - 1) // tk + 1
    for k in streams:
        reset_stats(k)
    for k in streams:
        sel_scores_into(k, sa_scr, 0)
    n_pairs = lax.shift_right_logical(hi - 1, 1)

    def pair(i, c):
        k0 = 2 * i
        for k in streams:
            sel_scores_into(k, sb_scr, k0 + 1)
        for k in streams:
            update(k, sa_scr[k], chunk_off(k0), vst_ref, None)
        for k in streams:
            sel_scores_into(k, sa_scr, k0 + 2)
        for k in streams:
            update(k, sb_scr[k], chunk_off(k0 + 1), vst_ref, None)
        return c

    lax.fori_loop(0, n_pairs, pair, 0)

    last_off = chunk_off(hi - 1)
    odd_tail = ((hi - 1) & 1) == 1
    full_last = s0 > last_off

    def tail(n_last, prefetch):
        if prefetch:
            for k in streams:
                keys = jnp.concatenate([ks_ref[pl.ds(last_off, n_last), hd(k)], e_ref[pl.ds(last_off, n_last), :]],
                                       axis=1)
                sb_scr[k, :n_last, :] = _nt_dot(keys, q_aug[k])
            for k in streams:
                update(k, sa_scr[k], chunk_off(hi - 2), vst_ref, None)
        last = sb_scr if prefetch else sa_scr
        for k in streams:
            update(k, last[k, :n_last, :], last_off, vst_ref, causal)

    pl.when(odd_tail & full_last)(lambda: tail(tk, True))
    pl.when(odd_tail & jnp.logical_not(full_last))(lambda: tail(tq, True))
    pl.when(jnp.logical_not(odd_tail) & full_last)(lambda: tail(tk, False))
    pl.when(jnp.logical_not(odd_tail) & jnp.logical_not(full_last))(lambda: tail(tq, False))

    sg = jax.nn.sigmoid(gate_ref[...].astype(F32))
    for k in streams:
        o_s = result(k)
        outs = []
        for r in range(GROUP_SIZE):
            sl = slice(r * tq, (r + 1) * tq)
            g0 = 3 * (k * GROUP_SIZE + r)
            o_r = (sg[g0:g0 + 1] * o_c[k][:, sl] + sg[g0 + 1:g0 + 2] * o_s[:, sl]
                   + sg[g0 + 2:g0 + 3] * o_w[k][:, sl])
            outs.append(_t(o_r))
        o_ref[:, k * gw:(k + 1) * gw] = jnp.concatenate(outs, axis=1).astype(o_ref.dtype)


def _nsa(proj, kvc, kvct, vt, e_tab, bsz, seq):
    tq, tk = NSA_TQ, NSA_TK
    assert WINDOW % tq == 0 and tk == 2 * tq
    ns = NSA_STREAMS
    assert ns == N_KV_GROUPS
    nq = seq // tq
    nc = kvc.shape[3]
    gw = ns * GROUP_SIZE * HEAD_DIM
    kw_ = ns * HEAD_DIM
    rows = GROUP_SIZE * tq
    n_vs_blocks = KV_DIM // kw_

    once = dict(pipeline_mode=pl.Buffered(1))

    def k_spec(off):
        cb = off // kw_
        return pl.BlockSpec((seq, kw_), lambda b, g, i: (b, cb + g), **once)

    return pl.pallas_call(
        functools.partial(_nsa_kernel, tq=tq, tk=tk, seq=seq),
        grid=(bsz, N_KV_GROUPS // ns, nq),
        in_specs=[pl.BlockSpec((tq, gw), lambda b, g, i: (b * nq + i, g)),
                  pl.BlockSpec((LANES, tq), lambda b, g, i: (2 * KV_DIM // LANES, b * nq + i)),
                  pl.BlockSpec((None, None, ns, nc, HEAD_DIM), lambda b, g, i: (0, b, g, 0, 0)),
                  pl.BlockSpec((None, None, ns, HEAD_DIM, nc), lambda b, g, i: (1, b, g, 0, 0)),
                  k_spec(OFF_KS),
                  pl.BlockSpec((seq, LANES), lambda b, g, i: (0, 0), **once),
                  pl.BlockSpec((kw_, seq), lambda b, g, i: (g, b), **once),
                  k_spec(OFF_KW),
                  pl.BlockSpec((kw_, seq), lambda b, g, i: (n_vs_blocks + g, b), **once)],
        out_specs=pl.BlockSpec((tq, gw), lambda b, g, i: (b * nq + i, g)),
        out_shape=jax.ShapeDtypeStruct((bsz * seq, D_MODEL), BF16),
        scratch_shapes=[pltpu.VMEM((ns, tk, rows), F32), pltpu.VMEM((ns, tk, rows), F32),
                        pltpu.VMEM((ns, 1, rows), F32),
                        pltpu.VMEM((ns, HEAD_DIM + ONES_ROWS, rows), F32)],
        compiler_params=_cparams(("parallel", "parallel", "arbitrary")),
        name="nsa",
    )(proj, vt, kvc, kvct, proj, e_tab, vt, proj, vt)


def _s5_kernel(u_ref, pg_ref, bt_ref, pl_ref, dsk_ref, wgl_ref, wou_ref, o_ref, wgl_o, wou_o,
               wt_scr, ws_scr, wo_scr, xp_scr, tok_scr, y_scr, *, chunks_per_seq):
    L = SSM_CHUNK
    hp = lax.Precision.HIGHEST
    wgl_o[...] = wgl_ref[...].astype(wgl_o.dtype)
    wou_o[...] = wou_ref[...].astype(wou_o.dtype)
    are_g, aim_g, ldt_g = pg_ref[0], pg_ref[1], pg_ref[2]
    bt = bt_ref[...]
    are_l, aim_l, ldt_l, cr_l, ci_l = pl_ref[0], pl_ref[1], pl_ref[2], pl_ref[3], pl_ref[4]

    lane = lax.broadcasted_iota(jnp.int32, (LANES, LANES), 1)
    rowi = lax.broadcasted_iota(jnp.int32, (LANES, LANES), 0)
    is_re = lane < SSM_STATE
    row_grp = rowi // SSM_GROUP
    lane_grp = lane // SSM_GROUP

    def rows16(z):
        return jnp.concatenate([jnp.broadcast_to(z[g:g + 1], (SSM_GROUP, LANES))
                                for g in range(GROUPS_PER_BLOCK)], axis=0)

    def cmul(x, zr, zi):
        return x * zr + pltpu.roll(x, SSM_STATE, axis=1) * jnp.where(is_re, -zi, zi)

    dt_g = jnp.exp(ldt_g)
    l1r = jnp.exp(are_g * dt_g) * jnp.cos(aim_g * dt_g)
    l1i = jnp.exp(are_g * dt_g) * jnp.sin(aim_g * dt_g)
    den = are_g * are_g + aim_g * aim_g
    nr = l1r - 1.0
    coef_r = (nr * are_g + l1i * aim_g) / den
    coef_i = (l1i * are_g - nr * aim_g) / den
    bbar = cmul(bt, rows16(coef_r), rows16(coef_i))

    pw_g = [(jnp.ones_like(l1r), jnp.zeros_like(l1r))]
    for _ in range(L):
        pr, pi = pw_g[-1]
        pw_g.append((pr * l1r - pi * l1i, pr * l1i + pi * l1r))

    dt_l = jnp.exp(ldt_l)
    m1r = jnp.exp(are_l * dt_l) * jnp.cos(aim_l * dt_l)
    m1i = jnp.exp(are_l * dt_l) * jnp.sin(aim_l * dt_l)
    gs = []
    pr, pi = jnp.ones_like(m1r), jnp.zeros_like(m1r)
    for _ in range(L + 1):
        gs.append(jnp.concatenate([cr_l * pr - ci_l * pi, -(cr_l * pi + ci_l * pr)], axis=0))
        pr, pi = pr * m1r - pi * m1i, pr * m1i + pi * m1r

    same_grp = row_grp == lane_grp
    zeros_tile = jnp.zeros((LANES, LANES), BF16)

    for t in range(L):
        d_t = jnp.dot(bbar, gs[t], precision=hp, preferred_element_type=F32)
        d_t = jnp.where(same_grp, d_t, 0.0).astype(BF16)
        for j in range(L - t):
            i = j + t
            wt_scr[j * LANES:(j + 1) * LANES, i * LANES:(i + 1) * LANES] = d_t
    for p2 in range(L // 2):
        wt_scr[(2 * p2 + 1) * LANES:(2 * p2 + 2) * LANES, 2 * p2 * LANES:(2 * p2 + 1) * LANES] = zeros_tile

    ws_scr[...] = jnp.zeros(ws_scr.shape, BF16)
    for j in range(L):
        zr, zi = pw_g[L - 1 - j]
        s_j = cmul(bbar, rows16(zr), rows16(zi)).astype(BF16)
        for g in range(GROUPS_PER_BLOCK):
            r0 = j * LANES + g * SSM_GROUP
            ws_scr[r0:r0 + SSM_GROUP, g * LANES:(g + 1) * LANES] = s_j[g * SSM_GROUP:(g + 1) * SSM_GROUP]

    for g in range(GROUPS_PER_BLOCK):
        for i in range(L):
            wo_scr[g * LANES:(g + 1) * LANES, i * LANES:(i + 1) * LANES] = (
                jnp.where(lane_grp == g, gs[i + 1], 0.0).astype(BF16))

    tok_scr[...] = u_ref[...].astype(F32)
    nrow = u_ref.shape[0] // L
    u = jnp.concatenate([tok_scr[pl.ds(j, nrow, stride=L), :].astype(BF16) for j in range(L)], axis=1)
    st = jnp.dot(u, ws_scr[...], preferred_element_type=F32)

    kk = lax.broadcasted_iota(jnp.int32, (nrow, 1), 0) & (chunks_per_seq - 1)
    lane1 = lax.broadcasted_iota(jnp.int32, (1, LANES), 1)
    assert L // 2 == GROUPS_PER_BLOCK
    for g in range(GROUPS_PER_BLOCK):
        cols = slice(2 * g * LANES, (2 * g + 2) * LANES)
        kr = (2 * g + 2) * LANES
        y_scr[:, cols] = jnp.dot(u[:, :kr], wt_scr[:kr, cols], preferred_element_type=F32)

        xs = st[:, g * LANES:(g + 1) * LANES]
        zr = pw_g[L][0][g:g + 1]
        zi = pw_g[L][1][g:g + 1]
        d = 1
        while d < chunks_per_seq:
            sh = jnp.where(kk >= d, pltpu.roll(xs, d, axis=0), 0.0)
            zmix = jnp.where(lane1 < SSM_STATE, -zi, zi)
            xs = xs + sh * zr + pltpu.roll(sh, SSM_STATE, axis=1) * zmix
            zr, zi = zr * zr - zi * zi, 2.0 * zr * zi
            d *= 2
        prev = jnp.where(kk >= 1, pltpu.roll(xs, 1, axis=0), 0.0)
        xp_scr[:, g * LANES:(g + 1) * LANES] = prev.astype(BF16)

    xp = xp_scr[...]
    for p2 in range(L // 2):
        cols = slice(2 * p2 * LANES, (2 * p2 + 2) * LANES)
        y = y_scr[:, cols] + jnp.dot(xp, wo_scr[:, cols], preferred_element_type=F32)
        y = jax.nn.gelu(y + dsk_ref[:, cols] * u[:, cols].astype(F32))
        for i in range(2):
            tok_scr[pl.ds(2 * p2 + i, nrow, stride=L), :] = y[:, i * LANES:(i + 1) * LANES]
    o_ref[...] = tok_scr[...].astype(o_ref.dtype)


def _s5(proj, p_grp, p_bt, p_lanes, dsk, w_glu, w_out, chunks_per_seq):
    m = proj.shape[0]
    nrow, kk = m // SSM_CHUNK, SSM_CHUNK * LANES
    nst = GROUPS_PER_BLOCK * LANES
    ws_ = D_MODEL // N_SSM_BLOCKS
    wspec = pl.BlockSpec((ws_, D_MODEL), lambda i: (i, 0))
    return pl.pallas_call(
        functools.partial(_s5_kernel, chunks_per_seq=chunks_per_seq),
        grid=(N_SSM_BLOCKS,),
        in_specs=[pl.BlockSpec((m, LANES), lambda i: (0, OFF_SSM // LANES + i)),
                  pl.BlockSpec((None, 3, GROUPS_PER_BLOCK, LANES), lambda i: (i, 0, 0, 0)),
                  pl.BlockSpec((None, LANES, LANES), lambda i: (i, 0, 0)),
                  pl.BlockSpec((None, 5, SSM_STATE, LANES), lambda i: (i, 0, 0, 0)),
                  pl.BlockSpec((None, 1, kk), lambda i: (i, 0, 0)),
                  wspec, wspec],
        out_specs=[pl.BlockSpec((m, LANES), lambda i: (0, i)), wspec, wspec],
        out_shape=[jax.ShapeDtypeStruct((m, D_MODEL), BF16),
                   jax.ShapeDtypeStruct((D_MODEL, D_MODEL), BF16), jax.ShapeDtypeStruct((D_MODEL, D_MODEL), BF16)],
        scratch_shapes=[pltpu.VMEM((kk, kk), BF16), pltpu.VMEM((kk, nst), BF16),
                        pltpu.VMEM((nst, kk), BF16), pltpu.VMEM((nrow, nst), BF16),
                        pltpu.VMEM((m, LANES), F32), pltpu.VMEM((nrow, kk), F32)],
        compiler_params=_cparams(("parallel",)),
        name="s5",
    )(proj, p_grp, p_bt, p_lanes, dsk, w_glu, w_out)


def _mix_kernel(ys_ref, ya_ref, ga_ref, gb_ref, x_ref, wg_ref, bg_ref, wo_ref, gt_ref, gm_ref, sc_ref, sh_ref,
                wu_ref, wd_ref, h_ref, u_ref, wu_o, wd_o):
    wu_o[...] = wu_ref[...].astype(wu_o.dtype)
    wd_o[...] = wd_ref[...].astype(wd_o.dtype)
    ys = ys_ref[...]
    z = jnp.dot(ys, wg_ref[...], preferred_element_type=F32) + bg_ref[...]
    yb = ys.astype(F32) * jax.nn.sigmoid(z)
    mix = (jax.nn.sigmoid(ga_ref[...].astype(F32)) * ya_ref[...].astype(F32)
           + jax.nn.sigmoid(gb_ref[...].astype(F32)) * yb)
    h = x_ref[...] + gt_ref[...] * jnp.dot(mix.astype(BF16), wo_ref[...], preferred_element_type=F32)
    h_ref[...] = h
    ms = jnp.mean(h * h, axis=-1, keepdims=True)
    y = h * lax.rsqrt(ms + EPS) * gm_ref[...]
    u_ref[...] = (y * (1.0 + sc_ref[...]) + sh_ref[...]).astype(u_ref.dtype)


def _mix(ys, ya, proj, x2, w_glu, b_glu, w_out, g_mlp, ada3, w_up, w_down, seq):
    m = ys.shape[0]
    tm = MIX_TM
    per_b = seq // tm
    ff = w_up.shape[1]
    fs = ff // (m // tm)
    const = dict(pipeline_mode=pl.Buffered(1))
    row = lambda cb: pl.BlockSpec((tm, D_MODEL), lambda i: (i, cb))

    def ada_spec(k):
        return pl.BlockSpec((None, 1, D_MODEL), lambda i: ((i // per_b) * 6 + k, 0, 0))

    return pl.pallas_call(
        _mix_kernel,
        grid=(m // tm,),
        in_specs=[row(0), row(0), row(OFF_GA // D_MODEL), row(OFF_GB // D_MODEL), row(0),
                  pl.BlockSpec((D_MODEL, D_MODEL), lambda i: (0, 0), **const),
                  pl.BlockSpec((1, D_MODEL), lambda i: (0, 0)),
                  pl.BlockSpec((D_MODEL, D_MODEL), lambda i: (0, 0), **const),
                  ada_spec(2),
                  pl.BlockSpec((1, D_MODEL), lambda i: (0, 0)),
                  ada_spec(4), ada_spec(3),
                  pl.BlockSpec((D_MODEL, fs), lambda i: (0, i)),
                  pl.BlockSpec((fs, D_MODEL), lambda i: (i, 0))],
        out_specs=[row(0), row(0),
                   pl.BlockSpec((D_MODEL, fs), lambda i: (0, i)),
                   pl.BlockSpec((fs, D_MODEL), lambda i: (i, 0))],
        out_shape=[jax.ShapeDtypeStruct((m, D_MODEL), F32), jax.ShapeDtypeStruct((m, D_MODEL), BF16),
                   jax.ShapeDtypeStruct((D_MODEL, ff), BF16), jax.ShapeDtypeStruct((ff, D_MODEL), BF16)],
        compiler_params=_cparams(("parallel",)),
        name="mix",
    )(ys, ya, proj, proj, x2, w_glu, b_glu, w_out, ada3, g_mlp, ada3, ada3, w_up, w_down)


def _mlp_kernel(h_ref, u_ref, gt_ref, wu_ref, wd_ref, gf_ref, o_ref, acc_scr):
    f = pl.program_id(1)

    @pl.when(f == 0)
    def _():
        acc_scr[...] = jnp.zeros_like(acc_scr)

    a = jnp.dot(u_ref[...], wu_ref[...], preferred_element_type=F32)
    a = jnp.square(jnp.maximum(a, 0.0))
    acc_scr[...] += jnp.dot(a.astype(BF16), wd_ref[...], preferred_element_type=F32)

    @pl.when(f == pl.num_programs(1) - 1)
    def _():
        h2 = h_ref[...] + gt_ref[...] * acc_scr[...]
        ms = jnp.mean(h2 * h2, axis=-1, keepdims=True)
        o_ref[...] = h2 * lax.rsqrt(ms + EPS) * gf_ref[...]


def _mlp(h1, u2, ada3, w_up, w_down, g_final, seq):
    m = h1.shape[0]
    tm, tf = MLP_TM, MLP_TF
    per_b = seq // tm
    ff = w_up.shape[1]
    return pl.pallas_call(
        _mlp_kernel,
        grid=(m // tm, ff // tf),
        in_specs=[pl.BlockSpec((tm, D_MODEL), lambda i, f: (i, 0)),
                  pl.BlockSpec((tm, D_MODEL), lambda i, f: (i, 0)),
                  pl.BlockSpec((None, 1, D_MODEL), lambda i, f: ((i // per_b) * 6 + 5, 0, 0)),
                  pl.BlockSpec((D_MODEL, tf), lambda i, f: (0, f)),
                  pl.BlockSpec((tf, D_MODEL), lambda i, f: (f, 0)),
                  pl.BlockSpec((1, D_MODEL), lambda i, f: (0, 0))],
        out_specs=pl.BlockSpec((tm, D_MODEL), lambda i, f: (i, 0)),
        out_shape=jax.ShapeDtypeStruct((m, D_MODEL), F32),
        scratch_shapes=[pltpu.VMEM((tm, D_MODEL), F32)],
        compiler_params=_cparams(("parallel", "arbitrary")),
        name="mlp",
    )(h1, u2, ada3, w_up, w_down, g_final)


def _gate_rows(wt):
    return jnp.pad(wt[RAW_GATE:RAW_SSM], ((0, LANES - N_GATES), (0, 0))).astype(BF16)


def _s5_params(a_re, a_im, log_dt, b_re, b_im, c_re, c_im, d_skip):
    nb, gp, p, cg = N_SSM_BLOCKS, GROUPS_PER_BLOCK, SSM_STATE, SSM_GROUP
    ldt = jnp.broadcast_to(log_dt[:, None], (SSM_NGROUPS, p))

    def grp(a):
        return jnp.concatenate([a, a], axis=-1).reshape(nb, gp, LANES)

    def lanes(a):
        a = jnp.broadcast_to(a.reshape(nb, gp, p, 1).transpose(0, 2, 1, 3), (nb, p, gp, cg))
        return a.reshape(nb, p, LANES)

    def ct(cm):
        return cm.reshape(nb, gp, cg, p).transpose(0, 3, 1, 2).reshape(nb, p, LANES)

    p_grp = jnp.stack([grp(a_re), grp(a_im), grp(ldt)], axis=1)
    p_bt = jnp.stack([b_re, b_im], axis=1).transpose(0, 3, 1, 2).reshape(nb, LANES, LANES)
    p_lanes = jnp.stack([lanes(a_re), lanes(a_im), lanes(ldt), ct(c_re), ct(c_im)], axis=1)
    dsk = jnp.tile(d_skip.reshape(nb, 1, LANES), (1, 1, SSM_CHUNK))
    return p_grp, p_bt, p_lanes, dsk


def kernel(x, c, w_ada, b_ada, g_mix, w_in, w_ck1, w_ck2, pe_ck, w_cv1, w_cv2, pe_cv,
           a_re, a_im, log_dt, b_re, b_im, c_re, c_im, d_skip, w_glu, b_glu,
           w_out, g_mlp, w_up, w_down, g_final):
    bsz, seq, _ = x.shape
    assert w_ada.shape[0] == 1, "single-layer block"
    assert seq % 1024 == 0 and (seq & (seq - 1)) == 0
    m = bsz * seq
    x2 = x.reshape(m, D_MODEL)

    c_pad = jnp.pad(c, ((0, SUBLANES - bsz), (0, 0)))
    ada = _ada(c_pad, w_ada[0], b_ada)
    ada3 = ada[:bsz].reshape(bsz * 6, 1, D_MODEL)

    w_in_t = w_in[0].T
    proj, vt = _inproj(x2, g_mix, ada3, w_in_t, _gate_rows(w_in_t), seq)

    w1s = jnp.stack([w_ck1[0], w_cv1[0]]).reshape(2, CMP_LEN * HEAD_DIM, HEAD_DIM).astype(BF16)
    pes = jnp.stack([pe_ck[0], pe_cv[0]]).reshape(2, 1, CMP_LEN * HEAD_DIM)
    pes = jnp.pad(pes, ((0, 0), (0, SUBLANES - 1), (0, 0))).astype(BF16)
    w2s = jnp.stack([w_ck2[0], w_cv2[0]]).astype(BF16)
    kvc, kvct = _compress(proj, w1s, pes, w2s, bsz, seq)

    e_tab = (jnp.arange(seq)[:, None] // SEL_BLOCK == jnp.arange(LANES)[None, :]).astype(BF16)
    y_a = _nsa(proj, kvc, kvct, vt, e_tab, bsz, seq)

    p_grp, p_bt, p_lanes, dsk = _s5_params(a_re[0], a_im[0], log_dt[0], b_re[0], b_im[0],
                                           c_re[0], c_im[0], d_skip[0])
    ys, w_glu_b, w_out_b = _s5(proj, p_grp, p_bt, p_lanes, dsk, w_glu[0], w_out[0], seq // SSM_CHUNK)

    h1, u2, w_up_b, w_down_b = _mix(ys, y_a, proj, x2, w_glu_b, b_glu, w_out_b, g_mlp, ada3, w_up[0], w_down[0], seq)
    out = _mlp(h1, u2, ada3, w_up_b, w_down_b, g_final.reshape(1, D_MODEL), seq)
    return out.reshape(bsz, seq, D_MODEL)
```
